```python
import jax, jax.numpy as jnp
from jax import lax
import numpy as np

D_MODEL = 1024
BATCH = 8
SEQ = 2048
DEPTH = 2
DEC_BATCH = 128
DEC_SEQ = 8
PAST_LEN = 8192
PAGE_SIZE = 128

N_MIXERS = 2
POOL_WINDOWS = (2, 4, 8, 16)
N_POOL_GROUPS = 4
POOL_GROUP_DIM = D_MODEL // N_POOL_GROUPS
POOL_PREFIX = max(POOL_WINDOWS) - 1
N_HEADS = 16
N_KV_HEADS = 4
HEAD_DIM = 64
GROUP = N_HEADS // N_KV_HEADS
WINDOW = 128
ROPE_THETA = 10000.0
D_FF = 2816
Q_DIM = N_HEADS * HEAD_DIM
KV_DIM = N_KV_HEADS * HEAD_DIM
QKV_DIM = Q_DIM + 2 * KV_DIM
RMS_EPS = 1e-6
NEG_INF = -1e30

kernel_name = "hybrid_pool_swa_sink_macaron_step"


def rmsnorm(x, g):
    xf = x.astype(jnp.float32)
    y = xf * lax.rsqrt(jnp.mean(xf * xf, axis=-1, keepdims=True) + RMS_EPS)
    return (y * g.astype(jnp.float32)).astype(x.dtype)


def half_ffn(x, g, w_in, w_out):
    h = rmsnorm(x, g) @ w_in
    gate, up = h[..., :D_FF], h[..., D_FF:]
    return x + 0.5 * ((jax.nn.silu(gate) * up) @ w_out)


def pool_mix(u, prefix, start_pos, w, scale):
    T = u.shape[1]
    P = POOL_PREFIX
    ext = jnp.concatenate([prefix.astype(u.dtype), u], axis=1)
    c = jnp.cumsum(ext.astype(jnp.float32), axis=1)
    c = jnp.pad(c, ((0, 0), (1, 0), (0, 0)))
    n_seen = start_pos + jnp.arange(T) + 1
    outs = []
    for gi, wg in enumerate(POOL_WINDOWS):
        sl = slice(gi * POOL_GROUP_DIM, (gi + 1) * POOL_GROUP_DIM)
        s = c[:, P + 1:P + T + 1, sl] - c[:, P + 1 - wg:P + T + 1 - wg, sl]
        cnt = jnp.minimum(n_seen, wg).astype(jnp.float32)[:, None]
        p = s / cnt - u[..., sl].astype(jnp.float32)
        outs.append(jnp.einsum("btc,cd->btd", p, w[gi].astype(jnp.float32)))
    y = jnp.concatenate(outs, axis=-1) * scale.astype(jnp.float32)
    return y.astype(u.dtype), ext[:, -P:]


def rope(x, pos):
    half = HEAD_DIM // 2
    inv = ROPE_THETA ** (-jnp.arange(half, dtype=jnp.float32) / half)
    ang = pos.astype(jnp.float32)[:, None] * inv[None, :]
    cos = jnp.cos(ang)[:, None, :]
    sin = jnp.sin(ang)[:, None, :]
    xf = x.astype(jnp.float32)
    x1, x2 = xf[..., :half], xf[..., half:]
    return jnp.concatenate([x1 * cos - x2 * sin, x2 * cos + x1 * sin], axis=-1).astype(x.dtype)


def project_qkv(u, pos, w_qkv, b_qkv):
    h = u @ w_qkv + b_qkv
    lead = u.shape[:-1]
    q = h[..., :Q_DIM].reshape(*lead, N_HEADS, HEAD_DIM)
    k = h[..., Q_DIM:Q_DIM + KV_DIM].reshape(*lead, N_KV_HEADS, HEAD_DIM)
    v = h[..., Q_DIM + KV_DIM:].reshape(*lead, N_KV_HEADS, HEAD_DIM)
    return rope(q, pos), rope(k, pos), v


def band_mask(qpos, kpos):
    d = qpos[..., :, None] - kpos[..., None, :]
    return (d >= 0) & (d < WINDOW) & (kpos[..., None, :] >= 0)


def attend_with_sinks(q, k, v, mask, sinks):
    s = jnp.einsum("...qhgd,...khd->...hgqk", q, k).astype(jnp.float32) * (HEAD_DIM ** -0.5)
    s = jnp.where(mask, s, NEG_INF)
    sink = sinks.astype(jnp.float32).reshape(N_KV_HEADS, GROUP, 1, 1)
    m = jnp.maximum(jnp.max(s, axis=-1, keepdims=True), sink)
    e = jnp.exp(s - m)
    p = e / (jnp.sum(e, axis=-1, keepdims=True) + jnp.exp(sink - m))
    return jnp.einsum("...hgqk,...khd->...qhgd", p.astype(v.dtype), v)


def swa_prompt(u, w_qkv, b_qkv, w_o, b_o, sinks):
    B, T, _ = u.shape
    nb = T // WINDOW
    pos = jnp.arange(T)
    q, k, v = project_qkv(u, pos, w_qkv, b_qkv)
    qb = q.reshape(B, nb, WINDOW, N_KV_HEADS, GROUP, HEAD_DIM)
    pad = jnp.zeros((B, WINDOW, N_KV_HEADS, HEAD_DIM), k.dtype)
    kb = jnp.concatenate([pad, k], axis=1).reshape(B, nb + 1, WINDOW, N_KV_HEADS, HEAD_DIM)
    vb = jnp.concatenate([pad, v], axis=1).reshape(B, nb + 1, WINDOW, N_KV_HEADS, HEAD_DIM)
    kband = jnp.concatenate([kb[:, :-1], kb[:, 1:]], axis=2)
    vband = jnp.concatenate([vb[:, :-1], vb[:, 1:]], axis=2)
    qpos = pos.reshape(nb, WINDOW)
    kpos = jnp.arange(nb)[:, None] * WINDOW - WINDOW + jnp.arange(2 * WINDOW)[None, :]
    mask = band_mask(qpos, kpos)[None, :, None, None]
    o = attend_with_sinks(qb, kband, vband, mask, sinks).reshape(B, T, Q_DIM)
    return o @ w_o + b_o, k[:, -WINDOW:], v[:, -WINDOW:]


def swa_sample(u, k_buf, v_buf, w_qkv, b_qkv, w_o, b_o, sinks):
    B, T, _ = u.shape
    pos = PAST_LEN + jnp.arange(T)
    q, k, v = project_qkv(u, pos, w_qkv, b_qkv)
    kk = jnp.concatenate([k_buf.astype(k.dtype), k], axis=1)
    vv = jnp.concatenate([v_buf.astype(v.dtype), v], axis=1)
    kpos = PAST_LEN - WINDOW + jnp.arange(WINDOW + T)
    mask = band_mask(pos, kpos)[None, None, None]
    qg = q.reshape(B, T, N_KV_HEADS, GROUP, HEAD_DIM)
    o = attend_with_sinks(qg, kk, vv, mask, sinks).reshape(B, T, Q_DIM)
    return o @ w_o + b_o, kk[:, -WINDOW:], vv[:, -WINDOW:]


def setup_inputs(seed: int = 0) -> dict:
    key = jax.random.key(seed)
    ks = jax.random.split(key, 24)
    f32 = jnp.float32
    n_pool = (DEPTH + 1) // 2
    n_attn = DEPTH // 2
    nrm = lambda k, s: jax.random.normal(k, s, f32)
    return {
        "x_prompt": nrm(ks[0], (BATCH, SEQ, D_MODEL)),
        "x_sample": nrm(ks[1], (DEC_BATCH, DEC_SEQ, D_MODEL)),
        "state_pool": nrm(ks[2], (n_pool, DEC_BATCH, POOL_PREFIX, D_MODEL)),
        "cache_k": nrm(ks[3], (n_attn, DEC_BATCH, WINDOW, N_KV_HEADS, HEAD_DIM)),
        "cache_v": nrm(ks[4], (n_attn, DEC_BATCH, WINDOW, N_KV_HEADS, HEAD_DIM)),
        "norm_ffn1": 1.0 + 0.05 * nrm(ks[5], (DEPTH, D_MODEL)),
        "ffn1_w_in": nrm(ks[6], (DEPTH, D_MODEL, 2 * D_FF)) * D_MODEL ** -0.5,
        "ffn1_w_out": nrm(ks[7], (DEPTH, D_FF, D_MODEL)) * D_FF ** -0.5,
        "norm_mix": 1.0 + 0.05 * nrm(ks[8], (DEPTH, D_MODEL)),
        "norm_ffn2": 1.0 + 0.05 * nrm(ks[9], (DEPTH, D_MODEL)),
        "ffn2_w_in": nrm(ks[10], (DEPTH, D_MODEL, 2 * D_FF)) * D_MODEL ** -0.5,
        "ffn2_w_out": nrm(ks[11], (DEPTH, D_FF, D_MODEL)) * D_FF ** -0.5,
        "pool_w": nrm(ks[12], (n_pool, N_POOL_GROUPS, POOL_GROUP_DIM, POOL_GROUP_DIM)) * POOL_GROUP_DIM ** -0.5,
        "pool_scale": 1.0 + 0.05 * nrm(ks[13], (n_pool, D_MODEL)),
        "attn_w_qkv": nrm(ks[14], (n_attn, D_MODEL, QKV_DIM)) * D_MODEL ** -0.5,
        "attn_b_qkv": 0.02 * nrm(ks[15], (n_attn, QKV_DIM)),
        "attn_w_o": nrm(ks[16], (n_attn, Q_DIM, D_MODEL)) * Q_DIM ** -0.5,
        "attn_b_o": 0.02 * nrm(ks[17], (n_attn, D_MODEL)),
        "attn_sinks": nrm(ks[18], (n_attn, N_HEADS)),
        "norm_final": 1.0 + 0.05 * nrm(ks[19], (D_MODEL,)),
    }


def reference(x_prompt, x_sample, state_pool, cache_k, cache_v,
              norm_ffn1, ffn1_w_in, ffn1_w_out, norm_mix, norm_ffn2, ffn2_w_in, ffn2_w_out,
              pool_w, pool_scale, attn_w_qkv, attn_b_qkv, attn_w_o, attn_b_o, attn_sinks,
              norm_final):
    xp, xs = x_prompt, x_sample
    pool_p, pool_s, kp_l, vp_l, ks_l, vs_l = [], [], [], [], [], []
    for i in range(DEPTH):
        xp = half_ffn(xp, norm_ffn1[i], ffn1_w_in[i], ffn1_w_out[i])
        xs = half_ffn(xs, norm_ffn1[i], ffn1_w_in[i], ffn1_w_out[i])
        up = rmsnorm(xp, norm_mix[i])
        us = rmsnorm(xs, norm_mix[i])
        j = i // N_MIXERS
        if i % N_MIXERS == 0:
            zeros = jnp.zeros((up.shape[0], POOL_PREFIX, D_MODEL), up.dtype)
            yp, sp = pool_mix(up, zeros, 0, pool_w[j], pool_scale[j])
            ys, ss = pool_mix(us, state_pool[j], PAST_LEN, pool_w[j], pool_scale[j])
            pool_p.append(sp)
            pool_s.append(ss)
        else:
            yp, kp, vp = swa_prompt(up, attn_w_qkv[j], attn_b_qkv[j], attn_w_o[j], attn_b_o[j], attn_sinks[j])
            ys, kn, vn = swa_sample(us, cache_k[j], cache_v[j], attn_w_qkv[j], attn_b_qkv[j],
                                    attn_w_o[j], attn_b_o[j], attn_sinks[j])
            kp_l.append(kp)
            vp_l.append(vp)
            ks_l.append(kn)
            vs_l.append(vn)
        xp = xp + yp
        xs = xs + ys
        xp = half_ffn(xp, norm_ffn2[i], ffn2_w_in[i], ffn2_w_out[i])
        xs = half_ffn(xs, norm_ffn2[i], ffn2_w_in[i], ffn2_w_out[i])
    y_prompt = rmsnorm(xp, norm_final)
    y_sample = rmsnorm(xs, norm_final)
    new_pool_prompt = jnp.stack(pool_p)
    new_pool_sample = jnp.stack(pool_s)
    new_k_prompt = jnp.stack(kp_l)
    new_v_prompt = jnp.stack(vp_l)
    new_k_sample = jnp.stack(ks_l)
    new_v_sample = jnp.stack(vs_l)
    return (y_prompt, y_sample, new_pool_prompt, new_pool_sample,
            new_k_prompt, new_v_prompt, new_k_sample, new_v_sample)
```

```python
import functools

import jax
import jax.numpy as jnp
from jax import lax
from jax.experimental import pallas as pl
from jax.experimental.pallas import tpu as pltpu

D_MODEL = 1024
D_FF = 2816
POOL_WINDOWS = (2, 4, 8, 16)
POOL_GROUP_DIM = D_MODEL // len(POOL_WINDOWS)
POOL_PREFIX = max(POOL_WINDOWS) - 1
N_HEADS = 16
N_KV_HEADS = 4
HEAD_DIM = 64
GROUP = N_HEADS // N_KV_HEADS
WINDOW = 128
ROPE_THETA = 10000.0
Q_DIM = N_HEADS * HEAD_DIM
KV_DIM = N_KV_HEADS * HEAD_DIM
QKV_DIM = Q_DIM + 2 * KV_DIM
RMS_EPS = 1e-6
NEG_INF = -1e30
PAST_LEN = 8192

LANES = 128
HEADS_PER_VREG = LANES // HEAD_DIM
POOL_HALO = 16
FF_CHUNK = 256
VMEM_LIMIT = 52 * 1024 * 1024

F32 = jnp.float32
BF16 = jnp.bfloat16


def _rms(x, g):
    ms = jnp.mean(x * x, axis=-1, keepdims=True)
    return x * lax.rsqrt(ms + RMS_EPS) * g


def _resident(shape):
    nd = len(shape)
    return pl.BlockSpec(shape, lambda *_: (0,) * nd, pipeline_mode=pl.Buffered(1))


def _ffn_kernel(x_ref, g_ref, win_ref, wout_ref, *rest, final_norm):
    if final_norm:
        gf_ref, o_ref, h_ref = rest
    else:
        o_ref, h_ref = rest
    x = x_ref[...]
    xn = _rms(x, g_ref[...]).astype(BF16)
    for c in range(D_FF // FF_CHUNK):
        lo = c * FF_CHUNK
        gate = jnp.dot(xn, win_ref[:, lo:lo + FF_CHUNK], preferred_element_type=F32)
        up = jnp.dot(xn, win_ref[:, D_FF + lo:D_FF + lo + FF_CHUNK], preferred_element_type=F32)
        h_ref[:, lo:lo + FF_CHUNK] = (gate * jax.nn.sigmoid(gate) * up).astype(BF16)
    y = x + 0.5 * jnp.dot(h_ref[...], wout_ref[...], preferred_element_type=F32)
    if final_norm:
        y = _rms(y, gf_ref[...])
    o_ref[...] = y


def _ffn(x, g, w_in, w_out, final_gain=None, *, tm):
    n = x.shape[0]
    final_norm = final_gain is not None
    in_specs = [
        pl.BlockSpec((tm, D_MODEL), lambda i: (i, 0)),
        _resident((1, D_MODEL)),
        _resident((D_MODEL, 2 * D_FF)),
        _resident((D_FF, D_MODEL)),
    ]
    args = [x, g.reshape(1, D_MODEL), w_in, w_out]
    if final_norm:
        in_specs.append(_resident((1, D_MODEL)))
        args.append(final_gain.reshape(1, D_MODEL))
    return pl.pallas_call(
        functools.partial(_ffn_kernel, final_norm=final_norm),
        out_shape=jax.ShapeDtypeStruct((n, D_MODEL), F32),
        grid=(n // tm,),
        in_specs=in_specs,
        out_specs=pl.BlockSpec((tm, D_MODEL), lambda i: (i, 0)),
        scratch_shapes=[pltpu.VMEM((tm, D_FF), BF16)],
        compiler_params=pltpu.CompilerParams(
            dimension_semantics=("arbitrary",), vmem_limit_bytes=VMEM_LIMIT),
        name="ffn_final" if final_norm else "ffn",
    )(*args)


def _pool_kernel(x_ref, *rest, has_prefix, start_pos, tt):
    if has_prefix:
        pre_ref, g_ref, w_ref, sc_ref, o_ref, st_ref, ext_ref = rest
    else:
        g_ref, w_ref, sc_ref, o_ref, st_ref, ext_ref = rest
    ti = pl.program_id(1)
    bb = x_ref.shape[0]

    @pl.when(ti == 0)
    def _():
        ext_ref[:, 0:POOL_HALO, :] = jnp.zeros((bb, POOL_HALO, D_MODEL), F32)
        if has_prefix:
            ext_ref[:, POOL_HALO - POOL_PREFIX:POOL_HALO, :] = pre_ref[...]

    if tt >= POOL_HALO:
        @pl.when(ti > 0)
        def _():
            ext_ref[:, 0:POOL_HALO, :] = ext_ref[:, tt:tt + POOL_HALO, :]

    x = x_ref[...]
    u = _rms(x, g_ref[...])
    ext_ref[:, POOL_HALO:POOL_HALO + tt, :] = u

    n_seen = start_pos + ti * tt + lax.broadcasted_iota(jnp.int32, (1, tt, 1), 1) + 1
    for gi, wg in enumerate(POOL_WINDOWS):
        sl = slice(gi * POOL_GROUP_DIM, (gi + 1) * POOL_GROUP_DIM)
        s = ext_ref[:, POOL_HALO:POOL_HALO + tt, sl]
        for k in range(1, wg):
            s = s + ext_ref[:, POOL_HALO - k:POOL_HALO - k + tt, sl]
        cnt = jnp.minimum(n_seen, wg).astype(F32)
        p = s / cnt - u[:, :, sl]
        p2 = p.reshape(bb * tt, POOL_GROUP_DIM).astype(BF16)
        y = jnp.dot(p2, w_ref[gi], preferred_element_type=F32).reshape(bb, tt, POOL_GROUP_DIM)
        o_ref[:, :, sl] = x[:, :, sl] + y * sc_ref[:, :, sl]

    @pl.when(ti == pl.num_programs(1) - 1)
    def _():
        st_ref[...] = ext_ref[:, tt + POOL_HALO - POOL_PREFIX:tt + POOL_HALO, :]


def _pool(x, prefix, g, w, scale, *, start_pos, bb, tt):
    b, t, _ = x.shape
    has_prefix = prefix is not None
    in_specs = [pl.BlockSpec((bb, tt, D_MODEL), lambda i, j: (i, j, 0))]
    args = [x]
    if has_prefix:
        in_specs.append(pl.BlockSpec((bb, POOL_PREFIX, D_MODEL), lambda i, j: (i, 0, 0)))
        args.append(prefix)
    in_specs += [
        _resident((1, 1, D_MODEL)),
        _resident((len(POOL_WINDOWS), POOL_GROUP_DIM, POOL_GROUP_DIM)),
        _resident((1, 1, D_MODEL)),
    ]
    args += [g.reshape(1, 1, D_MODEL), w, scale.reshape(1, 1, D_MODEL)]
    return pl.pallas_call(
        functools.partial(_pool_kernel, has_prefix=has_prefix, start_pos=start_pos, tt=tt),
        out_shape=(jax.ShapeDtypeStruct((b, t, D_MODEL), F32),
                   jax.ShapeDtypeStruct((b, POOL_PREFIX, D_MODEL), F32)),
        grid=(b // bb, t // tt),
        in_specs=in_specs,
        out_specs=(pl.BlockSpec((bb, tt, D_MODEL), lambda i, j: (i, j, 0)),
                   pl.BlockSpec((bb, POOL_PREFIX, D_MODEL), lambda i, j: (i, 0, 0))),
        scratch_shapes=[pltpu.VMEM((bb, tt + POOL_HALO, D_MODEL), F32)],
        compiler_params=pltpu.CompilerParams(
            dimension_semantics=("arbitrary", "arbitrary"), vmem_limit_bytes=VMEM_LIMIT),
        name="pool_sample" if has_prefix else "pool_prompt",
    )(*args)


def _rope_tables(pos):
    half = HEAD_DIM // 2
    inv = ROPE_THETA ** (-jnp.arange(half, dtype=F32) / half)
    ang = pos.astype(F32)[:, None] * inv[None, :]
    cos, sin = jnp.cos(ang), jnp.sin(ang)
    cos64 = jnp.concatenate([cos, cos], axis=-1)
    sin64 = jnp.concatenate([-sin, sin], axis=-1)
    return jnp.tile(cos64, (1, HEADS_PER_VREG)), jnp.tile(sin64, (1, HEADS_PER_VREG))


def _rope_col(xc, cos, sin, first_half):
    half = HEAD_DIM // 2
    partner = jnp.where(first_half, pltpu.roll(xc, LANES - half, axis=1), pltpu.roll(xc, half, axis=1))
    return xc * cos + partner * sin


def _head_half_variants(col, lo_half):
    sw = pltpu.roll(col, HEAD_DIM, axis=1)
    zero = jnp.zeros_like(col)
    return {
        (0, 0): jnp.where(lo_half, col, zero),
        (0, 1): jnp.where(lo_half, zero, sw),
        (1, 0): jnp.where(lo_half, sw, zero),
        (1, 1): jnp.where(lo_half, zero, col),
    }


def _attn_prompt_kernel(x_ref, cos_ref, sin_ref, g_ref, wqkv_ref, bqkv_ref, wo_ref, bo_ref, sink_ref,
                        o_ref, kout_ref, vout_ref, kvar_ref, vstk_ref):
    j = pl.program_id(1)
    w = WINDOW

    @pl.when(j == 0)
    def _():
        kvar_ref[:, 0:w, :] = jnp.zeros((2 * N_KV_HEADS, w, LANES), BF16)
        vstk_ref[:, 0:w, :] = jnp.zeros((N_KV_HEADS, w, LANES), BF16)
        vstk_ref[:, 2 * w:3 * w, :] = jnp.zeros((N_KV_HEADS, w, LANES), BF16)

    @pl.when(j > 0)
    def _():
        kvar_ref[:, 0:w, :] = kvar_ref[:, w:2 * w, :]
        vstk_ref[:, 0:w, :] = vstk_ref[:, w:2 * w, :]
        vstk_ref[:, 2 * w:3 * w, :] = vstk_ref[:, 3 * w:4 * w, :]

    x = x_ref[0]
    u = _rms(x, g_ref[...]).astype(BF16)
    qkv = jnp.dot(u, wqkv_ref[...], preferred_element_type=F32) + bqkv_ref[...]
    cos, sin = cos_ref[...], sin_ref[...]
    lane = lax.broadcasted_iota(jnp.int32, (w, LANES), 1)
    first_half = (lane % HEAD_DIM) < (HEAD_DIM // 2)
    lo_half = lane < HEAD_DIM

    qcols = []
    for c in range(Q_DIM // LANES):
        qc = _rope_col(qkv[:, c * LANES:(c + 1) * LANES], cos, sin, first_half)
        qcols.append((qc * (HEAD_DIM ** -0.5)).astype(BF16))

    for c in range(KV_DIM // LANES):
        kc = _rope_col(qkv[:, Q_DIM + c * LANES:Q_DIM + (c + 1) * LANES], cos, sin, first_half)
        vc = qkv[:, Q_DIM + KV_DIM + c * LANES:Q_DIM + KV_DIM + (c + 1) * LANES]
        kout_ref[0, :, c * LANES:(c + 1) * LANES] = kc
        vout_ref[0, :, c * LANES:(c + 1) * LANES] = vc
        kv = _head_half_variants(kc, lo_half)
        vv = _head_half_variants(vc, lo_half)
        for hb in range(HEADS_PER_VREG):
            kvh = c * HEADS_PER_VREG + hb
            for a in range(HEADS_PER_VREG):
                kvar_ref[kvh * HEADS_PER_VREG + a, w:2 * w, :] = kv[(hb, a)].astype(BF16)
            vstk_ref[kvh, w:2 * w, :] = vv[(hb, 0)].astype(BF16)
            vstk_ref[kvh, 3 * w:4 * w, :] = vv[(hb, 1)].astype(BF16)

    row = lax.broadcasted_iota(jnp.int32, (2 * w, 2 * w), 0) % w
    col = lax.broadcasted_iota(jnp.int32, (2 * w, 2 * w), 1)
    first_key = jnp.where(j > 0, 0, w)
    valid = (col > row) & (col <= row + w) & (col >= first_key)
    top_rows = lax.broadcasted_iota(jnp.int32, (2 * w, 1), 0) < w

    ocols = [None] * (Q_DIM // LANES)
    for kvh in range(N_KV_HEADS):
        es, rinvs = [], []
        for a in range(HEADS_PER_VREG):
            qop = jnp.concatenate([qcols[2 * kvh], qcols[2 * kvh + 1]], axis=0)
            s = lax.dot_general(qop, kvar_ref[kvh * HEADS_PER_VREG + a],
                                (((1,), (1,)), ((), ())), preferred_element_type=F32)
            s = jnp.where(valid, s, NEG_INF)
            h0 = kvh * GROUP + a
            sink = jnp.where(top_rows, sink_ref[h0], sink_ref[h0 + HEADS_PER_VREG])
            m = jnp.maximum(jnp.max(s, axis=-1, keepdims=True), sink)
            e = jnp.exp(s - m)
            denom = jnp.sum(e, axis=-1, keepdims=True) + jnp.exp(sink - m)
            es.append(e.astype(BF16))
            rinvs.append(1.0 / denom)
        for mm in range(2):
            rows = slice(mm * w, (mm + 1) * w)
            lhs = jnp.concatenate([es[0][rows], es[1][rows]], axis=1)
            o = jnp.dot(lhs, vstk_ref[kvh], preferred_element_type=F32)
            ocols[2 * kvh + mm] = o * jnp.where(lo_half, rinvs[0][rows], rinvs[1][rows])

    o_all = jnp.concatenate(ocols, axis=1).astype(BF16)
    o_ref[0] = x + jnp.dot(o_all, wo_ref[...], preferred_element_type=F32) + bo_ref[...]


def _attn_prompt(x, g, w_qkv, b_qkv, w_o, b_o, sinks):
    b, t, _ = x.shape
    nb = t // WINDOW
    cos, sin = _rope_tables(jnp.arange(t))
    return pl.pallas_call(
        _attn_prompt_kernel,
        out_shape=(jax.ShapeDtypeStruct((b, t, D_MODEL), F32),
                   jax.ShapeDtypeStruct((b, WINDOW, KV_DIM), F32),
                   jax.ShapeDtypeStruct((b, WINDOW, KV_DIM), F32)),
        grid=(b, nb),
        in_specs=[
            pl.BlockSpec((1, WINDOW, D_MODEL), lambda i, j: (i, j, 0)),
            pl.BlockSpec((WINDOW, LANES), lambda i, j: (j, 0)),
            pl.BlockSpec((WINDOW, LANES), lambda i, j: (j, 0)),
            _resident((1, D_MODEL)),
            _resident((D_MODEL, QKV_DIM)),
            _resident((1, QKV_DIM)),
            _resident((Q_DIM, D_MODEL)),
            _resident((1, D_MODEL)),
            pl.BlockSpec(memory_space=pltpu.SMEM),
        ],
        out_specs=(pl.BlockSpec((1, WINDOW, D_MODEL), lambda i, j: (i, j, 0)),
                   pl.BlockSpec((1, WINDOW, KV_DIM), lambda i, j: (i, 0, 0)),
                   pl.BlockSpec((1, WINDOW, KV_DIM), lambda i, j: (i, 0, 0))),
        scratch_shapes=[pltpu.VMEM((2 * N_KV_HEADS, 2 * WINDOW, LANES), BF16),
                        pltpu.VMEM((N_KV_HEADS, 4 * WINDOW, LANES), BF16)],
        compiler_params=pltpu.CompilerParams(
            dimension_semantics=("arbitrary", "arbitrary"), vmem_limit_bytes=VMEM_LIMIT),
        name="attn_prompt",
    )(x, cos, sin, g.reshape(1, D_MODEL), w_qkv, b_qkv.reshape(1, QKV_DIM), w_o,
      b_o.reshape(1, D_MODEL), sinks)


SAMPLE_KEYS = 2 * WINDOW


def _attn_sample_kernel(x_ref, ck_ref, cv_ref, cos_ref, sin_ref, g_ref, wqkv_ref, bqkv_ref, wo_ref, bo_ref,
                        sinkcol_ref, o_ref, kout_ref, vout_ref, q_s, kn_s, vn_s, kk_s, vv_s, oa_s, *, dt):
    bb = x_ref.shape[0]
    w = WINDOW
    x = x_ref[...].reshape(bb * dt, D_MODEL)
    u = _rms(x, g_ref[...]).astype(BF16)
    qkv = jnp.dot(u, wqkv_ref[...], preferred_element_type=F32) + bqkv_ref[...]
    cos, sin = cos_ref[...], sin_ref[...]
    lane_t = lax.broadcasted_iota(jnp.int32, (bb * dt, LANES), 1)
    first_half = (lane_t % HEAD_DIM) < (HEAD_DIM // 2)
    for c in range(Q_DIM // LANES):
        qc = _rope_col(qkv[:, c * LANES:(c + 1) * LANES], cos, sin, first_half)
        q_s[:, c * LANES:(c + 1) * LANES] = qc * (HEAD_DIM ** -0.5)
    for c in range(KV_DIM // LANES):
        kn_s[:, c * LANES:(c + 1) * LANES] = _rope_col(
            qkv[:, Q_DIM + c * LANES:Q_DIM + (c + 1) * LANES], cos, sin, first_half)
    vn_s[...] = qkv[:, Q_DIM + KV_DIM:]

    kk_s[w + dt:, :] = jnp.zeros((SAMPLE_KEYS - w - dt, KV_DIM), BF16)
    vv_s[w + dt:, :] = jnp.zeros((SAMPLE_KEYS - w - dt, KV_DIM), BF16)

    heads_per_col = N_HEADS // (KV_DIM // LANES)
    rows = heads_per_col * dt
    lane8 = lax.broadcasted_iota(jnp.int32, (dt, LANES), 1)
    lo8 = lane8 < HEAD_DIM
    t_row = lax.broadcasted_iota(jnp.int32, (rows, SAMPLE_KEYS), 0) % dt
    key = lax.broadcasted_iota(jnp.int32, (rows, SAMPLE_KEYS), 1)
    valid = (key > t_row) & (key <= t_row + w)

    def body(b, carry):
        r0 = pl.multiple_of(b * dt, dt)
        knew = kn_s[pl.ds(r0, dt), :]
        vnew = vn_s[pl.ds(r0, dt), :]
        kout_ref[b, 0:w - dt, :] = ck_ref[b, dt:w, :]
        kout_ref[b, w - dt:w, :] = knew
        vout_ref[b, 0:w - dt, :] = cv_ref[b, dt:w, :]
        vout_ref[b, w - dt:w, :] = vnew
        kk_s[0:w, :] = ck_ref[b].astype(BF16)
        kk_s[w:w + dt, :] = knew.astype(BF16)
        vv_s[0:w, :] = cv_ref[b].astype(BF16)
        vv_s[w:w + dt, :] = vnew.astype(BF16)
        for c in range(KV_DIM // LANES):
            pieces = []
            for h8 in range(heads_per_col):
                qcol_idx = c * (heads_per_col // HEADS_PER_VREG) + h8 // HEADS_PER_VREG
                qcol = q_s[pl.ds(r0, dt), qcol_idx * LANES:(qcol_idx + 1) * LANES]
                a, hb = h8 % HEADS_PER_VREG, h8 // GROUP
                src = qcol if a == hb else pltpu.roll(qcol, HEAD_DIM, axis=1)
                keep = lo8 if hb == 0 else jnp.logical_not(lo8)
                pieces.append(jnp.where(keep, src, 0.0))
            qop = jnp.concatenate(pieces, axis=0).astype(BF16)
            s = lax.dot_general(qop, kk_s[:, c * LANES:(c + 1) * LANES],
                                (((1,), (1,)), ((), ())), preferred_element_type=F32)
            s = jnp.where(valid, s, NEG_INF)
            sink = sinkcol_ref[c * rows:(c + 1) * rows, :]
            m = jnp.maximum(jnp.max(s, axis=-1, keepdims=True), sink)
            e = jnp.exp(s - m)
            denom = jnp.sum(e, axis=-1, keepdims=True) + jnp.exp(sink - m)
            o = jnp.dot(e.astype(BF16), vv_s[:, c * LANES:(c + 1) * LANES], preferred_element_type=F32)
            o = o * (1.0 / denom)
            for mm in range(heads_per_col // HEADS_PER_VREG):
                halves = []
                for a in range(HEADS_PER_VREG):
                    h8 = mm * HEADS_PER_VREG + a
                    hb = h8 // GROUP
                    piece = o[h8 * dt:(h8 + 1) * dt, :]
                    halves.append(piece if a == hb else pltpu.roll(piece, HEAD_DIM, axis=1))
                ocol_idx = c * (heads_per_col // HEADS_PER_VREG) + mm
                oa_s[pl.ds(r0, dt), ocol_idx * LANES:(ocol_idx + 1) * LANES] = jnp.where(lo8, halves[0], halves[1])
        return carry

    lax.fori_loop(0, bb, body, 0)

    y = x + jnp.dot(oa_s[...].astype(BF16), wo_ref[...], preferred_element_type=F32) + bo_ref[...]
    o_ref[...] = y.reshape(bb, dt, D_MODEL)


def _attn_sample(x, cache_k, cache_v, g, w_qkv, b_qkv, w_o, b_o, sinks, *, bb):
    b, dt, _ = x.shape
    cos, sin = _rope_tables(PAST_LEN + jnp.arange(dt))
    cos, sin = jnp.tile(cos, (bb, 1)), jnp.tile(sin, (bb, 1))
    sinkcol = jnp.repeat(sinks, dt).reshape(N_HEADS * dt, 1)
    n = bb * dt
    return pl.pallas_call(
        functools.partial(_attn_sample_kernel, dt=dt),
        out_shape=(jax.ShapeDtypeStruct((b, dt, D_MODEL), F32),
                   jax.ShapeDtypeStruct((b, WINDOW, KV_DIM), F32),
                   jax.ShapeDtypeStruct((b, WINDOW, KV_DIM), F32)),
        grid=(b // bb,),
        in_specs=[
            pl.BlockSpec((bb, dt, D_MODEL), lambda i: (i, 0, 0)),
            pl.BlockSpec((bb, WINDOW, KV_DIM), lambda i: (i, 0, 0)),
            pl.BlockSpec((bb, WINDOW, KV_DIM), lambda i: (i, 0, 0)),
            _resident((n, LANES)),
            _resident((n, LANES)),
            _resident((1, D_MODEL)),
            _resident((D_MODEL, QKV_DIM)),
            _resident((1, QKV_DIM)),
            _resident((Q_DIM, D_MODEL)),
            _resident((1, D_MODEL)),
            _resident((N_HEADS * dt, 1)),
        ],
        out_specs=(pl.BlockSpec((bb, dt, D_MODEL), lambda i: (i, 0, 0)),
                   pl.BlockSpec((bb, WINDOW, KV_DIM), lambda i: (i, 0, 0)),
                   pl.BlockSpec((bb, WINDOW, KV_DIM), lambda i: (i, 0, 0))),
        scratch_shapes=[pltpu.VMEM((n, Q_DIM), F32),
                        pltpu.VMEM((n, KV_DIM), F32),
                        pltpu.VMEM((n, KV_DIM), F32),
                        pltpu.VMEM((SAMPLE_KEYS, KV_DIM), BF16),
                        pltpu.VMEM((SAMPLE_KEYS, KV_DIM), BF16),
                        pltpu.VMEM((n, Q_DIM), F32)],
        compiler_params=pltpu.CompilerParams(
            dimension_semantics=("arbitrary",), vmem_limit_bytes=VMEM_LIMIT),
        name="attn_sample",
    )(x, cache_k, cache_v, cos, sin, g.reshape(1, D_MODEL), w_qkv, b_qkv.reshape(1, QKV_DIM), w_o,
      b_o.reshape(1, D_MODEL), sinkcol)


def kernel(x_prompt, x_sample, state_pool, cache_k, cache_v, norm_ffn1, ffn1_w_in, ffn1_w_out, norm_mix,
           norm_ffn2, ffn2_w_in, ffn2_w_out, pool_w, pool_scale, attn_w_qkv, attn_b_qkv, attn_w_o, attn_b_o,
           attn_sinks, norm_final):
    batch, seq, _ = x_prompt.shape
    dec_batch, dec_seq, _ = x_sample.shape
    depth = norm_ffn1.shape[0]
    n_mixers = 2
    xp = x_prompt.reshape(batch * seq, D_MODEL)
    xs = x_sample.reshape(dec_batch * dec_seq, D_MODEL)
    pool_p, pool_s, kp_l, vp_l, ks_l, vs_l = [], [], [], [], [], []
    for i in range(depth):
        w_in, w_out = ffn1_w_in[i].astype(BF16), ffn1_w_out[i].astype(BF16)
        xp = _ffn(xp, norm_ffn1[i], w_in, w_out, tm=512)
        xs = _ffn(xs, norm_ffn1[i], w_in, w_out, tm=512)
        j = i // n_mixers
        xp3 = xp.reshape(batch, seq, D_MODEL)
        xs3 = xs.reshape(dec_batch, dec_seq, D_MODEL)
        if i % n_mixers == 0:
            pw = pool_w[j].astype(BF16)
            xp3, sp = _pool(xp3, None, norm_mix[i], pw, pool_scale[j], start_pos=0, bb=1, tt=256)
            xs3, ss = _pool(xs3, state_pool[j], norm_mix[i], pw, pool_scale[j], start_pos=PAST_LEN,
                            bb=32, tt=dec_seq)
            pool_p.append(sp)
            pool_s.append(ss)
        else:
            wqkv, wo = attn_w_qkv[j].astype(BF16), attn_w_o[j].astype(BF16)
            xp3, kp, vp = _attn_prompt(xp3, norm_mix[i], wqkv, attn_b_qkv[j], wo, attn_b_o[j], attn_sinks[j])
            xs3, kn, vn = _attn_sample(
                xs3, cache_k[j].reshape(dec_batch, WINDOW, KV_DIM), cache_v[j].reshape(dec_batch, WINDOW, KV_DIM),
                norm_mix[i], wqkv, attn_b_qkv[j], wo, attn_b_o[j], attn_sinks[j], bb=32)
            kp_l.append(kp.reshape(batch, WINDOW, N_KV_HEADS, HEAD_DIM))
            vp_l.append(vp.reshape(batch, WINDOW, N_KV_HEADS, HEAD_DIM))
            ks_l.append(kn.reshape(dec_batch, WINDOW, N_KV_HEADS, HEAD_DIM))
            vs_l.append(vn.reshape(dec_batch, WINDOW, N_KV_HEADS, HEAD_DIM))
        xp = xp3.reshape(batch * seq, D_MODEL)
        xs = xs3.reshape(dec_batch * dec_seq, D_MODEL)
        w_in, w_out = ffn2_w_in[i].astype(BF16), ffn2_w_out[i].astype(BF16)
        fg = norm_final if i == depth - 1 else None
        xp = _ffn(xp, norm_ffn2[i], w_in, w_out, fg, tm=512)
        xs = _ffn(xs, norm_ffn2[i], w_in, w_out, fg, tm=512)
    return (xp.reshape(batch, seq, D_MODEL), xs.reshape(dec_batch, dec_seq, D_MODEL),
            jnp.stack(pool_p), jnp.stack(pool_s), jnp.stack(kp_l), jnp.stack(vp_l),
            jnp.stack(ks_l), jnp.stack(vs_l))
```

```python
import functools

import jax
import jax.numpy as jnp
from jax import lax
from jax.experimental import pallas as pl
from jax.experimental.pallas import tpu as pltpu

D_MODEL = 1024
D_FF = 2816
POOL_WINDOWS = (2, 4, 8, 16)
POOL_GROUP_DIM = D_MODEL // len(POOL_WINDOWS)
POOL_PREFIX = max(POOL_WINDOWS) - 1
N_HEADS = 16
N_KV_HEADS = 4
HEAD_DIM = 64
GROUP = N_HEADS // N_KV_HEADS
WINDOW = 128
ROPE_THETA = 10000.0
Q_DIM = N_HEADS * HEAD_DIM
KV_DIM = N_KV_HEADS * HEAD_DIM
QKV_DIM = Q_DIM + 2 * KV_DIM
RMS_EPS = 1e-6
NEG_INF = -1e30
PAST_LEN = 8192

LANES = 128
HEADS_PER_VREG = LANES // HEAD_DIM
Q_COLS = Q_DIM // LANES
KV_COLS = KV_DIM // LANES
POOL_HALO = 16
FF_CHUNK = 256
QK_LOOKAHEAD = 3
VMEM_LIMIT = 52 * 1024 * 1024

F32 = jnp.float32
BF16 = jnp.bfloat16


def _rms(x, g):
    ms = jnp.mean(x * x, axis=-1, keepdims=True)
    return x * lax.rsqrt(ms + RMS_EPS) * g


def _resident(shape):
    nd = len(shape)
    return pl.BlockSpec(shape, lambda *_: (0,) * nd, pipeline_mode=pl.Buffered(1))


def _resident_layer(shape, layer):
    nd = len(shape)
    return pl.BlockSpec((None,) + shape, lambda *_: (layer,) + (0,) * nd, pipeline_mode=pl.Buffered(1))


def _ffn_kernel(x_ref, g_ref, win_ref, wout_ref, *rest, final_norm):
    if final_norm:
        gf_ref, o_ref, h_ref = rest
    else:
        o_ref, h_ref = rest
    x = x_ref[...]
    xn = _rms(x, g_ref[...]).astype(BF16)
    for c in range(D_FF // FF_CHUNK):
        lo = c * FF_CHUNK
        gate = jnp.dot(xn, win_ref[:, lo:lo + FF_CHUNK], preferred_element_type=F32)
        up = jnp.dot(xn, win_ref[:, D_FF + lo:D_FF + lo + FF_CHUNK], preferred_element_type=F32)
        h_ref[:, lo:lo + FF_CHUNK] = (gate * jax.nn.sigmoid(gate) * up).astype(BF16)
    y = x + 0.5 * jnp.dot(h_ref[...], wout_ref[...], preferred_element_type=F32)
    if final_norm:
        y = _rms(y, gf_ref[...])
    o_ref[...] = y


def _ffn(x, g, w_in, w_out, layer, final_gain=None, *, tm):
    n = x.shape[0]
    final_norm = final_gain is not None
    in_specs = [
        pl.BlockSpec((tm, D_MODEL), lambda i: (i, 0)),
        _resident((1, D_MODEL)),
        _resident_layer((D_MODEL, 2 * D_FF), layer),
        _resident_layer((D_FF, D_MODEL), layer),
    ]
    args = [x, g.reshape(1, D_MODEL), w_in, w_out]
    if final_norm:
        in_specs.append(_resident((1, D_MODEL)))
        args.append(final_gain.reshape(1, D_MODEL))
    return pl.pallas_call(
        functools.partial(_ffn_kernel, final_norm=final_norm),
        out_shape=jax.ShapeDtypeStruct((n, D_MODEL), F32),
        grid=(n // tm,),
        in_specs=in_specs,
        out_specs=pl.BlockSpec((tm, D_MODEL), lambda i: (i, 0)),
        scratch_shapes=[pltpu.VMEM((tm, D_FF), BF16)],
        compiler_params=pltpu.CompilerParams(
            dimension_semantics=("arbitrary",), vmem_limit_bytes=VMEM_LIMIT),
        name="ffn_final" if final_norm else "ffn",
    )(*args)


def _pool_kernel(x_ref, *rest, has_prefix, start_pos, tt):
    if has_prefix:
        pre_ref, g_ref, w_ref, sc_ref, o_ref, st_ref, ext_ref = rest
    else:
        g_ref, w_ref, sc_ref, o_ref, st_ref, ext_ref = rest
    ti = pl.program_id(1)
    bb = x_ref.shape[0]
    te = tt + POOL_HALO

    @pl.when(ti == 0)
    def _():
        ext_ref[:, 0:POOL_HALO, :] = jnp.zeros((bb, POOL_HALO, D_MODEL), F32)
        if has_prefix:
            ext_ref[:, POOL_HALO - POOL_PREFIX:POOL_HALO, :] = pre_ref[...]

    if tt >= POOL_HALO:
        @pl.when(ti > 0)
        def _():
            ext_ref[:, 0:POOL_HALO, :] = ext_ref[:, tt:te, :]

    x = x_ref[...]
    u = _rms(x, g_ref[...])
    ext_ref[:, POOL_HALO:te, :] = u

    n_seen = start_pos + ti * tt + lax.broadcasted_iota(jnp.int32, (1, tt, 1), 1) + 1
    for gi, wg in enumerate(POOL_WINDOWS):
        sl = slice(gi * POOL_GROUP_DIM, (gi + 1) * POOL_GROUP_DIM)
        s = ext_ref[:, :, sl].reshape(bb * te, POOL_GROUP_DIM)
        span = 1
        while span < wg:
            s = s + pltpu.roll(s, span, axis=0)
            span *= 2
        s = s.reshape(bb, te, POOL_GROUP_DIM)[:, POOL_HALO:, :]
        cnt = jnp.minimum(n_seen, wg).astype(F32)
        p = s / cnt - u[:, :, sl]
        p2 = p.reshape(bb * tt, POOL_GROUP_DIM).astype(BF16)
        y = jnp.dot(p2, w_ref[gi], preferred_element_type=F32).reshape(bb, tt, POOL_GROUP_DIM)
        o_ref[:, :, sl] = x[:, :, sl] + y * sc_ref[:, :, sl]

    @pl.when(ti == pl.num_programs(1) - 1)
    def _():
        st_ref[...] = ext_ref[:, te - POOL_PREFIX:te, :]


def _pool(x, prefix, g, w, scale, layer, *, start_pos, bb, tt):
    b, t, _ = x.shape
    has_prefix = prefix is not None
    in_specs = [pl.BlockSpec((bb, tt, D_MODEL), lambda i, j: (i, j, 0))]
    args = [x]
    if has_prefix:
        in_specs.append(pl.BlockSpec((None, bb, POOL_PREFIX, D_MODEL), lambda i, j: (layer, i, 0, 0)))
        args.append(prefix)
    in_specs += [
        _resident((1, 1, D_MODEL)),
        _resident_layer((len(POOL_WINDOWS), POOL_GROUP_DIM, POOL_GROUP_DIM), layer),
        _resident((1, 1, D_MODEL)),
    ]
    args += [g.reshape(1, 1, D_MODEL), w, scale.reshape(1, 1, D_MODEL)]
    return pl.pallas_call(
        functools.partial(_pool_kernel, has_prefix=has_prefix, start_pos=start_pos, tt=tt),
        out_shape=(jax.ShapeDtypeStruct((b, t, D_MODEL), F32),
                   jax.ShapeDtypeStruct((b, POOL_PREFIX, D_MODEL), F32)),
        grid=(b // bb, t // tt),
        in_specs=in_specs,
        out_specs=(pl.BlockSpec((bb, tt, D_MODEL), lambda i, j: (i, j, 0)),
                   pl.BlockSpec((bb, POOL_PREFIX, D_MODEL), lambda i, j: (i, 0, 0))),
        scratch_shapes=[pltpu.VMEM((bb, tt + POOL_HALO, D_MODEL), F32)],
        compiler_params=pltpu.CompilerParams(
            dimension_semantics=("arbitrary", "arbitrary"), vmem_limit_bytes=VMEM_LIMIT),
        name="pool_sample" if has_prefix else "pool_prompt",
    )(*args)


def _rope_tables(pos):
    half = HEAD_DIM // 2
    inv = ROPE_THETA ** (-jnp.arange(half, dtype=F32) / half)
    ang = pos.astype(F32)[:, None] * inv[None, :]
    cos, sin = jnp.cos(ang), jnp.sin(ang)
    cos64 = jnp.concatenate([cos, cos], axis=-1)
    sin64 = jnp.concatenate([-sin, sin], axis=-1)
    return jnp.tile(cos64, (1, HEADS_PER_VREG)), jnp.tile(sin64, (1, HEADS_PER_VREG))


def _rope_col(xc, cos, sin, first_half):
    half = HEAD_DIM // 2
    partner = jnp.where(first_half, pltpu.roll(xc, LANES - half, axis=1), pltpu.roll(xc, half, axis=1))
    return xc * cos + partner * sin


def _head_half_variants(col, lo_half):
    sw = pltpu.roll(col, HEAD_DIM, axis=1)
    zero = jnp.zeros_like(col)
    return {
        (0, 0): jnp.where(lo_half, col, zero),
        (0, 1): jnp.where(lo_half, zero, sw),
        (1, 0): jnp.where(lo_half, sw, zero),
        (1, 1): jnp.where(lo_half, zero, col),
    }


def _softmax_with_sink(s, sink):
    m = jnp.maximum(jnp.max(s, axis=-1, keepdims=True), sink)
    e = jnp.exp(s - m)
    denom = jnp.sum(e, axis=-1, keepdims=True) + jnp.exp(sink - m)
    return e.astype(BF16), 1.0 / denom


def _attn_prompt_kernel(x_ref, cos_ref, sin_ref, g_ref, wqkv_ref, bqkv_ref, wo_ref, bo_ref, sink_ref,
                        o_ref, kout_ref, vout_ref, kvar_ref, vlo_ref, vhi_ref, *, nblk):
    j = pl.program_id(1)
    w = WINDOW
    r = nblk * w

    @pl.when(j == 0)
    def _():
        kvar_ref[:, 0:w, :] = jnp.zeros((2 * N_KV_HEADS, w, LANES), BF16)
        vlo_ref[:, 0:w, :] = jnp.zeros((N_KV_HEADS, w, LANES), BF16)
        vhi_ref[:, 0:w, :] = jnp.zeros((N_KV_HEADS, w, LANES), BF16)

    @pl.when(j > 0)
    def _():
        kvar_ref[:, 0:w, :] = kvar_ref[:, r:r + w, :]
        vlo_ref[:, 0:w, :] = vlo_ref[:, r:r + w, :]
        vhi_ref[:, 0:w, :] = vhi_ref[:, r:r + w, :]

    x = x_ref[0]
    u = _rms(x, g_ref[...]).astype(BF16)
    qkv = jnp.dot(u, wqkv_ref[...], preferred_element_type=F32) + bqkv_ref[...]
    cos, sin = cos_ref[...], sin_ref[...]
    lane = lax.broadcasted_iota(jnp.int32, (r, LANES), 1)
    first_half = (lane % HEAD_DIM) < (HEAD_DIM // 2)
    lo_half = lane < HEAD_DIM
    lo_half_blk = lax.broadcasted_iota(jnp.int32, (w, LANES), 1) < HEAD_DIM

    qcols = []
    for c in range(Q_COLS):
        qc = _rope_col(qkv[:, c * LANES:(c + 1) * LANES], cos, sin, first_half)
        qcols.append((qc * (HEAD_DIM ** -0.5)).astype(BF16))

    for c in range(KV_COLS):
        kc = _rope_col(qkv[:, Q_DIM + c * LANES:Q_DIM + (c + 1) * LANES], cos, sin, first_half)
        vc = qkv[:, Q_DIM + KV_DIM + c * LANES:Q_DIM + KV_DIM + (c + 1) * LANES]
        kout_ref[0, :, c * LANES:(c + 1) * LANES] = kc[r - w:]
        vout_ref[0, :, c * LANES:(c + 1) * LANES] = vc[r - w:]
        kv = _head_half_variants(kc, lo_half)
        vv = _head_half_variants(vc, lo_half)
        for hb in range(HEADS_PER_VREG):
            kvh = c * HEADS_PER_VREG + hb
            for a in range(HEADS_PER_VREG):
                kvar_ref[kvh * HEADS_PER_VREG + a, w:w + r, :] = kv[(hb, a)].astype(BF16)
            vlo_ref[kvh, w:w + r, :] = vv[(hb, 0)].astype(BF16)
            vhi_ref[kvh, w:w + r, :] = vv[(hb, 1)].astype(BF16)

    row = lax.broadcasted_iota(jnp.int32, (2 * w, 2 * w), 0) % w
    col = lax.broadcasted_iota(jnp.int32, (2 * w, 2 * w), 1)
    band = (col > row) & (col <= row + w)
    first_key = jnp.where(j > 0, 0, w)
    band_first = band & (col >= first_key)
    top_rows = lax.broadcasted_iota(jnp.int32, (2 * w, 1), 0) < w

    groups = [(blk, kvh, a) for blk in range(nblk) for kvh in range(N_KV_HEADS) for a in range(HEADS_PER_VREG)]

    def scores(blk, kvh, a):
        qrows = slice(blk * w, (blk + 1) * w)
        qop = jnp.concatenate([qcols[2 * kvh][qrows], qcols[2 * kvh + 1][qrows]], axis=0)
        return lax.dot_general(qop, kvar_ref[kvh * HEADS_PER_VREG + a, blk * w:(blk + 2) * w, :],
                               (((1,), (1,)), ((), ())), preferred_element_type=F32)

    pending = [scores(*g) for g in groups[:QK_LOOKAHEAD]]
    es, rinvs, acc = [], [], None
    for gi, (blk, kvh, a) in enumerate(groups):
        if gi + QK_LOOKAHEAD < len(groups):
            pending.append(scores(*groups[gi + QK_LOOKAHEAD]))
        s = jnp.where(band_first if blk == 0 else band, pending[gi], NEG_INF)
        pending[gi] = None
        h0 = kvh * GROUP + a
        sink = jnp.where(top_rows, sink_ref[h0], sink_ref[h0 + HEADS_PER_VREG])
        e, rinv = _softmax_with_sink(s, sink)
        es.append(e)
        rinvs.append(rinv)
        if a == HEADS_PER_VREG - 1:
            krows = slice(blk * w, (blk + 2) * w)
            vstack = jnp.concatenate([vlo_ref[kvh, krows, :], vhi_ref[kvh, krows, :]], axis=0)
            pair = []
            for mm in range(2):
                rows = slice(mm * w, (mm + 1) * w)
                lhs = jnp.concatenate([es[0][rows], es[1][rows]], axis=1)
                o = jnp.dot(lhs, vstack, preferred_element_type=F32)
                pair.append(o * jnp.where(lo_half_blk, rinvs[0][rows], rinvs[1][rows]))
            part = jnp.dot(jnp.concatenate(pair, axis=1).astype(BF16),
                           wo_ref[kvh * GROUP * HEAD_DIM:(kvh + 1) * GROUP * HEAD_DIM, :],
                           preferred_element_type=F32)
            acc = part if acc is None else acc + part
            es, rinvs = [], []
            if kvh == N_KV_HEADS - 1:
                qrows = slice(blk * w, (blk + 1) * w)
                o_ref[0, qrows, :] = x[qrows] + acc + bo_ref[...]
                acc = None


def _attn_prompt(x, g, w_qkv, b_qkv, w_o, b_o, sinks, layer, *, nblk):
    b, t, _ = x.shape
    r = nblk * WINDOW
    cos, sin = _rope_tables(jnp.arange(t))
    keys = (nblk + 1) * WINDOW
    return pl.pallas_call(
        functools.partial(_attn_prompt_kernel, nblk=nblk),
        out_shape=(jax.ShapeDtypeStruct((b, t, D_MODEL), F32),
                   jax.ShapeDtypeStruct((b, WINDOW, KV_DIM), F32),
                   jax.ShapeDtypeStruct((b, WINDOW, KV_DIM), F32)),
        grid=(b, t // r),
        in_specs=[
            pl.BlockSpec((1, r, D_MODEL), lambda i, j: (i, j, 0)),
            pl.BlockSpec((r, LANES), lambda i, j: (j, 0)),
            pl.BlockSpec((r, LANES), lambda i, j: (j, 0)),
            _resident((1, D_MODEL)),
            _resident_layer((D_MODEL, QKV_DIM), layer),
            _resident((1, QKV_DIM)),
            _resident_layer((Q_DIM, D_MODEL), layer),
            _resident((1, D_MODEL)),
            pl.BlockSpec(memory_space=pltpu.SMEM),
        ],
        out_specs=(pl.BlockSpec((1, r, D_MODEL), lambda i, j: (i, j, 0)),
                   pl.BlockSpec((1, WINDOW, KV_DIM), lambda i, j: (i, 0, 0)),
                   pl.BlockSpec((1, WINDOW, KV_DIM), lambda i, j: (i, 0, 0))),
        scratch_shapes=[pltpu.VMEM((2 * N_KV_HEADS, keys, LANES), BF16),
                        pltpu.VMEM((N_KV_HEADS, keys, LANES), BF16),
                        pltpu.VMEM((N_KV_HEADS, keys, LANES), BF16)],
        compiler_params=pltpu.CompilerParams(
            dimension_semantics=("arbitrary", "arbitrary"), vmem_limit_bytes=VMEM_LIMIT),
        name="attn_prompt",
    )(x, cos, sin, g.reshape(1, D_MODEL), w_qkv, b_qkv.reshape(1, QKV_DIM), w_o,
      b_o.reshape(1, D_MODEL), sinks)


SAMPLE_KEYS = 2 * WINDOW
SAMPLE_UNROLL = 4


def _attn_sample_kernel(x_ref, ck_ref, cv_ref, cos_ref, sin_ref, g_ref, wqkv_ref, bqkv_ref, wo_ref, bo_ref,
                        sinkcol_ref, o_ref, kout_ref, vout_ref, q_s, kn_s, vn_s, kk_s, vv_s, oa_s, *, dt):
    bb = x_ref.shape[0]
    w = WINDOW
    x = x_ref[...].reshape(bb * dt, D_MODEL)
    u = _rms(x, g_ref[...]).astype(BF16)
    qkv = jnp.dot(u, wqkv_ref[...], preferred_element_type=F32) + bqkv_ref[...]
    cos, sin = cos_ref[...], sin_ref[...]
    lane_t = lax.broadcasted_iota(jnp.int32, (bb * dt, LANES), 1)
    first_half = (lane_t % HEAD_DIM) < (HEAD_DIM // 2)
    for c in range(Q_COLS):
        qc = _rope_col(qkv[:, c * LANES:(c + 1) * LANES], cos, sin, first_half)
        q_s[:, c * LANES:(c + 1) * LANES] = qc * (HEAD_DIM ** -0.5)
    for c in range(KV_COLS):
        kn_s[:, c * LANES:(c + 1) * LANES] = _rope_col(
            qkv[:, Q_DIM + c * LANES:Q_DIM + (c + 1) * LANES], cos, sin, first_half)
    vn_s[...] = qkv[:, Q_DIM + KV_DIM:]

    kk_s[:, w + dt:, :] = jnp.zeros((SAMPLE_UNROLL, SAMPLE_KEYS - w - dt, KV_DIM), BF16)
    vv_s[:, w + dt:, :] = jnp.zeros((SAMPLE_UNROLL, SAMPLE_KEYS - w - dt, KV_DIM), BF16)

    heads_per_col = N_HEADS // KV_COLS
    qcols_per_col = heads_per_col // HEADS_PER_VREG
    rows = heads_per_col * dt
    lane8 = lax.broadcasted_iota(jnp.int32, (dt, LANES), 1)
    lo8 = lane8 < HEAD_DIM
    t_row = lax.broadcasted_iota(jnp.int32, (rows, SAMPLE_KEYS), 0) % dt
    key = lax.broadcasted_iota(jnp.int32, (rows, SAMPLE_KEYS), 1)
    valid = (key > t_row) & (key <= t_row + w)

    def stage(b, slot):
        r0 = pl.multiple_of(b * dt, dt)
        knew = kn_s[pl.ds(r0, dt), :]
        vnew = vn_s[pl.ds(r0, dt), :]
        kout_ref[b, 0:w - dt, :] = ck_ref[b, dt:w, :]
        kout_ref[b, w - dt:w, :] = knew
        vout_ref[b, 0:w - dt, :] = cv_ref[b, dt:w, :]
        vout_ref[b, w - dt:w, :] = vnew
        kk_s[slot, 0:w, :] = ck_ref[b].astype(BF16)
        kk_s[slot, w:w + dt, :] = knew.astype(BF16)
        vv_s[slot, 0:w, :] = cv_ref[b].astype(BF16)
        vv_s[slot, w:w + dt, :] = vnew.astype(BF16)

    def scores(b, slot, c):
        r0 = pl.multiple_of(b * dt, dt)
        pieces = []
        for h8 in range(heads_per_col):
            qi = c * qcols_per_col + h8 // HEADS_PER_VREG
            qcol = q_s[pl.ds(r0, dt), qi * LANES:(qi + 1) * LANES]
            a, hb = h8 % HEADS_PER_VREG, h8 // GROUP
            src = qcol if a == hb else pltpu.roll(qcol, HEAD_DIM, axis=1)
            keep = lo8 if hb == 0 else jnp.logical_not(lo8)
            pieces.append(jnp.where(keep, src, 0.0))
        qop = jnp.concatenate(pieces, axis=0).astype(BF16)
        return lax.dot_general(qop, kk_s[slot, :, c * LANES:(c + 1) * LANES],
                               (((1,), (1,)), ((), ())), preferred_element_type=F32)

    def place(b, c, o):
        r0 = pl.multiple_of(b * dt, dt)
        for mm in range(qcols_per_col):
            halves = []
            for a in range(HEADS_PER_VREG):
                h8 = mm * HEADS_PER_VREG + a
                hb = h8 // GROUP
                piece = o[h8 * dt:(h8 + 1) * dt, :]
                halves.append(piece if a == hb else pltpu.roll(piece, HEAD_DIM, axis=1))
            oi = c * qcols_per_col + mm
            oa_s[pl.ds(r0, dt), oi * LANES:(oi + 1) * LANES] = jnp.where(lo8, halves[0], halves[1])

    def body(gidx, carry):
        work = [(gidx * SAMPLE_UNROLL + slot, slot, c) for slot in range(SAMPLE_UNROLL) for c in range(KV_COLS)]
        for slot in range(SAMPLE_UNROLL):
            stage(gidx * SAMPLE_UNROLL + slot, slot)
        ss = [scores(b, slot, c) for b, slot, c in work]
        sm = [_softmax_with_sink(jnp.where(valid, s, NEG_INF), sinkcol_ref[c * rows:(c + 1) * rows, :])
              for s, (b, slot, c) in zip(ss, work)]
        for (e, rinv), (b, slot, c) in zip(sm, work):
            o = jnp.dot(e, vv_s[slot, :, c * LANES:(c + 1) * LANES], preferred_element_type=F32) * rinv
            place(b, c, o)
        return carry

    lax.fori_loop(0, bb // SAMPLE_UNROLL, body, 0)

    y = x + jnp.dot(oa_s[...].astype(BF16), wo_ref[...], preferred_element_type=F32) + bo_ref[...]
    o_ref[...] = y.reshape(bb, dt, D_MODEL)


def _attn_sample(x, cache_k, cache_v, g, w_qkv, b_qkv, w_o, b_o, sinks, layer, *, bb):
    b, dt, _ = x.shape
    cos, sin = _rope_tables(PAST_LEN + jnp.arange(dt))
    cos, sin = jnp.tile(cos, (bb, 1)), jnp.tile(sin, (bb, 1))
    sinkcol = jnp.repeat(sinks, dt).reshape(N_HEADS * dt, 1)
    n = bb * dt
    return pl.pallas_call(
        functools.partial(_attn_sample_kernel, dt=dt),
        out_shape=(jax.ShapeDtypeStruct((b, dt, D_MODEL), F32),
                   jax.ShapeDtypeStruct((b, WINDOW, KV_DIM), F32),
                   jax.ShapeDtypeStruct((b, WINDOW, KV_DIM), F32)),
        grid=(b // bb,),
        in_specs=[
            pl.BlockSpec((bb, dt, D_MODEL), lambda i: (i, 0, 0)),
            pl.BlockSpec((None, bb, WINDOW, KV_DIM), lambda i: (layer, i, 0, 0)),
            pl.BlockSpec((None, bb, WINDOW, KV_DIM), lambda i: (layer, i, 0, 0)),
            _resident((n, LANES)),
            _resident((n, LANES)),
            _resident((1, D_MODEL)),
            _resident_layer((D_MODEL, QKV_DIM), layer),
            _resident((1, QKV_DIM)),
            _resident_layer((Q_DIM, D_MODEL), layer),
            _resident((1, D_MODEL)),
            _resident((N_HEADS * dt, 1)),
        ],
        out_specs=(pl.BlockSpec((bb, dt, D_MODEL), lambda i: (i, 0, 0)),
                   pl.BlockSpec((bb, WINDOW, KV_DIM), lambda i: (i, 0, 0)),
                   pl.BlockSpec((bb, WINDOW, KV_DIM), lambda i: (i, 0, 0))),
        scratch_shapes=[pltpu.VMEM((n, Q_DIM), F32),
                        pltpu.VMEM((n, KV_DIM), F32),
                        pltpu.VMEM((n, KV_DIM), F32),
                        pltpu.VMEM((SAMPLE_UNROLL, SAMPLE_KEYS, KV_DIM), BF16),
                        pltpu.VMEM((SAMPLE_UNROLL, SAMPLE_KEYS, KV_DIM), BF16),
                        pltpu.VMEM((n, Q_DIM), F32)],
        compiler_params=pltpu.CompilerParams(
            dimension_semantics=("arbitrary",), vmem_limit_bytes=VMEM_LIMIT),
        name="attn_sample",
    )(x, cache_k, cache_v, cos, sin, g.reshape(1, D_MODEL), w_qkv, b_qkv.reshape(1, QKV_DIM), w_o,
      b_o.reshape(1, D_MODEL), sinkcol)


def kernel(x_prompt, x_sample, state_pool, cache_k, cache_v, norm_ffn1, ffn1_w_in, ffn1_w_out, norm_mix,
           norm_ffn2, ffn2_w_in, ffn2_w_out, pool_w, pool_scale, attn_w_qkv, attn_b_qkv, attn_w_o, attn_b_o,
           attn_sinks, norm_final):
    batch, seq, _ = x_prompt.shape
    dec_batch, dec_seq, _ = x_sample.shape
    depth = norm_ffn1.shape[0]
    n_mixers = 2
    w1_in, w1_out = ffn1_w_in.astype(BF16), ffn1_w_out.astype(BF16)
    w2_in, w2_out = ffn2_w_in.astype(BF16), ffn2_w_out.astype(BF16)
    pw = pool_w.astype(BF16)
    wqkv, wo = attn_w_qkv.astype(BF16), attn_w_o.astype(BF16)
    n_attn = cache_k.shape[0]
    ck = cache_k.reshape(n_attn, dec_batch, WINDOW, KV_DIM)
    cv = cache_v.reshape(n_attn, dec_batch, WINDOW, KV_DIM)

    xp = x_prompt.reshape(batch * seq, D_MODEL)
    xs = x_sample.reshape(dec_batch * dec_seq, D_MODEL)
    pool_p, pool_s, kp_l, vp_l, ks_l, vs_l = [], [], [], [], [], []
    for i in range(depth):
        xp = _ffn(xp, norm_ffn1[i], w1_in, w1_out, i, tm=512)
        xs = _ffn(xs, norm_ffn1[i], w1_in, w1_out, i, tm=512)
        j = i // n_mixers
        xp3 = xp.reshape(batch, seq, D_MODEL)
        xs3 = xs.reshape(dec_batch, dec_seq, D_MODEL)
        if i % n_mixers == 0:
            xp3, sp = _pool(xp3, None, norm_mix[i], pw, pool_scale[j], j, start_pos=0, bb=1, tt=512)
            xs3, ss = _pool(xs3, state_pool, norm_mix[i], pw, pool_scale[j], j, start_pos=PAST_LEN,
                            bb=32, tt=dec_seq)
            pool_p.append(sp)
            pool_s.append(ss)
        else:
            xp3, kp, vp = _attn_prompt(xp3, norm_mix[i], wqkv, attn_b_qkv[j], wo, attn_b_o[j], attn_sinks[j], j,
                                       nblk=2)
            xs3, kn, vn = _attn_sample(xs3, ck, cv, norm_mix[i], wqkv, attn_b_qkv[j], wo, attn_b_o[j],
                                       attn_sinks[j], j, bb=32)
            kp_l.append(kp.reshape(batch, WINDOW, N_KV_HEADS, HEAD_DIM))
            vp_l.append(vp.reshape(batch, WINDOW, N_KV_HEADS, HEAD_DIM))
            ks_l.append(kn.reshape(dec_batch, WINDOW, N_KV_HEADS, HEAD_DIM))
            vs_l.append(vn.reshape(dec_batch, WINDOW, N_KV_HEADS, HEAD_DIM))
        xp = xp3.reshape(batch * seq, D_MODEL)
        xs = xs3.reshape(dec_batch * dec_seq, D_MODEL)
        fg = norm_final if i == depth - 1 else None
        xp = _ffn(xp, norm_ffn2[i], w2_in, w2_out, i, fg, tm=512)
        xs = _ffn(xs, norm_ffn2[i], w2_in, w2_out, i, fg, tm=512)
    return (xp.reshape(batch, seq, D_MODEL), xs.reshape(dec_batch, dec_seq, D_MODEL),
            jnp.stack(pool_p), jnp.stack(pool_s), jnp.stack(kp_l), jnp.stack(vp_l),
            jnp.stack(ks_l), jnp.stack(vs_l))
```

```python
import functools

import jax
import jax.numpy as jnp
from jax import lax
from jax.experimental import pallas as pl
from jax.experimental.pallas import tpu as pltpu

D_MODEL = 1024
D_FF = 2816
POOL_WINDOWS = (2, 4, 8, 16)
POOL_GROUP_DIM = D_MODEL // len(POOL_WINDOWS)
POOL_PREFIX = max(POOL_WINDOWS) - 1
N_HEADS = 16
N_KV_HEADS = 4
HEAD_DIM = 64
GROUP = N_HEADS // N_KV_HEADS
WINDOW = 128
ROPE_THETA = 10000.0
Q_DIM = N_HEADS * HEAD_DIM
KV_DIM = N_KV_HEADS * HEAD_DIM
QKV_DIM = Q_DIM + 2 * KV_DIM
RMS_EPS = 1e-6
NEG_INF = -1e30
PAST_LEN = 8192

LANES = 128
HEADS_PER_VREG = LANES // HEAD_DIM
Q_COLS = Q_DIM // LANES
KV_COLS = KV_DIM // LANES
POOL_HALO = 16
FF_CHUNK = 256
FFN_TILE = 1024
ATTN_BLOCKS_PER_STEP = 8
ATTN_PROJ_BLOCKS = 2
PROJ_PIECES = QKV_DIM // (2 * LANES)
PROJ_AT_GROUP = 2
QK_LOOKAHEAD = 3
LOG2_E = 1.4426950408889634
QK_SCALE_LOG2 = HEAD_DIM ** -0.5 * LOG2_E
VMEM_LIMIT = 52 * 1024 * 1024

F32 = jnp.float32
BF16 = jnp.bfloat16


def _rms(x, g):
    ms = jnp.mean(x * x, axis=-1, keepdims=True)
    return x * lax.rsqrt(ms + RMS_EPS) * g


def _resident(shape):
    nd = len(shape)
    return pl.BlockSpec(shape, lambda *_: (0,) * nd, pipeline_mode=pl.Buffered(1))


def _resident_layer(shape, layer):
    nd = len(shape)
    return pl.BlockSpec((None,) + shape, lambda *_: (layer,) + (0,) * nd, pipeline_mode=pl.Buffered(1))


def _ffn_kernel(x_ref, g_ref, win_ref, wout_ref, *rest, final_norm):
    if final_norm:
        gf_ref, o_ref, h_ref = rest
    else:
        o_ref, h_ref = rest
    x = x_ref[...]
    xn = _rms(x, g_ref[...]).astype(BF16)
    for c in range(D_FF // FF_CHUNK):
        lo = c * FF_CHUNK
        gate = jnp.dot(xn, win_ref[:, lo:lo + FF_CHUNK], preferred_element_type=F32)
        up = jnp.dot(xn, win_ref[:, D_FF + lo:D_FF + lo + FF_CHUNK], preferred_element_type=F32)
        h_ref[:, lo:lo + FF_CHUNK] = (gate * jax.nn.sigmoid(gate) * up).astype(BF16)
    y = x + 0.5 * jnp.dot(h_ref[...], wout_ref[...], preferred_element_type=F32)
    if final_norm:
        y = _rms(y, gf_ref[...])
    o_ref[...] = y


def _ffn(x, g, w_in, w_out, layer, final_gain=None, *, tm):
    n = x.shape[0]
    final_norm = final_gain is not None
    in_specs = [
        pl.BlockSpec((tm, D_MODEL), lambda i: (i, 0)),
        _resident((1, D_MODEL)),
        _resident_layer((D_MODEL, 2 * D_FF), layer),
        _resident_layer((D_FF, D_MODEL), layer),
    ]
    args = [x, g.reshape(1, D_MODEL), w_in, w_out]
    if final_norm:
        in_specs.append(_resident((1, D_MODEL)))
        args.append(final_gain.reshape(1, D_MODEL))
    return pl.pallas_call(
        functools.partial(_ffn_kernel, final_norm=final_norm),
        out_shape=jax.ShapeDtypeStruct((n, D_MODEL), F32),
        grid=(n // tm,),
        in_specs=in_specs,
        out_specs=pl.BlockSpec((tm, D_MODEL), lambda i: (i, 0)),
        scratch_shapes=[pltpu.VMEM((tm, D_FF), BF16)],
        compiler_params=pltpu.CompilerParams(
            dimension_semantics=("arbitrary",), vmem_limit_bytes=VMEM_LIMIT),
        name="ffn_final" if final_norm else "ffn",
    )(*args)


def _pool_kernel(x_ref, *rest, has_prefix, start_pos, tt):
    if has_prefix:
        pre_ref, g_ref, w_ref, sc_ref, o_ref, st_ref, ext_ref = rest
    else:
        g_ref, w_ref, sc_ref, o_ref, st_ref, ext_ref = rest
    ti = pl.program_id(1)
    bb = x_ref.shape[0]
    te = tt + POOL_HALO

    @pl.when(ti == 0)
    def _():
        ext_ref[:, 0:POOL_HALO, :] = jnp.zeros((bb, POOL_HALO, D_MODEL), F32)
        if has_prefix:
            ext_ref[:, POOL_HALO - POOL_PREFIX:POOL_HALO, :] = pre_ref[...]

    if tt >= POOL_HALO:
        @pl.when(ti > 0)
        def _():
            ext_ref[:, 0:POOL_HALO, :] = ext_ref[:, tt:te, :]

    x = x_ref[...]
    u = _rms(x, g_ref[...])
    ext_ref[:, POOL_HALO:te, :] = u

    n_seen = start_pos + ti * tt + lax.broadcasted_iota(jnp.int32, (1, tt, 1), 1) + 1
    for gi, wg in enumerate(POOL_WINDOWS):
        sl = slice(gi * POOL_GROUP_DIM, (gi + 1) * POOL_GROUP_DIM)
        s = ext_ref[:, :, sl].reshape(bb * te, POOL_GROUP_DIM)
        span = 1
        while span < wg:
            s = s + pltpu.roll(s, span, axis=0)
            span *= 2
        s = s.reshape(bb, te, POOL_GROUP_DIM)[:, POOL_HALO:, :]
        cnt = jnp.minimum(n_seen, wg).astype(F32)
        p = s / cnt - u[:, :, sl]
        p2 = p.reshape(bb * tt, POOL_GROUP_DIM).astype(BF16)
        y = jnp.dot(p2, w_ref[gi], preferred_element_type=F32).reshape(bb, tt, POOL_GROUP_DIM)
        o_ref[:, :, sl] = x[:, :, sl] + y * sc_ref[:, :, sl]

    @pl.when(ti == pl.num_programs(1) - 1)
    def _():
        st_ref[...] = ext_ref[:, te - POOL_PREFIX:te, :]


def _pool(x, prefix, g, w, scale, layer, *, start_pos, bb, tt):
    b, t, _ = x.shape
    has_prefix = prefix is not None
    in_specs = [pl.BlockSpec((bb, tt, D_MODEL), lambda i, j: (i, j, 0))]
    args = [x]
    if has_prefix:
        in_specs.append(pl.BlockSpec((None, bb, POOL_PREFIX, D_MODEL), lambda i, j: (layer, i, 0, 0)))
        args.append(prefix)
    in_specs += [
        _resident((1, 1, D_MODEL)),
        _resident_layer((len(POOL_WINDOWS), POOL_GROUP_DIM, POOL_GROUP_DIM), layer),
        _resident((1, 1, D_MODEL)),
    ]
    args += [g.reshape(1, 1, D_MODEL), w, scale.reshape(1, 1, D_MODEL)]
    return pl.pallas_call(
        functools.partial(_pool_kernel, has_prefix=has_prefix, start_pos=start_pos, tt=tt),
        out_shape=(jax.ShapeDtypeStruct((b, t, D_MODEL), F32),
                   jax.ShapeDtypeStruct((b, POOL_PREFIX, D_MODEL), F32)),
        grid=(b // bb, t // tt),
        in_specs=in_specs,
        out_specs=(pl.BlockSpec((bb, tt, D_MODEL), lambda i, j: (i, j, 0)),
                   pl.BlockSpec((bb, POOL_PREFIX, D_MODEL), lambda i, j: (i, 0, 0))),
        scratch_shapes=[pltpu.VMEM((bb, tt + POOL_HALO, D_MODEL), F32)],
        compiler_params=pltpu.CompilerParams(
            dimension_semantics=("arbitrary", "arbitrary"), vmem_limit_bytes=VMEM_LIMIT),
        name="pool_sample" if has_prefix else "pool_prompt",
    )(*args)


def _rope_tables(pos):
    half = HEAD_DIM // 2
    inv = ROPE_THETA ** (-jnp.arange(half, dtype=F32) / half)
    ang = pos.astype(F32)[:, None] * inv[None, :]
    cos, sin = jnp.cos(ang), jnp.sin(ang)
    cos64 = jnp.concatenate([cos, cos], axis=-1)
    sin64 = jnp.concatenate([-sin, sin], axis=-1)
    return jnp.tile(cos64, (1, HEADS_PER_VREG)), jnp.tile(sin64, (1, HEADS_PER_VREG))


def _rope_col(xc, cos, sin, first_half):
    half = HEAD_DIM // 2
    partner = jnp.where(first_half, pltpu.roll(xc, LANES - half, axis=1), pltpu.roll(xc, half, axis=1))
    return xc * cos + partner * sin


def _head_half_variants(col, lo_half):
    sw = pltpu.roll(col, HEAD_DIM, axis=1)
    zero = jnp.zeros_like(col)
    return {
        (0, 0): jnp.where(lo_half, col, zero),
        (0, 1): jnp.where(lo_half, zero, sw),
        (1, 0): jnp.where(lo_half, sw, zero),
        (1, 1): jnp.where(lo_half, zero, col),
    }


def _softmax_with_sink(s, sink):
    m = jnp.maximum(jnp.max(s, axis=-1, keepdims=True), sink)
    e = jnp.exp2(s - m)
    denom = jnp.sum(e, axis=-1, keepdims=True) + jnp.exp2(sink - m)
    return e.astype(BF16), 1.0 / denom


def _attn_prompt_kernel(x_ref, cos_ref, sin_ref, g_ref, wqkv_ref, bqkv_ref, wo_ref, bo_ref, sink_ref,
                        o_ref, kout_ref, vout_ref, kt_ref, vlo_ref, vhi_ref, *, nblk):
    j = pl.program_id(1)
    w = WINDOW
    r = nblk * w

    @pl.when(j == 0)
    def _():
        kt_ref[...] = jnp.zeros(kt_ref.shape, BF16)
        vlo_ref[:, 0:w, 0:LANES] = jnp.zeros((N_KV_HEADS, w, LANES), BF16)
        vhi_ref[:, 0:w, 0:LANES] = jnp.zeros((N_KV_HEADS, w, LANES), BF16)
        keys = vlo_ref.shape[1]
        ones_lo = (lax.broadcasted_iota(jnp.int32, (N_KV_HEADS, keys, LANES), 2) < HEAD_DIM).astype(BF16)
        vlo_ref[:, :, LANES:] = ones_lo
        vhi_ref[:, :, LANES:] = 1 - ones_lo

    @pl.when(j > 0)
    def _():
        kt_ref[:, :, 0:w] = kt_ref[:, :, r:r + w]
        vlo_ref[:, 0:w, 0:LANES] = vlo_ref[:, r:r + w, 0:LANES]
        vhi_ref[:, 0:w, 0:LANES] = vhi_ref[:, r:r + w, 0:LANES]

    rc = ATTN_PROJ_BLOCKS * w
    n_chunks = nblk // ATTN_PROJ_BLOCKS
    lane = lax.broadcasted_iota(jnp.int32, (rc, LANES), 1)
    first_half = (lane % HEAD_DIM) < (HEAD_DIM // 2)
    lo_half = lane < HEAD_DIM
    lo_half_2w = lax.broadcasted_iota(jnp.int32, (2 * w, LANES), 1) < HEAD_DIM

    u_of, qcols_of = {}, {}

    def project_piece(ch, t):
        rows = slice(ch * rc, (ch + 1) * rc)
        if t == 0:
            u_of[ch] = _rms(x_ref[0, rows, :], g_ref[...]).astype(BF16)
            qcols_of[ch] = []
        cols = slice(t * 2 * LANES, (t + 1) * 2 * LANES)
        tile = jnp.dot(u_of[ch], wqkv_ref[:, cols], preferred_element_type=F32) + bqkv_ref[:, cols]
        halves = [tile[:, :LANES], tile[:, LANES:]]
        if t < Q_COLS // 2:
            cos, sin = cos_ref[rows, :], sin_ref[rows, :]
            for hc in halves:
                qcols_of[ch].append((_rope_col(hc, cos, sin, first_half) * QK_SCALE_LOG2).astype(BF16))
            return
        is_k = t == Q_COLS // 2
        if is_k:
            cos, sin = cos_ref[rows, :], sin_ref[rows, :]
            halves = [_rope_col(hc, cos, sin, first_half) for hc in halves]
        krows = slice(w + ch * rc, w + (ch + 1) * rc)
        for c, col in enumerate(halves):
            if ch == n_chunks - 1:
                out_ref = kout_ref if is_k else vout_ref
                out_ref[0, :, c * LANES:(c + 1) * LANES] = col[rc - w:]
            if is_k:
                col_t = col.T.astype(BF16)
                for hb in range(HEADS_PER_VREG):
                    kvh = c * HEADS_PER_VREG + hb
                    head_t = col_t[hb * HEAD_DIM:(hb + 1) * HEAD_DIM, :]
                    for a in range(HEADS_PER_VREG):
                        kt_ref[kvh * HEADS_PER_VREG + a, a * HEAD_DIM:(a + 1) * HEAD_DIM, krows] = head_t
            else:
                var = _head_half_variants(col, lo_half)
                for hb in range(HEADS_PER_VREG):
                    kvh = c * HEADS_PER_VREG + hb
                    vlo_ref[kvh, krows, 0:LANES] = var[(hb, 0)].astype(BF16)
                    vhi_ref[kvh, krows, 0:LANES] = var[(hb, 1)].astype(BF16)

    for t in range(PROJ_PIECES):
        project_piece(0, t)

    row = lax.broadcasted_iota(jnp.int32, (2 * w, 2 * w), 0) % w
    col = lax.broadcasted_iota(jnp.int32, (2 * w, 2 * w), 1)
    band = (col > row) & (col <= row + w)
    first_key = jnp.where(j > 0, 0, w)
    band_first = band & (col >= first_key)
    top_rows = lax.broadcasted_iota(jnp.int32, (2 * w, 1), 0) < w

    groups = [(ch, kvh, bl, a) for ch in range(n_chunks) for kvh in range(N_KV_HEADS)
              for bl in range(ATTN_PROJ_BLOCKS) for a in range(HEADS_PER_VREG)]

    def scores(ch, kvh, bl, a):
        qcols = qcols_of[ch]
        qrows = slice(bl * w, (bl + 1) * w)
        blk = ch * ATTN_PROJ_BLOCKS + bl
        qop = jnp.concatenate([qcols[2 * kvh][qrows], qcols[2 * kvh + 1][qrows]], axis=0)
        return jnp.dot(qop, kt_ref[kvh * HEADS_PER_VREG + a, :, blk * w:(blk + 2) * w],
                       preferred_element_type=F32)

    groups_per_chunk = len(groups) // n_chunks
    pending = [scores(*g) for g in groups[:QK_LOOKAHEAD]]
    es, sink_terms, o_parts, acc = [], [], [], None
    for gi, (ch, kvh, bl, a) in enumerate(groups):
        gl = gi % groups_per_chunk
        if 0 <= gl - PROJ_AT_GROUP < PROJ_PIECES and ch + 1 < n_chunks:
            project_piece(ch + 1, gl - PROJ_AT_GROUP)
        if gi + QK_LOOKAHEAD < len(groups):
            pending.append(scores(*groups[gi + QK_LOOKAHEAD]))
        blk = ch * ATTN_PROJ_BLOCKS + bl
        s = jnp.where(band_first if blk == 0 else band, pending[gi], NEG_INF)
        pending[gi] = None
        h0 = kvh * GROUP + a
        sink = jnp.where(top_rows, sink_ref[h0] * LOG2_E, sink_ref[h0 + HEADS_PER_VREG] * LOG2_E)
        m = jnp.maximum(jnp.max(s, axis=-1, keepdims=True), sink)
        es.append(jnp.exp2(s - m).astype(BF16))
        sink_terms.append(jnp.exp2(sink - m))
        if a < HEADS_PER_VREG - 1:
            continue
        krows = slice(blk * w, (blk + 2) * w)
        vstack = jnp.concatenate([vlo_ref[kvh, krows, :], vhi_ref[kvh, krows, :]], axis=0)
        res = jnp.dot(jnp.concatenate(es, axis=1), vstack, preferred_element_type=F32)
        den = res[:, LANES:] + jnp.where(lo_half_2w, sink_terms[0], sink_terms[1])
        o = res[:, :LANES] * (1.0 / den)
        o_parts.append(jnp.concatenate([o[:w], o[w:]], axis=1).astype(BF16))
        es, sink_terms = [], []
        if bl < ATTN_PROJ_BLOCKS - 1:
            continue
        part = jnp.dot(jnp.concatenate(o_parts, axis=0),
                       wo_ref[kvh * GROUP * HEAD_DIM:(kvh + 1) * GROUP * HEAD_DIM, :],
                       preferred_element_type=F32)
        acc = part if acc is None else acc + part
        o_parts = []
        if kvh == N_KV_HEADS - 1:
            rows = slice(ch * rc, (ch + 1) * rc)
            o_ref[0, rows, :] = x_ref[0, rows, :] + acc + bo_ref[...]
            acc = None


def _attn_prompt(x, g, w_qkv, b_qkv, w_o, b_o, sinks, layer, *, nblk):
    b, t, _ = x.shape
    r = nblk * WINDOW
    cos, sin = _rope_tables(jnp.arange(t))
    keys = (nblk + 1) * WINDOW
    return pl.pallas_call(
        functools.partial(_attn_prompt_kernel, nblk=nblk),
        out_shape=(jax.ShapeDtypeStruct((b, t, D_MODEL), F32),
                   jax.ShapeDtypeStruct((b, WINDOW, KV_DIM), F32),
                   jax.ShapeDtypeStruct((b, WINDOW, KV_DIM), F32)),
        grid=(b, t // r),
        in_specs=[
            pl.BlockSpec((1, r, D_MODEL), lambda i, j: (i, j, 0)),
            pl.BlockSpec((r, LANES), lambda i, j: (j, 0)),
            pl.BlockSpec((r, LANES), lambda i, j: (j, 0)),
            _resident((1, D_MODEL)),
            _resident_layer((D_MODEL, QKV_DIM), layer),
            _resident((1, QKV_DIM)),
            _resident_layer((Q_DIM, D_MODEL), layer),
            _resident((1, D_MODEL)),
            pl.BlockSpec(memory_space=pltpu.SMEM),
        ],
        out_specs=(pl.BlockSpec((1, r, D_MODEL), lambda i, j: (i, j, 0)),
                   pl.BlockSpec((1, WINDOW, KV_DIM), lambda i, j: (i, 0, 0)),
                   pl.BlockSpec((1, WINDOW, KV_DIM), lambda i, j: (i, 0, 0))),
        scratch_shapes=[pltpu.VMEM((2 * N_KV_HEADS, LANES, keys), BF16),
                        pltpu.VMEM((N_KV_HEADS, keys, 2 * LANES), BF16),
                        pltpu.VMEM((N_KV_HEADS, keys, 2 * LANES), BF16)],
        compiler_params=pltpu.CompilerParams(
            dimension_semantics=("arbitrary", "arbitrary"), vmem_limit_bytes=VMEM_LIMIT),
        name="attn_prompt",
    )(x, cos, sin, g.reshape(1, D_MODEL), w_qkv, b_qkv.reshape(1, QKV_DIM), w_o,
      b_o.reshape(1, D_MODEL), sinks)


SAMPLE_KEYS = 2 * WINDOW
SAMPLE_UNROLL = 4


def _attn_sample_kernel(x_ref, ck_ref, cv_ref, cos_ref, sin_ref, g_ref, wqkv_ref, bqkv_ref, wo_ref, bo_ref,
                        sinkcol_ref, o_ref, kout_ref, vout_ref, q_s, kn_s, vn_s, kk_s, vv_s, oa_s, *, dt):
    bb = x_ref.shape[0]
    w = WINDOW
    x = x_ref[...].reshape(bb * dt, D_MODEL)
    u = _rms(x, g_ref[...]).astype(BF16)
    qkv = jnp.dot(u, wqkv_ref[...], preferred_element_type=F32) + bqkv_ref[...]
    cos, sin = cos_ref[...], sin_ref[...]
    lane_t = lax.broadcasted_iota(jnp.int32, (bb * dt, LANES), 1)
    first_half = (lane_t % HEAD_DIM) < (HEAD_DIM // 2)
    for c in range(Q_COLS):
        qc = _rope_col(qkv[:, c * LANES:(c + 1) * LANES], cos, sin, first_half)
        q_s[:, c * LANES:(c + 1) * LANES] = qc * QK_SCALE_LOG2
    for c in range(KV_COLS):
        kn_s[:, c * LANES:(c + 1) * LANES] = _rope_col(
            qkv[:, Q_DIM + c * LANES:Q_DIM + (c + 1) * LANES], cos, sin, first_half)
    vn_s[...] = qkv[:, Q_DIM + KV_DIM:]

    kk_s[:, w + dt:, :] = jnp.zeros((SAMPLE_UNROLL, SAMPLE_KEYS - w - dt, KV_DIM), BF16)
    vv_s[:, w + dt:, :] = jnp.zeros((SAMPLE_UNROLL, SAMPLE_KEYS - w - dt, KV_DIM), BF16)

    heads_per_col = N_HEADS // KV_COLS
    qcols_per_col = heads_per_col // HEADS_PER_VREG
    rows = heads_per_col * dt
    lane8 = lax.broadcasted_iota(jnp.int32, (dt, LANES), 1)
    lo8 = lane8 < HEAD_DIM
    t_row = lax.broadcasted_iota(jnp.int32, (rows, SAMPLE_KEYS), 0) % dt
    key = lax.broadcasted_iota(jnp.int32, (rows, SAMPLE_KEYS), 1)
    valid = (key > t_row) & (key <= t_row + w)

    def stage(b, slot):
        r0 = pl.multiple_of(b * dt, dt)
        knew = kn_s[pl.ds(r0, dt), :]
        vnew = vn_s[pl.ds(r0, dt), :]
        kout_ref[b, 0:w - dt, :] = ck_ref[b, dt:w, :]
        kout_ref[b, w - dt:w, :] = knew
        vout_ref[b, 0:w - dt, :] = cv_ref[b, dt:w, :]
        vout_ref[b, w - dt:w, :] = vnew
        kk_s[slot, 0:w, :] = ck_ref[b].astype(BF16)
        kk_s[slot, w:w + dt, :] = knew.astype(BF16)
        vv_s[slot, 0:w, :] = cv_ref[b].astype(BF16)
        vv_s[slot, w:w + dt, :] = vnew.astype(BF16)

    def scores(b, slot, c):
        r0 = pl.multiple_of(b * dt, dt)
        pieces = []
        for h8 in range(heads_per_col):
            qi = c * qcols_per_col + h8 // HEADS_PER_VREG
            qcol = q_s[pl.ds(r0, dt), qi * LANES:(qi + 1) * LANES]
            a, hb = h8 % HEADS_PER_VREG, h8 // GROUP
            src = qcol if a == hb else pltpu.roll(qcol, HEAD_DIM, axis=1)
            keep = lo8 if hb == 0 else jnp.logical_not(lo8)
            pieces.append(jnp.where(keep, src, 0.0))
        qop = jnp.concatenate(pieces, axis=0).astype(BF16)
        return lax.dot_general(qop, kk_s[slot, :, c * LANES:(c + 1) * LANES],
                               (((1,), (1,)), ((), ())), preferred_element_type=F32)

    def place(b, c, o):
        r0 = pl.multiple_of(b * dt, dt)
        for mm in range(qcols_per_col):
            halves = []
            for a in range(HEADS_PER_VREG):
                h8 = mm * HEADS_PER_VREG + a
                hb = h8 // GROUP
                piece = o[h8 * dt:(h8 + 1) * dt, :]
                halves.append(piece if a == hb else pltpu.roll(piece, HEAD_DIM, axis=1))
            oi = c * qcols_per_col + mm
            oa_s[pl.ds(r0, dt), oi * LANES:(oi + 1) * LANES] = jnp.where(lo8, halves[0], halves[1])

    def body(gidx, carry):
        work = [(gidx * SAMPLE_UNROLL + slot, slot, c) for slot in range(SAMPLE_UNROLL) for c in range(KV_COLS)]
        for slot in range(SAMPLE_UNROLL):
            stage(gidx * SAMPLE_UNROLL + slot, slot)
        ss = [scores(b, slot, c) for b, slot, c in work]
        sm = [_softmax_with_sink(jnp.where(valid, s, NEG_INF), sinkcol_ref[c * rows:(c + 1) * rows, :] * LOG2_E)
              for s, (b, slot, c) in zip(ss, work)]
        for (e, rinv), (b, slot, c) in zip(sm, work):
            o = jnp.dot(e, vv_s[slot, :, c * LANES:(c + 1) * LANES], preferred_element_type=F32) * rinv
            place(b, c, o)
        return carry

    lax.fori_loop(0, bb // SAMPLE_UNROLL, body, 0)

    y = x + jnp.dot(oa_s[...].astype(BF16), wo_ref[...], preferred_element_type=F32) + bo_ref[...]
    o_ref[...] = y.reshape(bb, dt, D_MODEL)


def _attn_sample(x, cache_k, cache_v, g, w_qkv, b_qkv, w_o, b_o, sinks, layer, *, bb):
    b, dt, _ = x.shape
    cos, sin = _rope_tables(PAST_LEN + jnp.arange(dt))
    cos, sin = jnp.tile(cos, (bb, 1)), jnp.tile(sin, (bb, 1))
    sinkcol = jnp.repeat(sinks, dt).reshape(N_HEADS * dt, 1)
    n = bb * dt
    return pl.pallas_call(
        functools.partial(_attn_sample_kernel, dt=dt),
        out_shape=(jax.ShapeDtypeStruct((b, dt, D_MODEL), F32),
                   jax.ShapeDtypeStruct((b, WINDOW, KV_DIM), F32),
                   jax.ShapeDtypeStruct((b, WINDOW, KV_DIM), F32)),
        grid=(b // bb,),
        in_specs=[
            pl.BlockSpec((bb, dt, D_MODEL), lambda i: (i, 0, 0)),
            pl.BlockSpec((None, bb, WINDOW, KV_DIM), lambda i: (layer, i, 0, 0)),
            pl.BlockSpec((None, bb, WINDOW, KV_DIM), lambda i: (layer, i, 0, 0)),
            _resident((n, LANES)),
            _resident((n, LANES)),
            _resident((1, D_MODEL)),
            _resident_layer((D_MODEL, QKV_DIM), layer),
            _resident((1, QKV_DIM)),
            _resident_layer((Q_DIM, D_MODEL), layer),
            _resident((1, D_MODEL)),
            _resident((N_HEADS * dt, 1)),
        ],
        out_specs=(pl.BlockSpec((bb, dt, D_MODEL), lambda i: (i, 0, 0)),
                   pl.BlockSpec((bb, WINDOW, KV_DIM), lambda i: (i, 0, 0)),
                   pl.BlockSpec((bb, WINDOW, KV_DIM), lambda i: (i, 0, 0))),
        scratch_shapes=[pltpu.VMEM((n, Q_DIM), F32),
                        pltpu.VMEM((n, KV_DIM), F32),
                        pltpu.VMEM((n, KV_DIM), F32),
                        pltpu.VMEM((SAMPLE_UNROLL, SAMPLE_KEYS, KV_DIM), BF16),
                        pltpu.VMEM((SAMPLE_UNROLL, SAMPLE_KEYS, KV_DIM), BF16),
                        pltpu.VMEM((n, Q_DIM), F32)],
        compiler_params=pltpu.CompilerParams(
            dimension_semantics=("arbitrary",), vmem_limit_bytes=VMEM_LIMIT),
        name="attn_sample",
    )(x, cache_k, cache_v, cos, sin, g.reshape(1, D_MODEL), w_qkv, b_qkv.reshape(1, QKV_DIM), w_o,
      b_o.reshape(1, D_MODEL), sinkcol)


def kernel(x_prompt, x_sample, state_pool, cache_k, cache_v, norm_ffn1, ffn1_w_in, ffn1_w_out, norm_mix,
           norm_ffn2, ffn2_w_in, ffn2_w_out, pool_w, pool_scale, attn_w_qkv, attn_b_qkv, attn_w_o, attn_b_o,
           attn_sinks, norm_final):
    batch, seq, _ = x_prompt.shape
    dec_batch, dec_seq, _ = x_sample.shape
    depth = norm_ffn1.shape[0]
    n_mixers = 2
    w1_in, w1_out = ffn1_w_in.astype(BF16), ffn1_w_out.astype(BF16)
    w2_in, w2_out = ffn2_w_in.astype(BF16), ffn2_w_out.astype(BF16)
    pw = pool_w.astype(BF16)
    wqkv, wo = attn_w_qkv.astype(BF16), attn_w_o.astype(BF16)
    n_attn = cache_k.shape[0]
    ck = cache_k.reshape(n_attn, dec_batch, WINDOW, KV_DIM)
    cv = cache_v.reshape(n_attn, dec_batch, WINDOW, KV_DIM)

    xp = x_prompt.reshape(batch * seq, D_MODEL)
    xs = x_sample.reshape(dec_batch * dec_seq, D_MODEL)
    pool_p, pool_s, kp_l, vp_l, ks_l, vs_l = [], [], [], [], [], []
    for i in range(depth):
        xp = _ffn(xp, norm_ffn1[i], w1_in, w1_out, i, tm=FFN_TILE)
        xs = _ffn(xs, norm_ffn1[i], w1_in, w1_out, i, tm=FFN_TILE)
        j = i // n_mixers
        xp3 = xp.reshape(batch, seq, D_MODEL)
        xs3 = xs.reshape(dec_batch, dec_seq, D_MODEL)
        if i % n_mixers == 0:
            xp3, sp = _pool(xp3, None, norm_mix[i], pw, pool_scale[j], j, start_pos=0, bb=1, tt=512)
            xs3, ss = _pool(xs3, state_pool, norm_mix[i], pw, pool_scale[j], j, start_pos=PAST_LEN,
                            bb=32, tt=dec_seq)
            pool_p.append(sp)
            pool_s.append(ss)
        else:
            xp3, kp, vp = _attn_prompt(xp3, norm_mix[i], wqkv, attn_b_qkv[j], wo, attn_b_o[j], attn_sinks[j], j,
                                       nblk=ATTN_BLOCKS_PER_STEP)
            xs3, kn, vn = _attn_sample(xs3, ck, cv, norm_mix[i], wqkv, attn_b_qkv[j], wo, attn_b_o[j],
                                       attn_sinks[j], j, bb=32)
            kp_l.append(kp.reshape(batch, WINDOW, N_KV_HEADS, HEAD_DIM))
            vp_l.append(vp.reshape(batch, WINDOW, N_KV_HEADS, HEAD_DIM))
            ks_l.append(kn.reshape(dec_batch, WINDOW, N_KV_HEADS, HEAD_DIM))
            vs_l.append(vn.reshape(dec_batch, WINDOW, N_KV_HEADS, HEAD_DIM))
        xp = xp3.reshape(batch * seq, D_MODEL)
        xs = xs3.reshape(dec_batch * dec_seq, D_MODEL)
        fg = norm_final if i == depth - 1 else None
        xp = _ffn(xp, norm_ffn2[i], w2_in, w2_out, i, fg, tm=FFN_TILE)
        xs = _ffn(xs, norm_ffn2[i], w2_in, w2_out, i, fg, tm=FFN_TILE)
    return (xp.reshape(batch, seq, D_MODEL), xs.reshape(dec_batch, dec_seq, D_MODEL),
            jnp.stack(pool_p), jnp.stack(pool_s), jnp.stack(kp_l), jnp.stack(vp_l),
            jnp.stack(ks_l), jnp.stack(vs_l))
```

```python
import functools

import jax
import jax.numpy as jnp
from jax import lax
from jax.experimental import pallas as pl
from jax.experimental.pallas import tpu as pltpu

D_MODEL = 1024
D_FF = 2816
POOL_WINDOWS = (2, 4, 8, 16)
POOL_GROUP_DIM = D_MODEL // len(POOL_WINDOWS)
POOL_PREFIX = max(POOL_WINDOWS) - 1
N_HEADS = 16
N_KV_HEADS = 4
HEAD_DIM = 64
GROUP = N_HEADS // N_KV_HEADS
WINDOW = 128
ROPE_THETA = 10000.0
Q_DIM = N_HEADS * HEAD_DIM
KV_DIM = N_KV_HEADS * HEAD_DIM
QKV_DIM = Q_DIM + 2 * KV_DIM
RMS_EPS = 1e-6
NEG_INF = -1e30
PAST_LEN = 8192

LANES = 128
HEADS_PER_VREG = LANES // HEAD_DIM
Q_COLS = Q_DIM // LANES
KV_COLS = KV_DIM // LANES
POOL_HALO = 16
FF_CHUNK = 256
FFN_TILE = 512
WEIGHT_CAST_STEPS = 16
ATTN_BLOCKS_PER_STEP = 8
ATTN_PROJ_BLOCKS = 2
PROJ_PIECES = QKV_DIM // (2 * LANES)
PROJ_AT_GROUP = 2
QK_LOOKAHEAD = 3
LOG2_E = 1.4426950408889634
QK_SCALE_LOG2 = HEAD_DIM ** -0.5 * LOG2_E
VMEM_LIMIT = 52 * 1024 * 1024

F32 = jnp.float32
BF16 = jnp.bfloat16


def _rms(x, g):
    ms = jnp.mean(x * x, axis=-1, keepdims=True)
    return x * lax.rsqrt(ms + RMS_EPS) * g


def _resident(shape):
    nd = len(shape)
    return pl.BlockSpec(shape, lambda *_: (0,) * nd, pipeline_mode=pl.Buffered(1))


def _resident_layer(shape, layer):
    nd = len(shape)
    return pl.BlockSpec((None,) + shape, lambda *_: (layer,) + (0,) * nd, pipeline_mode=pl.Buffered(1))


def _ffn_kernel(*refs, n_sample_tiles, final_norm, convert_next):
    refs = list(refs)
    xp_ref, xs_ref, g_ref, win_ref, wout_ref = refs[:5]
    del refs[:5]
    gf_ref = refs.pop(0) if final_norm else None
    if convert_next:
        nwin_ref, nwout_ref = refs.pop(0), refs.pop(0)
    op_ref, os_ref = refs.pop(0), refs.pop(0)
    if convert_next:
        cwin_ref, cwout_ref = refs.pop(0), refs.pop(0)
    (h_ref,) = refs

    i = pl.program_id(0)
    is_sample = i < n_sample_tiles
    x = jnp.where(is_sample, xs_ref[...], xp_ref[...])
    xn = _rms(x, g_ref[...]).astype(BF16)
    for c in range(D_FF // FF_CHUNK):
        lo = c * FF_CHUNK
        gate = jnp.dot(xn, win_ref[:, lo:lo + FF_CHUNK], preferred_element_type=F32)
        up = jnp.dot(xn, win_ref[:, D_FF + lo:D_FF + lo + FF_CHUNK], preferred_element_type=F32)
        h_ref[:, lo:lo + FF_CHUNK] = (gate * jax.nn.sigmoid(gate) * up).astype(BF16)
    y = x + 0.5 * jnp.dot(h_ref[...], wout_ref[...], preferred_element_type=F32)
    if final_norm:
        y = _rms(y, gf_ref[...])

    op_ref[...] = y

    @pl.when(is_sample)
    def _():
        os_ref[...] = op_ref[...]

    if convert_next:
        cwin_ref[...] = nwin_ref[...].astype(BF16)
        cwout_ref[...] = nwout_ref[...].astype(BF16)


def _ffn(xp, xs, g, w_in, w_out, final_gain=None, next_weights=None, *, tm):
    n_p, n_s = xp.shape[0] // tm, xs.shape[0] // tm
    final_norm = final_gain is not None
    convert_next = next_weights is not None

    def sample_tile(i):
        return (jnp.minimum(i, n_s - 1), 0)

    def prompt_tile(i):
        return (jnp.maximum(i - n_s, 0), 0)

    in_specs = [
        pl.BlockSpec((tm, D_MODEL), prompt_tile),
        pl.BlockSpec((tm, D_MODEL), sample_tile),
        _resident((1, D_MODEL)),
        _resident((D_MODEL, 2 * D_FF)),
        _resident((D_FF, D_MODEL)),
    ]
    args = [xp, xs, g.reshape(1, D_MODEL), w_in, w_out]
    out_shape = [jax.ShapeDtypeStruct(xp.shape, F32), jax.ShapeDtypeStruct(xs.shape, F32)]
    out_specs = [pl.BlockSpec((tm, D_MODEL), prompt_tile), pl.BlockSpec((tm, D_MODEL), sample_tile)]
    if final_norm:
        in_specs.append(_resident((1, D_MODEL)))
        args.append(final_gain.reshape(1, D_MODEL))
    if convert_next:
        nw_in, nw_out, layer = next_weights
        rows_in, rows_out = D_MODEL // WEIGHT_CAST_STEPS, D_FF // WEIGHT_CAST_STEPS

        def slab(i):
            return (jnp.minimum(i, WEIGHT_CAST_STEPS - 1), 0)

        in_specs += [pl.BlockSpec((None, rows_in, 2 * D_FF), lambda i: (layer,) + slab(i)),
                     pl.BlockSpec((None, rows_out, D_MODEL), lambda i: (layer,) + slab(i))]
        args += [nw_in, nw_out]
        out_shape += [jax.ShapeDtypeStruct((D_MODEL, 2 * D_FF), BF16), jax.ShapeDtypeStruct((D_FF, D_MODEL), BF16)]
        out_specs += [pl.BlockSpec((rows_in, 2 * D_FF), slab), pl.BlockSpec((rows_out, D_MODEL), slab)]
    return pl.pallas_call(
        functools.partial(_ffn_kernel, n_sample_tiles=n_s, final_norm=final_norm, convert_next=convert_next),
        out_shape=tuple(out_shape),
        grid=(n_p + n_s,),
        in_specs=in_specs,
        out_specs=tuple(out_specs),
        scratch_shapes=[pltpu.VMEM((tm, D_FF), BF16)],
        compiler_params=pltpu.CompilerParams(
            dimension_semantics=("arbitrary",), vmem_limit_bytes=VMEM_LIMIT),
        name="ffn_final" if final_norm else "ffn",
    )(*args)


def _pool_kernel(x_ref, *rest, has_prefix, start_pos, tt):
    if has_prefix:
        pre_ref, g_ref, w_ref, sc_ref, o_ref, st_ref, ext_ref = rest
    else:
        g_ref, w_ref, sc_ref, o_ref, st_ref, ext_ref = rest
    ti = pl.program_id(1)
    bb = x_ref.shape[0]
    te = tt + POOL_HALO

    @pl.when(ti == 0)
    def _():
        ext_ref[:, 0:POOL_HALO, :] = jnp.zeros((bb, POOL_HALO, D_MODEL), F32)
        if has_prefix:
            ext_ref[:, POOL_HALO - POOL_PREFIX:POOL_HALO, :] = pre_ref[...]

    if tt >= POOL_HALO:
        @pl.when(ti > 0)
        def _():
            ext_ref[:, 0:POOL_HALO, :] = ext_ref[:, tt:te, :]

    x = x_ref[...]
    u = _rms(x, g_ref[...])
    ext_ref[:, POOL_HALO:te, :] = u

    n_seen = start_pos + ti * tt + lax.broadcasted_iota(jnp.int32, (1, tt, 1), 1) + 1
    for gi, wg in enumerate(POOL_WINDOWS):
        sl = slice(gi * POOL_GROUP_DIM, (gi + 1) * POOL_GROUP_DIM)
        s = ext_ref[:, :, sl].reshape(bb * te, POOL_GROUP_DIM)
        span = 1
        while span < wg:
            s = s + pltpu.roll(s, span, axis=0)
            span *= 2
        s = s.reshape(bb, te, POOL_GROUP_DIM)[:, POOL_HALO:, :]
        cnt = jnp.minimum(n_seen, wg).astype(F32)
        p = s / cnt - u[:, :, sl]
        p2 = p.reshape(bb * tt, POOL_GROUP_DIM).astype(BF16)
        y = jnp.dot(p2, w_ref[gi], preferred_element_type=F32).reshape(bb, tt, POOL_GROUP_DIM)
        o_ref[:, :, sl] = x[:, :, sl] + y * sc_ref[:, :, sl]

    @pl.when(ti == pl.num_programs(1) - 1)
    def _():
        st_ref[...] = ext_ref[:, te - POOL_PREFIX:te, :]


def _pool(x, prefix, g, w, scale, layer, *, start_pos, bb, tt):
    b, t, _ = x.shape
    has_prefix = prefix is not None
    in_specs = [pl.BlockSpec((bb, tt, D_MODEL), lambda i, j: (i, j, 0))]
    args = [x]
    if has_prefix:
        in_specs.append(pl.BlockSpec((None, bb, POOL_PREFIX, D_MODEL), lambda i, j: (layer, i, 0, 0)))
        args.append(prefix)
    in_specs += [
        _resident((1, 1, D_MODEL)),
        _resident_layer((len(POOL_WINDOWS), POOL_GROUP_DIM, POOL_GROUP_DIM), layer),
        _resident((1, 1, D_MODEL)),
    ]
    args += [g.reshape(1, 1, D_MODEL), w, scale.reshape(1, 1, D_MODEL)]
    return pl.pallas_call(
        functools.partial(_pool_kernel, has_prefix=has_prefix, start_pos=start_pos, tt=tt),
        out_shape=(jax.ShapeDtypeStruct((b, t, D_MODEL), F32),
                   jax.ShapeDtypeStruct((b, POOL_PREFIX, D_MODEL), F32)),
        grid=(b // bb, t // tt),
        in_specs=in_specs,
        out_specs=(pl.BlockSpec((bb, tt, D_MODEL), lambda i, j: (i, j, 0)),
                   pl.BlockSpec((bb, POOL_PREFIX, D_MODEL), lambda i, j: (i, 0, 0))),
        scratch_shapes=[pltpu.VMEM((bb, tt + POOL_HALO, D_MODEL), F32)],
        compiler_params=pltpu.CompilerParams(
            dimension_semantics=("arbitrary", "arbitrary"), vmem_limit_bytes=VMEM_LIMIT),
        name="pool_sample" if has_prefix else "pool_prompt",
    )(*args)


def _rope_tables(pos):
    half = HEAD_DIM // 2
    inv = ROPE_THETA ** (-jnp.arange(half, dtype=F32) / half)
    ang = pos.astype(F32)[:, None] * inv[None, :]
    cos, sin = jnp.cos(ang), jnp.sin(ang)
    cos64 = jnp.concatenate([cos, cos], axis=-1)
    sin64 = jnp.concatenate([-sin, sin], axis=-1)
    return jnp.tile(cos64, (1, HEADS_PER_VREG)), jnp.tile(sin64, (1, HEADS_PER_VREG))


def _rope_col(xc, cos, sin, first_half):
    half = HEAD_DIM // 2
    partner = jnp.where(first_half, pltpu.roll(xc, LANES - half, axis=1), pltpu.roll(xc, half, axis=1))
    return xc * cos + partner * sin


def _head_half_variants(col, lo_half):
    sw = pltpu.roll(col, HEAD_DIM, axis=1)
    zero = jnp.zeros_like(col)
    return {
        (0, 0): jnp.where(lo_half, col, zero),
        (0, 1): jnp.where(lo_half, zero, sw),
        (1, 0): jnp.where(lo_half, sw, zero),
        (1, 1): jnp.where(lo_half, zero, col),
    }


def _softmax_with_sink(s, sink):
    m = jnp.maximum(jnp.max(s, axis=-1, keepdims=True), sink)
    e = jnp.exp2(s - m)
    denom = jnp.sum(e, axis=-1, keepdims=True) + jnp.exp2(sink - m)
    return e.astype(BF16), 1.0 / denom


def _attn_prompt_kernel(x_ref, cos_ref, sin_ref, g_ref, wqkv_ref, bqkv_ref, wo_ref, bo_ref, sink_ref,
                        o_ref, kout_ref, vout_ref, kt_ref, vlo_ref, vhi_ref, *, nblk):
    j = pl.program_id(1)
    w = WINDOW
    r = nblk * w

    @pl.when(j == 0)
    def _():
        kt_ref[...] = jnp.zeros(kt_ref.shape, BF16)
        vlo_ref[:, 0:w, 0:LANES] = jnp.zeros((N_KV_HEADS, w, LANES), BF16)
        vhi_ref[:, 0:w, 0:LANES] = jnp.zeros((N_KV_HEADS, w, LANES), BF16)
        keys = vlo_ref.shape[1]
        ones_lo = (lax.broadcasted_iota(jnp.int32, (N_KV_HEADS, keys, LANES), 2) < HEAD_DIM).astype(BF16)
        vlo_ref[:, :, LANES:] = ones_lo
        vhi_ref[:, :, LANES:] = 1 - ones_lo

    @pl.when(j > 0)
    def _():
        kt_ref[:, :, 0:w] = kt_ref[:, :, r:r + w]
        vlo_ref[:, 0:w, 0:LANES] = vlo_ref[:, r:r + w, 0:LANES]
        vhi_ref[:, 0:w, 0:LANES] = vhi_ref[:, r:r + w, 0:LANES]

    rc = ATTN_PROJ_BLOCKS * w
    n_chunks = nblk // ATTN_PROJ_BLOCKS
    lane = lax.broadcasted_iota(jnp.int32, (rc, LANES), 1)
    first_half = (lane % HEAD_DIM) < (HEAD_DIM // 2)
    lo_half = lane < HEAD_DIM
    lo_half_2w = lax.broadcasted_iota(jnp.int32, (2 * w, LANES), 1) < HEAD_DIM

    u_of, qcols_of = {}, {}

    def project_piece(ch, t):
        rows = slice(ch * rc, (ch + 1) * rc)
        if t == 0:
            u_of[ch] = _rms(x_ref[0, rows, :], g_ref[...]).astype(BF16)
            qcols_of[ch] = []
        cols = slice(t * 2 * LANES, (t + 1) * 2 * LANES)
        tile = jnp.dot(u_of[ch], wqkv_ref[:, cols], preferred_element_type=F32) + bqkv_ref[:, cols]
        halves = [tile[:, :LANES], tile[:, LANES:]]
        if t < Q_COLS // 2:
            cos, sin = cos_ref[rows, :], sin_ref[rows, :]
            for hc in halves:
                qcols_of[ch].append((_rope_col(hc, cos, sin, first_half) * QK_SCALE_LOG2).astype(BF16))
            return
        is_k = t == Q_COLS // 2
        if is_k:
            cos, sin = cos_ref[rows, :], sin_ref[rows, :]
            halves = [_rope_col(hc, cos, sin, first_half) for hc in halves]
        krows = slice(w + ch * rc, w + (ch + 1) * rc)
        for c, col in enumerate(halves):
            if ch == n_chunks - 1:
                out_ref = kout_ref if is_k else vout_ref
                out_ref[0, :, c * LANES:(c + 1) * LANES] = col[rc - w:]
            if is_k:
                col_t = col.T.astype(BF16)
                for hb in range(HEADS_PER_VREG):
                    kvh = c * HEADS_PER_VREG + hb
                    head_t = col_t[hb * HEAD_DIM:(hb + 1) * HEAD_DIM, :]
                    for a in range(HEADS_PER_VREG):
                        kt_ref[kvh * HEADS_PER_VREG + a, a * HEAD_DIM:(a + 1) * HEAD_DIM, krows] = head_t
            else:
                var = _head_half_variants(col, lo_half)
                for hb in range(HEADS_PER_VREG):
                    kvh = c * HEADS_PER_VREG + hb
                    vlo_ref[kvh, krows, 0:LANES] = var[(hb, 0)].astype(BF16)
                    vhi_ref[kvh, krows, 0:LANES] = var[(hb, 1)].astype(BF16)

    for t in range(PROJ_PIECES):
        project_piece(0, t)

    row = lax.broadcasted_iota(jnp.int32, (2 * w, 2 * w), 0) % w
    col = lax.broadcasted_iota(jnp.int32, (2 * w, 2 * w), 1)
    band = (col > row) & (col <= row + w)
    first_key = jnp.where(j > 0, 0, w)
    band_first = band & (col >= first_key)
    top_rows = lax.broadcasted_iota(jnp.int32, (2 * w, 1), 0) < w

    groups = [(ch, kvh, bl, a) for ch in range(n_chunks) for kvh in range(N_KV_HEADS)
              for bl in range(ATTN_PROJ_BLOCKS) for a in range(HEADS_PER_VREG)]

    def scores(ch, kvh, bl, a):
        qcols = qcols_of[ch]
        qrows = slice(bl * w, (bl + 1) * w)
        blk = ch * ATTN_PROJ_BLOCKS + bl
        qop = jnp.concatenate([qcols[2 * kvh][qrows], qcols[2 * kvh + 1][qrows]], axis=0)
        return jnp.dot(qop, kt_ref[kvh * HEADS_PER_VREG + a, :, blk * w:(blk + 2) * w],
                       preferred_element_type=F32)

    groups_per_chunk = len(groups) // n_chunks
    pending = [scores(*g) for g in groups[:QK_LOOKAHEAD]]
    es, sink_terms, o_parts, acc = [], [], [], None
    for gi, (ch, kvh, bl, a) in enumerate(groups):
        gl = gi % groups_per_chunk
        if 0 <= gl - PROJ_AT_GROUP < PROJ_PIECES and ch + 1 < n_chunks:
            project_piece(ch + 1, gl - PROJ_AT_GROUP)
        if gi + QK_LOOKAHEAD < len(groups):
            pending.append(scores(*groups[gi + QK_LOOKAHEAD]))
        blk = ch * ATTN_PROJ_BLOCKS + bl
        s = jnp.where(band_first if blk == 0 else band, pending[gi], NEG_INF)
        pending[gi] = None
        h0 = kvh * GROUP + a
        sink = jnp.where(top_rows, sink_ref[h0] * LOG2_E, sink_ref[h0 + HEADS_PER_VREG] * LOG2_E)
        m = jnp.maximum(jnp.max(s, axis=-1, keepdims=True), sink)
        es.append(jnp.exp2(s - m).astype(BF16))
        sink_terms.append(jnp.exp2(sink - m))
        if a < HEADS_PER_VREG - 1:
            continue
        krows = slice(blk * w, (blk + 2) * w)
        vstack = jnp.concatenate([vlo_ref[kvh, krows, :], vhi_ref[kvh, krows, :]], axis=0)
        res = jnp.dot(jnp.concatenate(es, axis=1), vstack, preferred_element_type=F32)
        den = res[:, LANES:] + jnp.where(lo_half_2w, sink_terms[0], sink_terms[1])
        o = res[:, :LANES] * (1.0 / den)
        o_parts.append(jnp.concatenate([o[:w], o[w:]], axis=1).astype(BF16))
        es, sink_terms = [], []
        if bl < ATTN_PROJ_BLOCKS - 1:
            continue
        part = jnp.dot(jnp.concatenate(o_parts, axis=0),
                       wo_ref[kvh * GROUP * HEAD_DIM:(kvh + 1) * GROUP * HEAD_DIM, :],
                       preferred_element_type=F32)
        acc = part if acc is None else acc + part
        o_parts = []
        if kvh == N_KV_HEADS - 1:
            rows = slice(ch * rc, (ch + 1) * rc)
            o_ref[0, rows, :] = x_ref[0, rows, :] + acc + bo_ref[...]
            acc = None


def _attn_prompt(x, g, w_qkv, b_qkv, w_o, b_o, sinks, layer, *, nblk):
    b, t, _ = x.shape
    r = nblk * WINDOW
    cos, sin = _rope_tables(jnp.arange(t))
    keys = (nblk + 1) * WINDOW
    return pl.pallas_call(
        functools.partial(_attn_prompt_kernel, nblk=nblk),
        out_shape=(jax.ShapeDtypeStruct((b, t, D_MODEL), F32),
                   jax.ShapeDtypeStruct((b, WINDOW, KV_DIM), F32),
                   jax.ShapeDtypeStruct((b, WINDOW, KV_DIM), F32)),
        grid=(b, t // r),
        in_specs=[
            pl.BlockSpec((1, r, D_MODEL), lambda i, j: (i, j, 0)),
            pl.BlockSpec((r, LANES), lambda i, j: (j, 0)),
            pl.BlockSpec((r, LANES), lambda i, j: (j, 0)),
            _resident((1, D_MODEL)),
            _resident_layer((D_MODEL, QKV_DIM), layer),
            _resident((1, QKV_DIM)),
            _resident_layer((Q_DIM, D_MODEL), layer),
            _resident((1, D_MODEL)),
            pl.BlockSpec(memory_space=pltpu.SMEM),
        ],
        out_specs=(pl.BlockSpec((1, r, D_MODEL), lambda i, j: (i, j, 0)),
                   pl.BlockSpec((1, WINDOW, KV_DIM), lambda i, j: (i, 0, 0)),
                   pl.BlockSpec((1, WINDOW, KV_DIM), lambda i, j: (i, 0, 0))),
        scratch_shapes=[pltpu.VMEM((2 * N_KV_HEADS, LANES, keys), BF16),
                        pltpu.VMEM((N_KV_HEADS, keys, 2 * LANES), BF16),
                        pltpu.VMEM((N_KV_HEADS, keys, 2 * LANES), BF16)],
        compiler_params=pltpu.CompilerParams(
            dimension_semantics=("arbitrary", "arbitrary"), vmem_limit_bytes=VMEM_LIMIT),
        name="attn_prompt",
    )(x, cos, sin, g.reshape(1, D_MODEL), w_qkv, b_qkv.reshape(1, QKV_DIM), w_o,
      b_o.reshape(1, D_MODEL), sinks)


SAMPLE_KEYS = 2 * WINDOW
SAMPLE_UNROLL = 4


def _attn_sample_kernel(x_ref, ck_ref, cv_ref, cos_ref, sin_ref, g_ref, wqkv_ref, bqkv_ref, wo_ref, bo_ref,
                        sinkcol_ref, o_ref, kout_ref, vout_ref, q_s, kn_s, vn_s, kk_s, vv_s, oa_s, *, dt):
    bb = x_ref.shape[0]
    w = WINDOW
    x = x_ref[...].reshape(bb * dt, D_MODEL)
    u = _rms(x, g_ref[...]).astype(BF16)
    qkv = jnp.dot(u, wqkv_ref[...], preferred_element_type=F32) + bqkv_ref[...]
    cos, sin = cos_ref[...], sin_ref[...]
    lane_t = lax.broadcasted_iota(jnp.int32, (bb * dt, LANES), 1)
    first_half = (lane_t % HEAD_DIM) < (HEAD_DIM // 2)
    for c in range(Q_COLS):
        qc = _rope_col(qkv[:, c * LANES:(c + 1) * LANES], cos, sin, first_half)
        q_s[:, c * LANES:(c + 1) * LANES] = qc * QK_SCALE_LOG2
    for c in range(KV_COLS):
        kn_s[:, c * LANES:(c + 1) * LANES] = _rope_col(
            qkv[:, Q_DIM + c * LANES:Q_DIM + (c + 1) * LANES], cos, sin, first_half)
    vn_s[...] = qkv[:, Q_DIM + KV_DIM:]

    kk_s[:, w + dt:, :] = jnp.zeros((SAMPLE_UNROLL, SAMPLE_KEYS - w - dt, KV_DIM), BF16)
    vv_s[:, w + dt:, :] = jnp.zeros((SAMPLE_UNROLL, SAMPLE_KEYS - w - dt, KV_DIM), BF16)

    heads_per_col = N_HEADS // KV_COLS
    qcols_per_col = heads_per_col // HEADS_PER_VREG
    rows = heads_per_col * dt
    lane8 = lax.broadcasted_iota(jnp.int32, (dt, LANES), 1)
    lo8 = lane8 < HEAD_DIM
    t_row = lax.broadcasted_iota(jnp.int32, (rows, SAMPLE_KEYS), 0) % dt
    key = lax.broadcasted_iota(jnp.int32, (rows, SAMPLE_KEYS), 1)
    valid = (key > t_row) & (key <= t_row + w)

    def stage(b, slot):
        r0 = pl.multiple_of(b * dt, dt)
        knew = kn_s[pl.ds(r0, dt), :]
        vnew = vn_s[pl.ds(r0, dt), :]
        kout_ref[b, 0:w - dt, :] = ck_ref[b, dt:w, :]
        kout_ref[b, w - dt:w, :] = knew
        vout_ref[b, 0:w - dt, :] = cv_ref[b, dt:w, :]
        vout_ref[b, w - dt:w, :] = vnew
        kk_s[slot, 0:w, :] = ck_ref[b].astype(BF16)
        kk_s[slot, w:w + dt, :] = knew.astype(BF16)
        vv_s[slot, 0:w, :] = cv_ref[b].astype(BF16)
        vv_s[slot, w:w + dt, :] = vnew.astype(BF16)

    def scores(b, slot, c):
        r0 = pl.multiple_of(b * dt, dt)
        pieces = []
        for h8 in range(heads_per_col):
            qi = c * qcols_per_col + h8 // HEADS_PER_VREG
            qcol = q_s[pl.ds(r0, dt), qi * LANES:(qi + 1) * LANES]
            a, hb = h8 % HEADS_PER_VREG, h8 // GROUP
            src = qcol if a == hb else pltpu.roll(qcol, HEAD_DIM, axis=1)
            keep = lo8 if hb == 0 else jnp.logical_not(lo8)
            pieces.append(jnp.where(keep, src, 0.0))
        qop = jnp.concatenate(pieces, axis=0).astype(BF16)
        return lax.dot_general(qop, kk_s[slot, :, c * LANES:(c + 1) * LANES],
                               (((1,), (1,)), ((), ())), preferred_element_type=F32)

    def place(b, c, o):
        r0 = pl.multiple_of(b * dt, dt)
        for mm in range(qcols_per_col):
            halves = []
            for a in range(HEADS_PER_VREG):
                h8 = mm * HEADS_PER_VREG + a
                hb = h8 // GROUP
                piece = o[h8 * dt:(h8 + 1) * dt, :]
                halves.append(piece if a == hb else pltpu.roll(piece, HEAD_DIM, axis=1))
            oi = c * qcols_per_col + mm
            oa_s[pl.ds(r0, dt), oi * LANES:(oi + 1) * LANES] = jnp.where(lo8, halves[0], halves[1])

    def body(gidx, carry):
        work = [(gidx * SAMPLE_UNROLL + slot, slot, c) for slot in range(SAMPLE_UNROLL) for c in range(KV_COLS)]
        for slot in range(SAMPLE_UNROLL):
            stage(gidx * SAMPLE_UNROLL + slot, slot)
        ss = [scores(b, slot, c) for b, slot, c in work]
        sm = [_softmax_with_sink(jnp.where(valid, s, NEG_INF), sinkcol_ref[c * rows:(c + 1) * rows, :] * LOG2_E)
              for s, (b, slot, c) in zip(ss, work)]
        for (e, rinv), (b, slot, c) in zip(sm, work):
            o = jnp.dot(e, vv_s[slot, :, c * LANES:(c + 1) * LANES], preferred_element_type=F32) * rinv
            place(b, c, o)
        return carry

    lax.fori_loop(0, bb // SAMPLE_UNROLL, body, 0)

    y = x + jnp.dot(oa_s[...].astype(BF16), wo_ref[...], preferred_element_type=F32) + bo_ref[...]
    o_ref[...] = y.reshape(bb, dt, D_MODEL)


def _attn_sample(x, cache_k, cache_v, g, w_qkv, b_qkv, w_o, b_o, sinks, layer, *, bb):
    b, dt, _ = x.shape
    cos, sin = _rope_tables(PAST_LEN + jnp.arange(dt))
    cos, sin = jnp.tile(cos, (bb, 1)), jnp.tile(sin, (bb, 1))
    sinkcol = jnp.repeat(sinks, dt).reshape(N_HEADS * dt, 1)
    n = bb * dt
    return pl.pallas_call(
        functools.partial(_attn_sample_kernel, dt=dt),
        out_shape=(jax.ShapeDtypeStruct((b, dt, D_MODEL), F32),
                   jax.ShapeDtypeStruct((b, WINDOW, KV_DIM), F32),
                   jax.ShapeDtypeStruct((b, WINDOW, KV_DIM), F32)),
        grid=(b // bb,),
        in_specs=[
            pl.BlockSpec((bb, dt, D_MODEL), lambda i: (i, 0, 0)),
            pl.BlockSpec((None, bb, WINDOW, KV_DIM), lambda i: (layer, i, 0, 0)),
            pl.BlockSpec((None, bb, WINDOW, KV_DIM), lambda i: (layer, i, 0, 0)),
            _resident((n, LANES)),
            _resident((n, LANES)),
            _resident((1, D_MODEL)),
            _resident_layer((D_MODEL, QKV_DIM), layer),
            _resident((1, QKV_DIM)),
            _resident_layer((Q_DIM, D_MODEL), layer),
            _resident((1, D_MODEL)),
            _resident((N_HEADS * dt, 1)),
        ],
        out_specs=(pl.BlockSpec((bb, dt, D_MODEL), lambda i: (i, 0, 0)),
                   pl.BlockSpec((bb, WINDOW, KV_DIM), lambda i: (i, 0, 0)),
                   pl.BlockSpec((bb, WINDOW, KV_DIM), lambda i: (i, 0, 0))),
        scratch_shapes=[pltpu.VMEM((n, Q_DIM), F32),
                        pltpu.VMEM((n, KV_DIM), F32),
                        pltpu.VMEM((n, KV_DIM), F32),
                        pltpu.VMEM((SAMPLE_UNROLL, SAMPLE_KEYS, KV_DIM), BF16),
                        pltpu.VMEM((SAMPLE_UNROLL, SAMPLE_KEYS, KV_DIM), BF16),
                        pltpu.VMEM((n, Q_DIM), F32)],
        compiler_params=pltpu.CompilerParams(
            dimension_semantics=("arbitrary",), vmem_limit_bytes=VMEM_LIMIT),
        name="attn_sample",
    )(x, cache_k, cache_v, cos, sin, g.reshape(1, D_MODEL), w_qkv, b_qkv.reshape(1, QKV_DIM), w_o,
      b_o.reshape(1, D_MODEL), sinkcol)


def kernel(x_prompt, x_sample, state_pool, cache_k, cache_v, norm_ffn1, ffn1_w_in, ffn1_w_out, norm_mix,
           norm_ffn2, ffn2_w_in, ffn2_w_out, pool_w, pool_scale, attn_w_qkv, attn_b_qkv, attn_w_o, attn_b_o,
           attn_sinks, norm_final):
    batch, seq, _ = x_prompt.shape
    dec_batch, dec_seq, _ = x_sample.shape
    depth = norm_ffn1.shape[0]
    n_mixers = 2
    w_in, w_out = ffn1_w_in[0].astype(BF16), ffn1_w_out[0].astype(BF16)
    pw = pool_w.astype(BF16)
    wqkv, wo = attn_w_qkv.astype(BF16), attn_w_o.astype(BF16)
    n_attn = cache_k.shape[0]
    ck = cache_k.reshape(n_attn, dec_batch, WINDOW, KV_DIM)
    cv = cache_v.reshape(n_attn, dec_batch, WINDOW, KV_DIM)

    xp = x_prompt.reshape(batch * seq, D_MODEL)
    xs = x_sample.reshape(dec_batch * dec_seq, D_MODEL)
    pool_p, pool_s, kp_l, vp_l, ks_l, vs_l = [], [], [], [], [], []
    for i in range(depth):
        xp, xs, w_in, w_out = _ffn(xp, xs, norm_ffn1[i], w_in, w_out,
                                   next_weights=(ffn2_w_in, ffn2_w_out, i), tm=FFN_TILE)
        j = i // n_mixers
        xp3 = xp.reshape(batch, seq, D_MODEL)
        xs3 = xs.reshape(dec_batch, dec_seq, D_MODEL)
        if i % n_mixers == 0:
            xp3, sp = _pool(xp3, None, norm_mix[i], pw, pool_scale[j], j, start_pos=0, bb=1, tt=512)
            xs3, ss = _pool(xs3, state_pool, norm_mix[i], pw, pool_scale[j], j, start_pos=PAST_LEN,
                            bb=32, tt=dec_seq)
            pool_p.append(sp)
            pool_s.append(ss)
        else:
            xp3, kp, vp = _attn_prompt(xp3, norm_mix[i], wqkv, attn_b_qkv[j], wo, attn_b_o[j], attn_sinks[j], j,
                                       nblk=ATTN_BLOCKS_PER_STEP)
            xs3, kn, vn = _attn_sample(xs3, ck, cv, norm_mix[i], wqkv, attn_b_qkv[j], wo, attn_b_o[j],
                                       attn_sinks[j], j, bb=32)
            kp_l.append(kp.reshape(batch, WINDOW, N_KV_HEADS, HEAD_DIM))
            vp_l.append(vp.reshape(batch, WINDOW, N_KV_HEADS, HEAD_DIM))
            ks_l.append(kn.reshape(dec_batch, WINDOW, N_KV_HEADS, HEAD_DIM))
            vs_l.append(vn.reshape(dec_batch, WINDOW, N_KV_HEADS, HEAD_DIM))
        xp = xp3.reshape(batch * seq, D_MODEL)
        xs = xs3.reshape(dec_batch * dec_seq, D_MODEL)
        if i == depth - 1:
            xp, xs = _ffn(xp, xs, norm_ffn2[i], w_in, w_out, final_gain=norm_final, tm=FFN_TILE)
        else:
            xp, xs, w_in, w_out = _ffn(xp, xs, norm_ffn2[i], w_in, w_out,
                                       next_weights=(ffn1_w_in, ffn1_w_out, i + 1), tm=FFN_TILE)
    return (xp.reshape(batch, seq, D_MODEL), xs.reshape(dec_batch, dec_seq, D_MODEL),
            jnp.stack(pool_p), jnp.stack(pool_s), jnp.stack(kp_l), jnp.stack(vp_l),
            jnp.stack(ks_l), jnp.stack(vs_l))
```

```python
import functools

import jax
import jax.numpy as jnp
from jax import lax
from jax.experimental import pallas as pl
from jax.experimental.pallas import tpu as pltpu

D_MODEL = 1024
D_FF = 2816
POOL_WINDOWS = (2, 4, 8, 16)
POOL_GROUP_DIM = D_MODEL // len(POOL_WINDOWS)
POOL_PREFIX = max(POOL_WINDOWS) - 1
N_HEADS = 16
N_KV_HEADS = 4
HEAD_DIM = 64
GROUP = N_HEADS // N_KV_HEADS
WINDOW = 128
ROPE_THETA = 10000.0
Q_DIM = N_HEADS * HEAD_DIM
KV_DIM = N_KV_HEADS * HEAD_DIM
QKV_DIM = Q_DIM + 2 * KV_DIM
RMS_EPS = 1e-6
NEG_INF = -1e30
PAST_LEN = 8192

LANES = 128
HEADS_PER_VREG = LANES // HEAD_DIM
Q_COLS = Q_DIM // LANES
KV_COLS = KV_DIM // LANES
POOL_HALO = 16
FF_CHUNK = 256
FFN_TILE = 512
FFN_SUB_ROWS = 256
RELAYOUT_SEQS = 4
RESTATE_SEQS = 8
WEIGHT_CAST_STEPS = 16
ATTN_BLOCKS_PER_STEP = 8
ATTN_PROJ_BLOCKS = 2
PROJ_PIECES = QKV_DIM // (2 * LANES)
PROJ_AT_GROUP = 2
QK_LOOKAHEAD = 3
LOG2_E = 1.4426950408889634
QK_SCALE_LOG2 = HEAD_DIM ** -0.5 * LOG2_E
VMEM_LIMIT = 52 * 1024 * 1024

F32 = jnp.float32
BF16 = jnp.bfloat16


def _rms(x, g):
    ms = jnp.mean(x * x, axis=-1, keepdims=True)
    return x * lax.rsqrt(ms + RMS_EPS) * g


def _resident(shape):
    nd = len(shape)
    return pl.BlockSpec(shape, lambda *_: (0,) * nd, pipeline_mode=pl.Buffered(1))


def _resident_layer(shape, layer):
    nd = len(shape)
    return pl.BlockSpec((None,) + shape, lambda *_: (layer,) + (0,) * nd, pipeline_mode=pl.Buffered(1))


def _ffn_kernel(*refs, n_sample_tiles, final_norm, convert_next, relayout, restate):
    refs = list(refs)
    xp_ref, xs_ref, g_ref, win_ref, wout_ref = refs[:5]
    del refs[:5]
    gf_ref = refs.pop(0) if final_norm else None
    if convert_next:
        nwin_ref, nwout_ref = refs.pop(0), refs.pop(0)
    if relayout:
        ri_ref = refs.pop(0)
    if restate:
        si_ref = refs.pop(0)
    op_ref, os_ref = refs.pop(0), refs.pop(0)
    if convert_next:
        cwin_ref, cwout_ref = refs.pop(0), refs.pop(0)
    if relayout:
        ro_ref = refs.pop(0)
    if restate:
        so_ref = refs.pop(0)
    (h_ref,) = refs

    i = pl.program_id(0)
    is_sample = i < n_sample_tiles
    x = jnp.where(is_sample, xs_ref[...], xp_ref[...])
    xn = _rms(x, g_ref[...]).astype(BF16)
    tm = x.shape[0]
    n_chunks = D_FF // FF_CHUNK
    for c in range(n_chunks):
        lo = c * FF_CHUNK
        for r0 in range(0, tm, FFN_SUB_ROWS):
            xs_ = xn[r0:r0 + FFN_SUB_ROWS]
            gate = jnp.dot(xs_, win_ref[:, lo:lo + FF_CHUNK], preferred_element_type=F32)
            up = jnp.dot(xs_, win_ref[:, D_FF + lo:D_FF + lo + FF_CHUNK], preferred_element_type=F32)
            h_ref[r0:r0 + FFN_SUB_ROWS, lo:lo + FF_CHUNK] = (gate * jax.nn.sigmoid(gate) * up).astype(BF16)
        if convert_next:
            cw = 2 * D_FF // n_chunks
            cwin_ref[:, c * cw:(c + 1) * cw] = nwin_ref[:, c * cw:(c + 1) * cw].astype(BF16)
            cr = cwout_ref.shape[0] // n_chunks
            cwout_ref[c * cr:(c + 1) * cr, :] = nwout_ref[c * cr:(c + 1) * cr, :].astype(BF16)
        if relayout and c < ri_ref.shape[0]:
            ro_ref[c] = ri_ref[c].T
        if restate:
            for p in range(c, POOL_PREFIX, n_chunks):
                so_ref[:, p, :] = si_ref[p]
    ys = []
    for r0 in range(0, tm, FFN_SUB_ROWS):
        rows = slice(r0, r0 + FFN_SUB_ROWS)
        yr = x[rows] + 0.5 * jnp.dot(h_ref[rows, :], wout_ref[...], preferred_element_type=F32)
        ys.append(_rms(yr, gf_ref[...]) if final_norm else yr)
    y = jnp.concatenate(ys, axis=0)

    op_ref[...] = y

    @pl.when(is_sample)
    def _():
        os_ref[...] = op_ref[...]


def _ffn(xp, xs, g, w_in, w_out, final_gain=None, next_weights=None, cache_t=None, state_t=None, *, tm):
    n_p, n_s = xp.shape[0] // tm, xs.shape[0] // tm
    final_norm = final_gain is not None
    convert_next = next_weights is not None
    relayout = cache_t is not None
    restate = state_t is not None

    def sample_tile(i):
        return (jnp.minimum(i, n_s - 1), 0)

    def prompt_tile(i):
        return (jnp.maximum(i - n_s, 0), 0)

    in_specs = [
        pl.BlockSpec((tm, D_MODEL), prompt_tile),
        pl.BlockSpec((tm, D_MODEL), sample_tile),
        _resident((1, D_MODEL)),
        _resident((D_MODEL, 2 * D_FF)),
        _resident((D_FF, D_MODEL)),
    ]
    args = [xp, xs, g.reshape(1, D_MODEL), w_in, w_out]
    out_shape = [jax.ShapeDtypeStruct(xp.shape, F32), jax.ShapeDtypeStruct(xs.shape, F32)]
    out_specs = [pl.BlockSpec((tm, D_MODEL), prompt_tile), pl.BlockSpec((tm, D_MODEL), sample_tile)]
    if final_norm:
        in_specs.append(_resident((1, D_MODEL)))
        args.append(final_gain.reshape(1, D_MODEL))
    if convert_next:
        nw_in, nw_out, layer = next_weights
        rows_in, rows_out = D_MODEL // WEIGHT_CAST_STEPS, D_FF // WEIGHT_CAST_STEPS

        def slab(i):
            return (jnp.minimum(i, WEIGHT_CAST_STEPS - 1), 0)

        in_specs += [pl.BlockSpec((None, rows_in, 2 * D_FF), lambda i: (layer,) + slab(i)),
                     pl.BlockSpec((None, rows_out, D_MODEL), lambda i: (layer,) + slab(i))]
        args += [nw_in, nw_out]
        out_shape += [jax.ShapeDtypeStruct((D_MODEL, 2 * D_FF), BF16), jax.ShapeDtypeStruct((D_FF, D_MODEL), BF16)]
        out_specs += [pl.BlockSpec((rows_in, 2 * D_FF), slab), pl.BlockSpec((rows_out, D_MODEL), slab)]
    if relayout:
        cache, cache_layer = cache_t
        n_seq = cache.shape[1]
        last_group = n_seq // RELAYOUT_SEQS - 1

        def seq_group(i):
            return (jnp.minimum(i, last_group), 0, 0)

        in_specs.append(pl.BlockSpec((None, RELAYOUT_SEQS, KV_DIM, WINDOW), lambda i: (cache_layer,) + seq_group(i)))
        args.append(cache)
        out_shape.append(jax.ShapeDtypeStruct((n_seq, WINDOW, KV_DIM), F32))
        out_specs.append(pl.BlockSpec((RELAYOUT_SEQS, WINDOW, KV_DIM), seq_group))
    if restate:
        state, state_layer = state_t
        n_seq = state.shape[2]
        last_state_group = n_seq // RESTATE_SEQS - 1

        def state_group(i):
            return jnp.minimum(i, last_state_group)

        in_specs.append(pl.BlockSpec((None, POOL_PREFIX, RESTATE_SEQS, D_MODEL),
                                     lambda i: (state_layer, 0, state_group(i), 0)))
        args.append(state)
        out_shape.append(jax.ShapeDtypeStruct((n_seq, POOL_PREFIX, D_MODEL), F32))
        out_specs.append(pl.BlockSpec((RESTATE_SEQS, POOL_PREFIX, D_MODEL), lambda i: (state_group(i), 0, 0)))
    return pl.pallas_call(
        functools.partial(_ffn_kernel, n_sample_tiles=n_s, final_norm=final_norm, convert_next=convert_next,
                          relayout=relayout, restate=restate),
        out_shape=tuple(out_shape),
        grid=(n_p + n_s,),
        in_specs=in_specs,
        out_specs=tuple(out_specs),
        scratch_shapes=[pltpu.VMEM((tm, D_FF), BF16)],
        compiler_params=pltpu.CompilerParams(
            dimension_semantics=("arbitrary",), vmem_limit_bytes=VMEM_LIMIT),
        name="ffn_final" if final_norm else "ffn",
    )(*args)


def _pool_kernel(x_ref, *rest, has_prefix, start_pos, tt):
    if has_prefix:
        pre_ref, g_ref, w_ref, sc_ref, o_ref, st_ref, ext_ref = rest
    else:
        g_ref, w_ref, sc_ref, o_ref, st_ref, ext_ref = rest
    ti = pl.program_id(1)
    bb = x_ref.shape[0]
    te = tt + POOL_HALO

    @pl.when(ti == 0)
    def _():
        ext_ref[:, 0:POOL_HALO, :] = jnp.zeros((bb, POOL_HALO, D_MODEL), F32)
        if has_prefix:
            ext_ref[:, POOL_HALO - POOL_PREFIX:POOL_HALO, :] = pre_ref[...]

    if tt >= POOL_HALO:
        @pl.when(ti > 0)
        def _():
            ext_ref[:, 0:POOL_HALO, :] = ext_ref[:, tt:te, :]

    x = x_ref[...]
    u = _rms(x, g_ref[...])
    ext_ref[:, POOL_HALO:te, :] = u

    n_seen = start_pos + ti * tt + lax.broadcasted_iota(jnp.int32, (1, tt, 1), 1) + 1
    for gi, wg in enumerate(POOL_WINDOWS):
        sl = slice(gi * POOL_GROUP_DIM, (gi + 1) * POOL_GROUP_DIM)
        s = ext_ref[:, :, sl].reshape(bb * te, POOL_GROUP_DIM)
        span = 1
        while span < wg:
            s = s + pltpu.roll(s, span, axis=0)
            span *= 2
        s = s.reshape(bb, te, POOL_GROUP_DIM)[:, POOL_HALO:, :]
        cnt = jnp.minimum(n_seen, wg).astype(F32)
        p = s / cnt - u[:, :, sl]
        p2 = p.reshape(bb * tt, POOL_GROUP_DIM).astype(BF16)
        y = jnp.dot(p2, w_ref[gi], preferred_element_type=F32).reshape(bb, tt, POOL_GROUP_DIM)
        o_ref[:, :, sl] = x[:, :, sl] + y * sc_ref[:, :, sl]

    @pl.when(ti == pl.num_programs(1) - 1)
    def _():
        st_ref[...] = ext_ref[:, te - POOL_PREFIX:te, :]


def _pool(x, prefix, g, w, scale, layer, *, start_pos, bb, tt):
    b, t, _ = x.shape
    has_prefix = prefix is not None
    in_specs = [pl.BlockSpec((bb, tt, D_MODEL), lambda i, j: (i, j, 0))]
    args = [x]
    if has_prefix:
        in_specs.append(pl.BlockSpec((bb, POOL_PREFIX, D_MODEL), lambda i, j: (i, 0, 0)))
        args.append(prefix)
    in_specs += [
        _resident((1, 1, D_MODEL)),
        _resident_layer((len(POOL_WINDOWS), POOL_GROUP_DIM, POOL_GROUP_DIM), layer),
        _resident((1, 1, D_MODEL)),
    ]
    args += [g.reshape(1, 1, D_MODEL), w, scale.reshape(1, 1, D_MODEL)]
    return pl.pallas_call(
        functools.partial(_pool_kernel, has_prefix=has_prefix, start_pos=start_pos, tt=tt),
        out_shape=(jax.ShapeDtypeStruct((b, t, D_MODEL), F32),
                   jax.ShapeDtypeStruct((b, POOL_PREFIX, D_MODEL), F32)),
        grid=(b // bb, t // tt),
        in_specs=in_specs,
        out_specs=(pl.BlockSpec((bb, tt, D_MODEL), lambda i, j: (i, j, 0)),
                   pl.BlockSpec((bb, POOL_PREFIX, D_MODEL), lambda i, j: (i, 0, 0))),
        scratch_shapes=[pltpu.VMEM((bb, tt + POOL_HALO, D_MODEL), F32)],
        compiler_params=pltpu.CompilerParams(
            dimension_semantics=("arbitrary", "arbitrary"), vmem_limit_bytes=VMEM_LIMIT),
        name="pool_sample" if has_prefix else "pool_prompt",
    )(*args)


def _rope_tables(pos):
    half = HEAD_DIM // 2
    inv = ROPE_THETA ** (-jnp.arange(half, dtype=F32) / half)
    ang = pos.astype(F32)[:, None] * inv[None, :]
    cos, sin = jnp.cos(ang), jnp.sin(ang)
    cos64 = jnp.concatenate([cos, cos], axis=-1)
    sin64 = jnp.concatenate([-sin, sin], axis=-1)
    return jnp.tile(cos64, (1, HEADS_PER_VREG)), jnp.tile(sin64, (1, HEADS_PER_VREG))


def _rope_col(xc, cos, sin, first_half):
    half = HEAD_DIM // 2
    partner = jnp.where(first_half, pltpu.roll(xc, LANES - half, axis=1), pltpu.roll(xc, half, axis=1))
    return xc * cos + partner * sin


def _head_half_variants(col, lo_half):
    sw = pltpu.roll(col, HEAD_DIM, axis=1)
    zero = jnp.zeros_like(col)
    return {
        (0, 0): jnp.where(lo_half, col, zero),
        (0, 1): jnp.where(lo_half, zero, sw),
        (1, 0): jnp.where(lo_half, sw, zero),
        (1, 1): jnp.where(lo_half, zero, col),
    }


def _softmax_with_sink(s, sink):
    m = jnp.maximum(jnp.max(s, axis=-1, keepdims=True), sink)
    e = jnp.exp2(s - m)
    denom = jnp.sum(e, axis=-1, keepdims=True) + jnp.exp2(sink - m)
    return e.astype(BF16), 1.0 / denom


def _attn_prompt_kernel(x_ref, cos_ref, sin_ref, g_ref, wqkv_ref, bqkv_ref, wo_ref, bo_ref, sink_ref,
                        o_ref, kout_ref, vout_ref, kt_ref, vlo_ref, vhi_ref, *, nblk):
    j = pl.program_id(1)
    w = WINDOW
    r = nblk * w

    @pl.when(j == 0)
    def _():
        kt_ref[...] = jnp.zeros(kt_ref.shape, BF16)
        vlo_ref[:, 0:w, 0:LANES] = jnp.zeros((N_KV_HEADS, w, LANES), BF16)
        vhi_ref[:, 0:w, 0:LANES] = jnp.zeros((N_KV_HEADS, w, LANES), BF16)
        keys = vlo_ref.shape[1]
        ones_lo = (lax.broadcasted_iota(jnp.int32, (N_KV_HEADS, keys, LANES), 2) < HEAD_DIM).astype(BF16)
        vlo_ref[:, :, LANES:] = ones_lo
        vhi_ref[:, :, LANES:] = 1 - ones_lo

    @pl.when(j > 0)
    def _():
        kt_ref[:, :, 0:w] = kt_ref[:, :, r:r + w]
        vlo_ref[:, 0:w, 0:LANES] = vlo_ref[:, r:r + w, 0:LANES]
        vhi_ref[:, 0:w, 0:LANES] = vhi_ref[:, r:r + w, 0:LANES]

    rc = ATTN_PROJ_BLOCKS * w
    n_chunks = nblk // ATTN_PROJ_BLOCKS
    lane = lax.broadcasted_iota(jnp.int32, (rc, LANES), 1)
    first_half = (lane % HEAD_DIM) < (HEAD_DIM // 2)
    lo_half = lane < HEAD_DIM
    lo_half_2w = lax.broadcasted_iota(jnp.int32, (2 * w, LANES), 1) < HEAD_DIM

    u_of, qcols_of = {}, {}

    def project_piece(ch, t):
        rows = slice(ch * rc, (ch + 1) * rc)
        if t == 0:
            u_of[ch] = _rms(x_ref[0, rows, :], g_ref[...]).astype(BF16)
            qcols_of[ch] = []
        cols = slice(t * 2 * LANES, (t + 1) * 2 * LANES)
        tile = jnp.dot(u_of[ch], wqkv_ref[:, cols], preferred_element_type=F32) + bqkv_ref[:, cols]
        halves = [tile[:, :LANES], tile[:, LANES:]]
        if t < Q_COLS // 2:
            cos, sin = cos_ref[rows, :], sin_ref[rows, :]
            for hc in halves:
                qcols_of[ch].append((_rope_col(hc, cos, sin, first_half) * QK_SCALE_LOG2).astype(BF16))
            return
        is_k = t == Q_COLS // 2
        if is_k:
            cos, sin = cos_ref[rows, :], sin_ref[rows, :]
            halves = [_rope_col(hc, cos, sin, first_half) for hc in halves]
        krows = slice(w + ch * rc, w + (ch + 1) * rc)
        for c, col in enumerate(halves):
            if ch == n_chunks - 1:
                out_ref = kout_ref if is_k else vout_ref
                out_ref[0, :, c * LANES:(c + 1) * LANES] = col[rc - w:]
            if is_k:
                col_t = col.T.astype(BF16)
                for hb in range(HEADS_PER_VREG):
                    kvh = c * HEADS_PER_VREG + hb
                    head_t = col_t[hb * HEAD_DIM:(hb + 1) * HEAD_DIM, :]
                    for a in range(HEADS_PER_VREG):
                        kt_ref[kvh * HEADS_PER_VREG + a, a * HEAD_DIM:(a + 1) * HEAD_DIM, krows] = head_t
            else:
                var = _head_half_variants(col, lo_half)
                for hb in range(HEADS_PER_VREG):
                    kvh = c * HEADS_PER_VREG + hb
                    vlo_ref[kvh, krows, 0:LANES] = var[(hb, 0)].astype(BF16)
                    vhi_ref[kvh, krows, 0:LANES] = var[(hb, 1)].astype(BF16)

    for t in range(PROJ_PIECES):
        project_piece(0, t)

    row = lax.broadcasted_iota(jnp.int32, (2 * w, 2 * w), 0) % w
    col = lax.broadcasted_iota(jnp.int32, (2 * w, 2 * w), 1)
    band = (col > row) & (col <= row + w)
    first_key = jnp.where(j > 0, 0, w)
    band_first = band & (col >= first_key)
    top_rows = lax.broadcasted_iota(jnp.int32, (2 * w, 1), 0) < w

    groups = [(ch, kvh, bl, a) for ch in range(n_chunks) for kvh in range(N_KV_HEADS)
              for bl in range(ATTN_PROJ_BLOCKS) for a in range(HEADS_PER_VREG)]

    def scores(ch, kvh, bl, a):
        qcols = qcols_of[ch]
        qrows = slice(bl * w, (bl + 1) * w)
        blk = ch * ATTN_PROJ_BLOCKS + bl
        qop = jnp.concatenate([qcols[2 * kvh][qrows], qcols[2 * kvh + 1][qrows]], axis=0)
        return jnp.dot(qop, kt_ref[kvh * HEADS_PER_VREG + a, :, blk * w:(blk + 2) * w],
                       preferred_element_type=F32)

    groups_per_chunk = len(groups) // n_chunks
    pending = [scores(*g) for g in groups[:QK_LOOKAHEAD]]
    es, sink_terms, o_parts, acc = [], [], [], None
    for gi, (ch, kvh, bl, a) in enumerate(groups):
        gl = gi % groups_per_chunk
        if 0 <= gl - PROJ_AT_GROUP < PROJ_PIECES and ch + 1 < n_chunks:
            project_piece(ch + 1, gl - PROJ_AT_GROUP)
        if gi + QK_LOOKAHEAD < len(groups):
            pending.append(scores(*groups[gi + QK_LOOKAHEAD]))
        blk = ch * ATTN_PROJ_BLOCKS + bl
        s = jnp.where(band_first if blk == 0 else band, pending[gi], NEG_INF)
        pending[gi] = None
        h0 = kvh * GROUP + a
        sink = jnp.where(top_rows, sink_ref[h0] * LOG2_E, sink_ref[h0 + HEADS_PER_VREG] * LOG2_E)
        m = jnp.maximum(jnp.max(s, axis=-1, keepdims=True), sink)
        es.append(jnp.exp2(s - m).astype(BF16))
        sink_terms.append(jnp.exp2(sink - m))
        if a < HEADS_PER_VREG - 1:
            continue
        krows = slice(blk * w, (blk + 2) * w)
        vstack = jnp.concatenate([vlo_ref[kvh, krows, :], vhi_ref[kvh, krows, :]], axis=0)
        res = jnp.dot(jnp.concatenate(es, axis=1), vstack, preferred_element_type=F32)
        den = res[:, LANES:] + jnp.where(lo_half_2w, sink_terms[0], sink_terms[1])
        o = res[:, :LANES] * (1.0 / den)
        o_parts.append(jnp.concatenate([o[:w], o[w:]], axis=1).astype(BF16))
        es, sink_terms = [], []
        if bl < ATTN_PROJ_BLOCKS - 1:
            continue
        part = jnp.dot(jnp.concatenate(o_parts, axis=0),
                       wo_ref[kvh * GROUP * HEAD_DIM:(kvh + 1) * GROUP * HEAD_DIM, :],
                       preferred_element_type=F32)
        acc = part if acc is None else acc + part
        o_parts = []
        if kvh == N_KV_HEADS - 1:
            rows = slice(ch * rc, (ch + 1) * rc)
            o_ref[0, rows, :] = x_ref[0, rows, :] + acc + bo_ref[...]
            acc = None


def _attn_prompt(x, g, w_qkv, b_qkv, w_o, b_o, sinks, layer, *, nblk):
    b, t, _ = x.shape
    r = nblk * WINDOW
    cos, sin = _rope_tables(jnp.arange(t))
    keys = (nblk + 1) * WINDOW
    return pl.pallas_call(
        functools.partial(_attn_prompt_kernel, nblk=nblk),
        out_shape=(jax.ShapeDtypeStruct((b, t, D_MODEL), F32),
                   jax.ShapeDtypeStruct((b, WINDOW, KV_DIM), F32),
                   jax.ShapeDtypeStruct((b, WINDOW, KV_DIM), F32)),
        grid=(b, t // r),
        in_specs=[
            pl.BlockSpec((1, r, D_MODEL), lambda i, j: (i, j, 0)),
            pl.BlockSpec((r, LANES), lambda i, j: (j, 0)),
            pl.BlockSpec((r, LANES), lambda i, j: (j, 0)),
            _resident((1, D_MODEL)),
            _resident_layer((D_MODEL, QKV_DIM), layer),
            _resident((1, QKV_DIM)),
            _resident_layer((Q_DIM, D_MODEL), layer),
            _resident((1, D_MODEL)),
            pl.BlockSpec(memory_space=pltpu.SMEM),
        ],
        out_specs=(pl.BlockSpec((1, r, D_MODEL), lambda i, j: (i, j, 0)),
                   pl.BlockSpec((1, WINDOW, KV_DIM), lambda i, j: (i, 0, 0)),
                   pl.BlockSpec((1, WINDOW, KV_DIM), lambda i, j: (i, 0, 0))),
        scratch_shapes=[pltpu.VMEM((2 * N_KV_HEADS, LANES, keys), BF16),
                        pltpu.VMEM((N_KV_HEADS, keys, 2 * LANES), BF16),
                        pltpu.VMEM((N_KV_HEADS, keys, 2 * LANES), BF16)],
        compiler_params=pltpu.CompilerParams(
            dimension_semantics=("arbitrary", "arbitrary"), vmem_limit_bytes=VMEM_LIMIT),
        name="attn_prompt",
    )(x, cos, sin, g.reshape(1, D_MODEL), w_qkv, b_qkv.reshape(1, QKV_DIM), w_o,
      b_o.reshape(1, D_MODEL), sinks)


SAMPLE_KEYS = 2 * WINDOW
SAMPLE_UNROLL = 4


def _attn_sample_kernel(x_ref, ck_ref, cv_ref, cos_ref, sin_ref, g_ref, wqkv_ref, bqkv_ref, wo_ref, bo_ref,
                        sinkcol_ref, o_ref, kout_ref, vout_ref, q_s, kn_s, vn_s, kk_s, vv_s, oa_s, *, dt):
    bb = x_ref.shape[0]
    w = WINDOW
    x = x_ref[...].reshape(bb * dt, D_MODEL)
    u = _rms(x, g_ref[...]).astype(BF16)
    qkv = jnp.dot(u, wqkv_ref[...], preferred_element_type=F32) + bqkv_ref[...]
    cos, sin = cos_ref[...], sin_ref[...]
    lane_t = lax.broadcasted_iota(jnp.int32, (bb * dt, LANES), 1)
    first_half = (lane_t % HEAD_DIM) < (HEAD_DIM // 2)
    for c in range(Q_COLS):
        qc = _rope_col(qkv[:, c * LANES:(c + 1) * LANES], cos, sin, first_half)
        q_s[:, c * LANES:(c + 1) * LANES] = qc * QK_SCALE_LOG2
    for c in range(KV_COLS):
        kn_s[:, c * LANES:(c + 1) * LANES] = _rope_col(
            qkv[:, Q_DIM + c * LANES:Q_DIM + (c + 1) * LANES], cos, sin, first_half)
    vn_s[...] = qkv[:, Q_DIM + KV_DIM:]

    kk_s[:, w + dt:, :] = jnp.zeros((SAMPLE_UNROLL, SAMPLE_KEYS - w - dt, KV_DIM), BF16)
    vv_s[:, w + dt:, :] = jnp.zeros((SAMPLE_UNROLL, SAMPLE_KEYS - w - dt, KV_DIM), BF16)

    heads_per_col = N_HEADS // KV_COLS
    qcols_per_col = heads_per_col // HEADS_PER_VREG
    rows = heads_per_col * dt
    lane8 = lax.broadcasted_iota(jnp.int32, (dt, LANES), 1)
    lo8 = lane8 < HEAD_DIM
    t_row = lax.broadcasted_iota(jnp.int32, (rows, SAMPLE_KEYS), 0) % dt
    key = lax.broadcasted_iota(jnp.int32, (rows, SAMPLE_KEYS), 1)
    valid = (key > t_row) & (key <= t_row + w)

    def stage(b, slot):
        r0 = pl.multiple_of(b * dt, dt)
        knew = kn_s[pl.ds(r0, dt), :]
        vnew = vn_s[pl.ds(r0, dt), :]
        kout_ref[b, 0:w - dt, :] = ck_ref[b, dt:w, :]
        kout_ref[b, w - dt:w, :] = knew
        vout_ref[b, 0:w - dt, :] = cv_ref[b, dt:w, :]
        vout_ref[b, w - dt:w, :] = vnew
        kk_s[slot, 0:w, :] = ck_ref[b].astype(BF16)
        kk_s[slot, w:w + dt, :] = knew.astype(BF16)
        vv_s[slot, 0:w, :] = cv_ref[b].astype(BF16)
        vv_s[slot, w:w + dt, :] = vnew.astype(BF16)

    def scores(b, slot, c):
        r0 = pl.multiple_of(b * dt, dt)
        pieces = []
        for h8 in range(heads_per_col):
            qi = c * qcols_per_col + h8 // HEADS_PER_VREG
            qcol = q_s[pl.ds(r0, dt), qi * LANES:(qi + 1) * LANES]
            a, hb = h8 % HEADS_PER_VREG, h8 // GROUP
            src = qcol if a == hb else pltpu.roll(qcol, HEAD_DIM, axis=1)
            keep = lo8 if hb == 0 else jnp.logical_not(lo8)
            pieces.append(jnp.where(keep, src, 0.0))
        qop = jnp.concatenate(pieces, axis=0).astype(BF16)
        return lax.dot_general(qop, kk_s[slot, :, c * LANES:(c + 1) * LANES],
                               (((1,), (1,)), ((), ())), preferred_element_type=F32)

    def place(b, c, o):
        r0 = pl.multiple_of(b * dt, dt)
        for mm in range(qcols_per_col):
            halves = []
            for a in range(HEADS_PER_VREG):
                h8 = mm * HEADS_PER_VREG + a
                hb = h8 // GROUP
                piece = o[h8 * dt:(h8 + 1) * dt, :]
                halves.append(piece if a == hb else pltpu.roll(piece, HEAD_DIM, axis=1))
            oi = c * qcols_per_col + mm
            oa_s[pl.ds(r0, dt), oi * LANES:(oi + 1) * LANES] = jnp.where(lo8, halves[0], halves[1])

    def body(gidx, carry):
        work = [(gidx * SAMPLE_UNROLL + slot, slot, c) for slot in range(SAMPLE_UNROLL) for c in range(KV_COLS)]
        for slot in range(SAMPLE_UNROLL):
            stage(gidx * SAMPLE_UNROLL + slot, slot)
        ss = [scores(b, slot, c) for b, slot, c in work]
        sm = [_softmax_with_sink(jnp.where(valid, s, NEG_INF), sinkcol_ref[c * rows:(c + 1) * rows, :] * LOG2_E)
              for s, (b, slot, c) in zip(ss, work)]
        for (e, rinv), (b, slot, c) in zip(sm, work):
            o = jnp.dot(e, vv_s[slot, :, c * LANES:(c + 1) * LANES], preferred_element_type=F32) * rinv
            place(b, c, o)
        return carry

    lax.fori_loop(0, bb // SAMPLE_UNROLL, body, 0)

    y = x + jnp.dot(oa_s[...].astype(BF16), wo_ref[...], preferred_element_type=F32) + bo_ref[...]
    o_ref[...] = y.reshape(bb, dt, D_MODEL)


def _attn_sample(x, cache_k, cache_v, g, w_qkv, b_qkv, w_o, b_o, sinks, layer, *, bb):
    b, dt, _ = x.shape
    cos, sin = _rope_tables(PAST_LEN + jnp.arange(dt))
    cos, sin = jnp.tile(cos, (bb, 1)), jnp.tile(sin, (bb, 1))
    sinkcol = jnp.repeat(sinks, dt).reshape(N_HEADS * dt, 1)
    n = bb * dt
    return pl.pallas_call(
        functools.partial(_attn_sample_kernel, dt=dt),
        out_shape=(jax.ShapeDtypeStruct((b, dt, D_MODEL), F32),
                   jax.ShapeDtypeStruct((b, WINDOW, KV_DIM), F32),
                   jax.ShapeDtypeStruct((b, WINDOW, KV_DIM), F32)),
        grid=(b // bb,),
        in_specs=[
            pl.BlockSpec((bb, dt, D_MODEL), lambda i: (i, 0, 0)),
            pl.BlockSpec((bb, WINDOW, KV_DIM), lambda i: (i, 0, 0)),
            pl.BlockSpec((bb, WINDOW, KV_DIM), lambda i: (i, 0, 0)),
            _resident((n, LANES)),
            _resident((n, LANES)),
            _resident((1, D_MODEL)),
            _resident_layer((D_MODEL, QKV_DIM), layer),
            _resident((1, QKV_DIM)),
            _resident_layer((Q_DIM, D_MODEL), layer),
            _resident((1, D_MODEL)),
            _resident((N_HEADS * dt, 1)),
        ],
        out_specs=(pl.BlockSpec((bb, dt, D_MODEL), lambda i: (i, 0, 0)),
                   pl.BlockSpec((bb, WINDOW, KV_DIM), lambda i: (i, 0, 0)),
                   pl.BlockSpec((bb, WINDOW, KV_DIM), lambda i: (i, 0, 0))),
        scratch_shapes=[pltpu.VMEM((n, Q_DIM), F32),
                        pltpu.VMEM((n, KV_DIM), F32),
                        pltpu.VMEM((n, KV_DIM), F32),
                        pltpu.VMEM((SAMPLE_UNROLL, SAMPLE_KEYS, KV_DIM), BF16),
                        pltpu.VMEM((SAMPLE_UNROLL, SAMPLE_KEYS, KV_DIM), BF16),
                        pltpu.VMEM((n, Q_DIM), F32)],
        compiler_params=pltpu.CompilerParams(
            dimension_semantics=("arbitrary",), vmem_limit_bytes=VMEM_LIMIT),
        name="attn_sample",
    )(x, cache_k, cache_v, cos, sin, g.reshape(1, D_MODEL), w_qkv, b_qkv.reshape(1, QKV_DIM), w_o,
      b_o.reshape(1, D_MODEL), sinkcol)


def kernel(x_prompt, x_sample, state_pool, cache_k, cache_v, norm_ffn1, ffn1_w_in, ffn1_w_out, norm_mix,
           norm_ffn2, ffn2_w_in, ffn2_w_out, pool_w, pool_scale, attn_w_qkv, attn_b_qkv, attn_w_o, attn_b_o,
           attn_sinks, norm_final):
    batch, seq, _ = x_prompt.shape
    dec_batch, dec_seq, _ = x_sample.shape
    depth = norm_ffn1.shape[0]
    n_mixers = 2
    w_in, w_out = ffn1_w_in[0].astype(BF16), ffn1_w_out[0].astype(BF16)
    pw = pool_w.astype(BF16)
    wqkv, wo = attn_w_qkv.astype(BF16), attn_w_o.astype(BF16)
    n_attn = cache_k.shape[0]
    ck_t = jnp.transpose(cache_k, (0, 1, 3, 4, 2)).reshape(n_attn, dec_batch, KV_DIM, WINDOW)
    cv_t = jnp.transpose(cache_v, (0, 1, 3, 4, 2)).reshape(n_attn, dec_batch, KV_DIM, WINDOW)
    ck = cv = None
    state_t = jnp.swapaxes(state_pool, 1, 2)

    xp = x_prompt.reshape(batch * seq, D_MODEL)
    xs = x_sample.reshape(dec_batch * dec_seq, D_MODEL)
    pool_p, pool_s, kp_l, vp_l, ks_l, vs_l = [], [], [], [], [], []
    for i in range(depth):
        j = i // n_mixers
        is_attn = i % n_mixers == 1
        if is_attn:
            xp, xs, w_in, w_out, cv = _ffn(xp, xs, norm_ffn1[i], w_in, w_out,
                                           next_weights=(ffn2_w_in, ffn2_w_out, i), cache_t=(cv_t, j), tm=FFN_TILE)
        else:
            xp, xs, w_in, w_out, state = _ffn(xp, xs, norm_ffn1[i], w_in, w_out,
                                              next_weights=(ffn2_w_in, ffn2_w_out, i), state_t=(state_t, j),
                                              tm=FFN_TILE)
        xp3 = xp.reshape(batch, seq, D_MODEL)
        xs3 = xs.reshape(dec_batch, dec_seq, D_MODEL)
        if not is_attn:
            xp3, sp = _pool(xp3, None, norm_mix[i], pw, pool_scale[j], j, start_pos=0, bb=1, tt=1024)
            xs3, ss = _pool(xs3, state, norm_mix[i], pw, pool_scale[j], j, start_pos=PAST_LEN,
                            bb=32, tt=dec_seq)
            pool_p.append(sp)
            pool_s.append(ss)
        else:
            xp3, kp, vp = _attn_prompt(xp3, norm_mix[i], wqkv, attn_b_qkv[j], wo, attn_b_o[j], attn_sinks[j], j,
                                       nblk=ATTN_BLOCKS_PER_STEP)
            if ck is None:
                ck = jnp.swapaxes(ck_t[j], 1, 2)
            xs3, kn, vn = _attn_sample(xs3, ck, cv, norm_mix[i], wqkv, attn_b_qkv[j], wo, attn_b_o[j],
                                       attn_sinks[j], j, bb=32)
            ck = None
            kp_l.append(kp.reshape(batch, WINDOW, N_KV_HEADS, HEAD_DIM))
            vp_l.append(vp.reshape(batch, WINDOW, N_KV_HEADS, HEAD_DIM))
            ks_l.append(kn.reshape(dec_batch, WINDOW, N_KV_HEADS, HEAD_DIM))
            vs_l.append(vn.reshape(dec_batch, WINDOW, N_KV_HEADS, HEAD_DIM))
        xp = xp3.reshape(batch * seq, D_MODEL)
        xs = xs3.reshape(dec_batch * dec_seq, D_MODEL)
        if i == depth - 1:
            xp, xs = _ffn(xp, xs, norm_ffn2[i], w_in, w_out, final_gain=norm_final, tm=FFN_TILE)
        elif (i + 1) % n_mixers == 1:
            xp, xs, w_in, w_out, ck = _ffn(xp, xs, norm_ffn2[i], w_in, w_out,
                                           next_weights=(ffn1_w_in, ffn1_w_out, i + 1),
                                           cache_t=(ck_t, (i + 1) // n_mixers), tm=FFN_TILE)
        else:
            xp, xs, w_in, w_out = _ffn(xp, xs, norm_ffn2[i], w_in, w_out,
                                       next_weights=(ffn1_w_in, ffn1_w_out, i + 1), tm=FFN_TILE)
    return (xp.reshape(batch, seq, D_MODEL), xs.reshape(dec_batch, dec_seq, D_MODEL),
            jnp.stack(pool_p), jnp.stack(pool_s), jnp.stack(kp_l), jnp.stack(vp_l),
            jnp.stack(ks_l), jnp.stack(vs_l))
```

```python
import functools

import jax
import jax.numpy as jnp
from jax import lax
from jax.experimental import pallas as pl
from jax.experimental.pallas import tpu as pltpu

D_MODEL = 1024
D_FF = 2816
POOL_WINDOWS = (2, 4, 8, 16)
POOL_GROUP_DIM = D_MODEL // len(POOL_WINDOWS)
POOL_PREFIX = max(POOL_WINDOWS) - 1
N_HEADS = 16
N_KV_HEADS = 4
HEAD_DIM = 64
GROUP = N_HEADS // N_KV_HEADS
WINDOW = 128
ROPE_THETA = 10000.0
Q_DIM = N_HEADS * HEAD_DIM
KV_DIM = N_KV_HEADS * HEAD_DIM
QKV_DIM = Q_DIM + 2 * KV_DIM
RMS_EPS = 1e-6
NEG_INF = -1e30
PAST_LEN = 8192

LANES = 128
HEADS_PER_VREG = LANES // HEAD_DIM
Q_COLS = Q_DIM // LANES
KV_COLS = KV_DIM // LANES
POOL_HALO = 16
FF_CHUNK = 256
FFN_TILE = 512
FFN_SUB_ROWS = 256
RELAYOUT_SEQS = 4
RESTATE_SEQS = 8
WEIGHT_CAST_STEPS = 16
ATTN_BLOCKS_PER_STEP = 8
ATTN_PROJ_BLOCKS = 2
OPROJ_CHUNKS = 1
PROJ_PIECES = QKV_DIM // (2 * LANES)
PROJ_AT_GROUP = 2
QK_LOOKAHEAD = 3
LOG2_E = 1.4426950408889634
QK_SCALE_LOG2 = HEAD_DIM ** -0.5 * LOG2_E
VMEM_LIMIT = 52 * 1024 * 1024

F32 = jnp.float32
BF16 = jnp.bfloat16


def _rms(x, g):
    ms = jnp.mean(x * x, axis=-1, keepdims=True)
    return x * lax.rsqrt(ms + RMS_EPS) * g


def _resident(shape):
    nd = len(shape)
    return pl.BlockSpec(shape, lambda *_: (0,) * nd, pipeline_mode=pl.Buffered(1))


def _resident_layer(shape, layer):
    nd = len(shape)
    return pl.BlockSpec((None,) + shape, lambda *_: (layer,) + (0,) * nd, pipeline_mode=pl.Buffered(1))


def _zero_bits_of(v):
    bits = lax.bitcast_convert_type(v, jnp.uint32)
    return (bits >> 16) >> 16


def _after(v, zero_bits):
    reps = (v.shape[0] // zero_bits.shape[0], v.shape[1] // zero_bits.shape[1])
    bits = lax.bitcast_convert_type(v, jnp.uint32) | jnp.tile(zero_bits, reps)
    return lax.bitcast_convert_type(bits, F32)


def _ffn_kernel(*refs, n_sample_tiles, split_in, split_out, final_norm, convert_next, relayout, restate):
    refs = list(refs)
    if split_in:
        xp_ref, xs_ref = refs.pop(0), refs.pop(0)
    else:
        x_ref = refs.pop(0)
    g_ref, win_ref, wout_ref = refs[:3]
    del refs[:3]
    gf_ref = refs.pop(0) if final_norm else None
    if convert_next:
        nwin_ref, nwout_ref = refs.pop(0), refs.pop(0)
    if relayout:
        ri_ref = refs.pop(0)
    if restate:
        si_ref = refs.pop(0)
    if split_out:
        op_ref, os_ref = refs.pop(0), refs.pop(0)
    else:
        op_ref = refs.pop(0)
    if convert_next:
        cwin_ref, cwout_ref = refs.pop(0), refs.pop(0)
    if relayout:
        ro_ref = refs.pop(0)
    if restate:
        so_ref = refs.pop(0)
    (h_ref,) = refs

    is_sample = pl.program_id(0) < n_sample_tiles
    x = jnp.where(is_sample, xs_ref[...], xp_ref[...]) if split_in else x_ref[...]
    xn = _rms(x, g_ref[...]).astype(BF16)
    tm = x.shape[0]
    n_chunks = D_FF // FF_CHUNK
    for c in range(n_chunks):
        lo = c * FF_CHUNK
        for r0 in range(0, tm, FFN_SUB_ROWS):
            xs_ = xn[r0:r0 + FFN_SUB_ROWS]
            gate = jnp.dot(xs_, win_ref[:, lo:lo + FF_CHUNK], preferred_element_type=F32)
            up = jnp.dot(xs_, win_ref[:, D_FF + lo:D_FF + lo + FF_CHUNK], preferred_element_type=F32)
            h_ref[r0:r0 + FFN_SUB_ROWS, lo:lo + FF_CHUNK] = (gate * jax.nn.sigmoid(gate) * up).astype(BF16)
        anchor = _zero_bits_of(gate[0:8, 0:LANES])
        if convert_next:
            cw = 2 * D_FF // n_chunks
            cwin_ref[:, c * cw:(c + 1) * cw] = _after(nwin_ref[:, c * cw:(c + 1) * cw], anchor).astype(BF16)
            cr = cwout_ref.shape[0] // n_chunks
            cwout_ref[c * cr:(c + 1) * cr, :] = _after(nwout_ref[c * cr:(c + 1) * cr, :], anchor).astype(BF16)
        if relayout and c < ri_ref.shape[0]:
            ro_ref[c] = _after(ri_ref[c], anchor).T
        if restate:
            for p in range(c, POOL_PREFIX, n_chunks):
                so_ref[:, p, :] = _after(si_ref[p], anchor)
    ys = []
    for r0 in range(0, tm, FFN_SUB_ROWS):
        rows = slice(r0, r0 + FFN_SUB_ROWS)
        yr = x[rows] + 0.5 * jnp.dot(h_ref[rows, :], wout_ref[...], preferred_element_type=F32)
        ys.append(_rms(yr, gf_ref[...]) if final_norm else yr)
    y = jnp.concatenate(ys, axis=0)

    op_ref[...] = y
    if split_out:
        @pl.when(is_sample)
        def _():
            os_ref[...] = op_ref[...]


def _ffn(x, g, w_in, w_out, n_prompt_rows, final_gain=None, next_weights=None, cache_t=None, state_t=None, *,
         split_out=False, tm):
    split_in = isinstance(x, tuple)
    n_rows = sum(a.shape[0] for a in x) if split_in else x.shape[0]
    n_p, n_s = n_prompt_rows // tm, (n_rows - n_prompt_rows) // tm
    final_norm = final_gain is not None
    convert_next = next_weights is not None
    relayout = cache_t is not None
    restate = state_t is not None

    def sample_tile(i):
        return (jnp.minimum(i, n_s - 1), 0)

    def prompt_tile(i):
        return (jnp.maximum(i - n_s, 0), 0)

    def sample_first_tile(i):
        return (jnp.where(i < n_s, n_p + i, i - n_s), 0)

    def same_tile(i):
        return (i, 0)

    reordered = split_in or split_out
    combined_tile = sample_first_tile if reordered else same_tile
    if split_in:
        in_specs = [pl.BlockSpec((tm, D_MODEL), prompt_tile), pl.BlockSpec((tm, D_MODEL), sample_tile)]
        args = list(x)
    else:
        in_specs = [pl.BlockSpec((tm, D_MODEL), combined_tile)]
        args = [x]
    in_specs += [_resident((1, D_MODEL)), _resident((D_MODEL, 2 * D_FF)), _resident((D_FF, D_MODEL))]
    args += [g.reshape(1, D_MODEL), w_in, w_out]
    if split_out:
        out_shape = [jax.ShapeDtypeStruct((n_p * tm, D_MODEL), F32), jax.ShapeDtypeStruct((n_s * tm, D_MODEL), F32)]
        out_specs = [pl.BlockSpec((tm, D_MODEL), prompt_tile), pl.BlockSpec((tm, D_MODEL), sample_tile)]
    else:
        out_shape = [jax.ShapeDtypeStruct((n_rows, D_MODEL), F32)]
        out_specs = [pl.BlockSpec((tm, D_MODEL), combined_tile)]
    if final_norm:
        in_specs.append(_resident((1, D_MODEL)))
        args.append(final_gain.reshape(1, D_MODEL))
    if convert_next:
        nw_in, nw_out, layer = next_weights
        rows_in, rows_out = D_MODEL // WEIGHT_CAST_STEPS, D_FF // WEIGHT_CAST_STEPS

        def slab(i):
            return (jnp.minimum(i, WEIGHT_CAST_STEPS - 1), 0)

        in_specs += [pl.BlockSpec((None, rows_in, 2 * D_FF), lambda i: (layer,) + slab(i)),
                     pl.BlockSpec((None, rows_out, D_MODEL), lambda i: (layer,) + slab(i))]
        args += [nw_in, nw_out]
        out_shape += [jax.ShapeDtypeStruct((D_MODEL, 2 * D_FF), BF16), jax.ShapeDtypeStruct((D_FF, D_MODEL), BF16)]
        out_specs += [pl.BlockSpec((rows_in, 2 * D_FF), slab), pl.BlockSpec((rows_out, D_MODEL), slab)]
    if relayout:
        cache, cache_layer = cache_t
        n_seq = cache.shape[1]
        last_group = n_seq // RELAYOUT_SEQS - 1

        def seq_group(i):
            return (jnp.minimum(i, last_group), 0, 0)

        in_specs.append(pl.BlockSpec((None, RELAYOUT_SEQS, KV_DIM, WINDOW), lambda i: (cache_layer,) + seq_group(i)))
        args.append(cache)
        out_shape.append(jax.ShapeDtypeStruct((n_seq, WINDOW, KV_DIM), F32))
        out_specs.append(pl.BlockSpec((RELAYOUT_SEQS, WINDOW, KV_DIM), seq_group))
    if restate:
        state, state_layer = state_t
        n_seq = state.shape[2]
        last_state_group = n_seq // RESTATE_SEQS - 1

        def state_group(i):
            return jnp.minimum(i, last_state_group)

        in_specs.append(pl.BlockSpec((None, POOL_PREFIX, RESTATE_SEQS, D_MODEL),
                                     lambda i: (state_layer, 0, state_group(i), 0)))
        args.append(state)
        out_shape.append(jax.ShapeDtypeStruct((n_seq, POOL_PREFIX, D_MODEL), F32))
        out_specs.append(pl.BlockSpec((RESTATE_SEQS, POOL_PREFIX, D_MODEL), lambda i: (state_group(i), 0, 0)))
    return pl.pallas_call(
        functools.partial(_ffn_kernel, n_sample_tiles=n_s, split_in=split_in, split_out=split_out,
                          final_norm=final_norm, convert_next=convert_next, relayout=relayout, restate=restate),
        out_shape=tuple(out_shape),
        grid=(n_p + n_s,),
        in_specs=in_specs,
        out_specs=tuple(out_specs),
        scratch_shapes=[pltpu.VMEM((tm, D_FF), BF16)],
        compiler_params=pltpu.CompilerParams(
            dimension_semantics=("arbitrary",), vmem_limit_bytes=VMEM_LIMIT),
        name="ffn_final" if final_norm else "ffn",
    )(*args)


def _pool_kernel(x_ref, *rest, has_prefix, aliased, start_pos, bb, tt):
    rest = list(rest)
    pre_ref = rest.pop(0) if has_prefix else None
    g_ref, w_ref, sc_ref = rest[:3]
    del rest[:3]
    if aliased:
        rest.pop(0)
    o_ref, st_ref, ext_ref = rest
    ti = pl.program_id(1)
    te = tt + POOL_HALO

    @pl.when(ti == 0)
    def _():
        ext_ref[:, 0:POOL_HALO, :] = jnp.zeros((bb, POOL_HALO, D_MODEL), F32)
        if has_prefix:
            ext_ref[:, POOL_HALO - POOL_PREFIX:POOL_HALO, :] = pre_ref[...]

    if tt >= POOL_HALO:
        @pl.when(ti > 0)
        def _():
            ext_ref[:, 0:POOL_HALO, :] = ext_ref[:, tt:te, :]

    x = x_ref[...].reshape(bb, tt, D_MODEL)
    u = _rms(x, g_ref[...])
    ext_ref[:, POOL_HALO:te, :] = u

    n_seen = start_pos + ti * tt + lax.broadcasted_iota(jnp.int32, (1, tt, 1), 1) + 1
    for gi, wg in enumerate(POOL_WINDOWS):
        sl = slice(gi * POOL_GROUP_DIM, (gi + 1) * POOL_GROUP_DIM)
        s = ext_ref[:, :, sl].reshape(bb * te, POOL_GROUP_DIM)
        span = 1
        while span < wg:
            s = s + pltpu.roll(s, span, axis=0)
            span *= 2
        s = s.reshape(bb, te, POOL_GROUP_DIM)[:, POOL_HALO:, :]
        cnt = jnp.minimum(n_seen, wg).astype(F32)
        p = s / cnt - u[:, :, sl]
        p2 = p.reshape(bb * tt, POOL_GROUP_DIM).astype(BF16)
        y = jnp.dot(p2, w_ref[gi], preferred_element_type=F32).reshape(bb, tt, POOL_GROUP_DIM)
        o_ref[:, sl] = (x[:, :, sl] + y * sc_ref[:, :, sl]).reshape(bb * tt, POOL_GROUP_DIM)

    @pl.when(ti == pl.num_programs(1) - 1)
    def _():
        st_ref[...] = ext_ref[:, te - POOL_PREFIX:te, :]


def _row_blocks(row0, n_seq, t, bb, tt):
    first, per_seq = row0 // (bb * tt), t // tt
    return lambda i, j: (first + i * per_seq + j, 0)


def _pool(x, prefix, g, w, scale, layer, *, row0, n_seq, t, start_pos, bb, tt, into=None):
    has_prefix = prefix is not None
    aliased = into is not None
    rows = _row_blocks(row0, n_seq, t, bb, tt)
    in_specs = [pl.BlockSpec((bb * tt, D_MODEL), rows)]
    args = [x]
    if has_prefix:
        in_specs.append(pl.BlockSpec((bb, POOL_PREFIX, D_MODEL), lambda i, j: (i, 0, 0)))
        args.append(prefix)
    in_specs += [
        _resident((1, 1, D_MODEL)),
        _resident_layer((len(POOL_WINDOWS), POOL_GROUP_DIM, POOL_GROUP_DIM), layer),
        _resident((1, 1, D_MODEL)),
    ]
    args += [g.reshape(1, 1, D_MODEL), w, scale.reshape(1, 1, D_MODEL)]
    if aliased:
        in_specs.append(pl.BlockSpec(memory_space=pl.ANY))
        args.append(into)
    return pl.pallas_call(
        functools.partial(_pool_kernel, has_prefix=has_prefix, aliased=aliased, start_pos=start_pos, bb=bb, tt=tt),
        out_shape=(jax.ShapeDtypeStruct(x.shape, F32),
                   jax.ShapeDtypeStruct((n_seq, POOL_PREFIX, D_MODEL), F32)),
        grid=(n_seq // bb, t // tt),
        in_specs=in_specs,
        out_specs=(pl.BlockSpec((bb * tt, D_MODEL), rows),
                   pl.BlockSpec((bb, POOL_PREFIX, D_MODEL), lambda i, j: (i, 0, 0))),
        scratch_shapes=[pltpu.VMEM((bb, tt + POOL_HALO, D_MODEL), F32)],
        input_output_aliases={len(args) - 1: 0} if aliased else {},
        compiler_params=pltpu.CompilerParams(
            dimension_semantics=("arbitrary", "arbitrary"), vmem_limit_bytes=VMEM_LIMIT),
        name="pool_sample" if has_prefix else "pool_prompt",
    )(*args)


def _rope_tables(pos):
    half = HEAD_DIM // 2
    inv = ROPE_THETA ** (-jnp.arange(half, dtype=F32) / half)
    ang = pos.astype(F32)[:, None] * inv[None, :]
    cos, sin = jnp.cos(ang), jnp.sin(ang)
    cos64 = jnp.concatenate([cos, cos], axis=-1)
    sin64 = jnp.concatenate([-sin, sin], axis=-1)
    return jnp.tile(cos64, (1, HEADS_PER_VREG)), jnp.tile(sin64, (1, HEADS_PER_VREG))


def _rope_col(xc, cos, sin, first_half):
    half = HEAD_DIM // 2
    partner = jnp.where(first_half, pltpu.roll(xc, LANES - half, axis=1), pltpu.roll(xc, half, axis=1))
    return xc * cos + partner * sin


def _head_half_variants(col, lo_half):
    sw = pltpu.roll(col, HEAD_DIM, axis=1)
    zero = jnp.zeros_like(col)
    return {
        (0, 0): jnp.where(lo_half, col, zero),
        (0, 1): jnp.where(lo_half, zero, sw),
        (1, 0): jnp.where(lo_half, sw, zero),
        (1, 1): jnp.where(lo_half, zero, col),
    }


def _softmax_with_sink(s, sink):
    m = jnp.maximum(jnp.max(s, axis=-1, keepdims=True), sink)
    e = jnp.exp2(s - m)
    denom = jnp.sum(e, axis=-1, keepdims=True) + jnp.exp2(sink - m)
    return e.astype(BF16), 1.0 / denom


def _attn_prompt_kernel(x_ref, cos_ref, sin_ref, g_ref, wqkv_ref, bqkv_ref, wo_ref, bo_ref, sink_ref,
                        o_ref, kout_ref, vout_ref, kt_ref, vlo_ref, vhi_ref, *, nblk):
    j = pl.program_id(1)
    w = WINDOW
    r = nblk * w

    @pl.when(j == 0)
    def _():
        kt_ref[...] = jnp.zeros(kt_ref.shape, BF16)
        vlo_ref[:, 0:w, 0:LANES] = jnp.zeros((N_KV_HEADS, w, LANES), BF16)
        vhi_ref[:, 0:w, 0:LANES] = jnp.zeros((N_KV_HEADS, w, LANES), BF16)
        keys = vlo_ref.shape[1]
        ones_lo = (lax.broadcasted_iota(jnp.int32, (N_KV_HEADS, keys, LANES), 2) < HEAD_DIM).astype(BF16)
        vlo_ref[:, :, LANES:] = ones_lo
        vhi_ref[:, :, LANES:] = 1 - ones_lo

    @pl.when(j > 0)
    def _():
        kt_ref[:, :, 0:w] = kt_ref[:, :, r:r + w]
        vlo_ref[:, 0:w, 0:LANES] = vlo_ref[:, r:r + w, 0:LANES]
        vhi_ref[:, 0:w, 0:LANES] = vhi_ref[:, r:r + w, 0:LANES]

    rc = ATTN_PROJ_BLOCKS * w
    n_chunks = nblk // ATTN_PROJ_BLOCKS
    lane = lax.broadcasted_iota(jnp.int32, (rc, LANES), 1)
    first_half = (lane % HEAD_DIM) < (HEAD_DIM // 2)
    lo_half = lane < HEAD_DIM
    lo_half_2w = lax.broadcasted_iota(jnp.int32, (2 * w, LANES), 1) < HEAD_DIM

    u_of, qcols_of = {}, {}

    def project_piece(ch, t):
        rows = slice(ch * rc, (ch + 1) * rc)
        if t == 0:
            u_of[ch] = _rms(x_ref[rows, :], g_ref[...]).astype(BF16)
            qcols_of[ch] = []
        cols = slice(t * 2 * LANES, (t + 1) * 2 * LANES)
        tile = jnp.dot(u_of[ch], wqkv_ref[:, cols], preferred_element_type=F32) + bqkv_ref[:, cols]
        halves = [tile[:, :LANES], tile[:, LANES:]]
        if t < Q_COLS // 2:
            cos, sin = cos_ref[rows, :], sin_ref[rows, :]
            for hc in halves:
                qcols_of[ch].append((_rope_col(hc, cos, sin, first_half) * QK_SCALE_LOG2).astype(BF16))
            return
        is_k = t == Q_COLS // 2
        if is_k:
            cos, sin = cos_ref[rows, :], sin_ref[rows, :]
            halves = [_rope_col(hc, cos, sin, first_half) for hc in halves]
        krows = slice(w + ch * rc, w + (ch + 1) * rc)
        for c, col in enumerate(halves):
            if ch == n_chunks - 1:
                out_ref = kout_ref if is_k else vout_ref
                out_ref[0, :, c * LANES:(c + 1) * LANES] = col[rc - w:]
            if is_k:
                col_t = col.T.astype(BF16)
                for hb in range(HEADS_PER_VREG):
                    kvh = c * HEADS_PER_VREG + hb
                    head_t = col_t[hb * HEAD_DIM:(hb + 1) * HEAD_DIM, :]
                    for a in range(HEADS_PER_VREG):
                        kt_ref[kvh * HEADS_PER_VREG + a, a * HEAD_DIM:(a + 1) * HEAD_DIM, krows] = head_t
            else:
                var = _head_half_variants(col, lo_half)
                for hb in range(HEADS_PER_VREG):
                    kvh = c * HEADS_PER_VREG + hb
                    vlo_ref[kvh, krows, 0:LANES] = var[(hb, 0)].astype(BF16)
                    vhi_ref[kvh, krows, 0:LANES] = var[(hb, 1)].astype(BF16)

    for t in range(PROJ_PIECES):
        project_piece(0, t)

    row = lax.broadcasted_iota(jnp.int32, (2 * w, 2 * w), 0) % w
    col = lax.broadcasted_iota(jnp.int32, (2 * w, 2 * w), 1)
    band = (col > row) & (col <= row + w)
    first_key = jnp.where(j > 0, 0, w)
    band_first = band & (col >= first_key)
    top_rows = lax.broadcasted_iota(jnp.int32, (2 * w, 1), 0) < w

    groups = [(ch, kvh, bl, a) for ch in range(n_chunks) for kvh in range(N_KV_HEADS)
              for bl in range(ATTN_PROJ_BLOCKS) for a in range(HEADS_PER_VREG)]

    def scores(ch, kvh, bl, a):
        qcols = qcols_of[ch]
        qrows = slice(bl * w, (bl + 1) * w)
        blk = ch * ATTN_PROJ_BLOCKS + bl
        qop = jnp.concatenate([qcols[2 * kvh][qrows], qcols[2 * kvh + 1][qrows]], axis=0)
        return jnp.dot(qop, kt_ref[kvh * HEADS_PER_VREG + a, :, blk * w:(blk + 2) * w],
                       preferred_element_type=F32)

    groups_per_chunk = len(groups) // n_chunks
    pending = [scores(*g) for g in groups[:QK_LOOKAHEAD]]
    es, sink_terms, o_parts, held, acc = [], [], [], {}, None
    for gi, (ch, kvh, bl, a) in enumerate(groups):
        gl = gi % groups_per_chunk
        if 0 <= gl - PROJ_AT_GROUP < PROJ_PIECES and ch + 1 < n_chunks:
            project_piece(ch + 1, gl - PROJ_AT_GROUP)
        if gi + QK_LOOKAHEAD < len(groups):
            pending.append(scores(*groups[gi + QK_LOOKAHEAD]))
        blk = ch * ATTN_PROJ_BLOCKS + bl
        s = jnp.where(band_first if blk == 0 else band, pending[gi], NEG_INF)
        pending[gi] = None
        h0 = kvh * GROUP + a
        sink = jnp.where(top_rows, sink_ref[h0] * LOG2_E, sink_ref[h0 + HEADS_PER_VREG] * LOG2_E)
        m = jnp.maximum(jnp.max(s, axis=-1, keepdims=True), sink)
        es.append(jnp.exp2(s - m).astype(BF16))
        sink_terms.append(jnp.exp2(sink - m))
        if a < HEADS_PER_VREG - 1:
            continue
        krows = slice(blk * w, (blk + 2) * w)
        vstack = jnp.concatenate([vlo_ref[kvh, krows, :], vhi_ref[kvh, krows, :]], axis=0)
        res = jnp.dot(jnp.concatenate(es, axis=1), vstack, preferred_element_type=F32)
        den = res[:, LANES:] + jnp.where(lo_half_2w, sink_terms[0], sink_terms[1])
        o = res[:, :LANES] * (1.0 / den)
        o_parts.append(jnp.concatenate([o[:w], o[w:]], axis=1).astype(BF16))
        es, sink_terms = [], []
        if bl < ATTN_PROJ_BLOCKS - 1:
            continue
        held[(ch % OPROJ_CHUNKS, kvh)] = jnp.concatenate(o_parts, axis=0)
        o_parts = []
        if ch % OPROJ_CHUNKS < OPROJ_CHUNKS - 1:
            continue
        part = jnp.dot(jnp.concatenate([held.pop((cc, kvh)) for cc in range(OPROJ_CHUNKS)], axis=0),
                       wo_ref[kvh * GROUP * HEAD_DIM:(kvh + 1) * GROUP * HEAD_DIM, :],
                       preferred_element_type=F32)
        acc = part if acc is None else acc + part
        if kvh == N_KV_HEADS - 1:
            rows = slice((ch + 1 - OPROJ_CHUNKS) * rc, (ch + 1) * rc)
            o_ref[rows, :] = x_ref[rows, :] + acc + bo_ref[...]
            acc = None


def _attn_prompt(x, g, w_qkv, b_qkv, w_o, b_o, sinks, layer, *, n_seq, t, nblk):
    b = n_seq
    r = nblk * WINDOW
    cos, sin = _rope_tables(jnp.arange(t))
    keys = (nblk + 1) * WINDOW
    rows = _row_blocks(0, n_seq, t, 1, r)
    return pl.pallas_call(
        functools.partial(_attn_prompt_kernel, nblk=nblk),
        out_shape=(jax.ShapeDtypeStruct(x.shape, F32),
                   jax.ShapeDtypeStruct((b, WINDOW, KV_DIM), F32),
                   jax.ShapeDtypeStruct((b, WINDOW, KV_DIM), F32)),
        grid=(b, t // r),
        in_specs=[
            pl.BlockSpec((r, D_MODEL), rows),
            pl.BlockSpec((r, LANES), lambda i, j: (j, 0)),
            pl.BlockSpec((r, LANES), lambda i, j: (j, 0)),
            _resident((1, D_MODEL)),
            _resident_layer((D_MODEL, QKV_DIM), layer),
            _resident((1, QKV_DIM)),
            _resident_layer((Q_DIM, D_MODEL), layer),
            _resident((1, D_MODEL)),
            pl.BlockSpec(memory_space=pltpu.SMEM),
        ],
        out_specs=(pl.BlockSpec((r, D_MODEL), rows),
                   pl.BlockSpec((1, WINDOW, KV_DIM), lambda i, j: (i, 0, 0)),
                   pl.BlockSpec((1, WINDOW, KV_DIM), lambda i, j: (i, 0, 0))),
        scratch_shapes=[pltpu.VMEM((2 * N_KV_HEADS, LANES, keys), BF16),
                        pltpu.VMEM((N_KV_HEADS, keys, 2 * LANES), BF16),
                        pltpu.VMEM((N_KV_HEADS, keys, 2 * LANES), BF16)],
        compiler_params=pltpu.CompilerParams(
            dimension_semantics=("arbitrary", "arbitrary"), vmem_limit_bytes=VMEM_LIMIT),
        name="attn_prompt",
    )(x, cos, sin, g.reshape(1, D_MODEL), w_qkv, b_qkv.reshape(1, QKV_DIM), w_o,
      b_o.reshape(1, D_MODEL), sinks)


SAMPLE_KEYS = 2 * WINDOW
SAMPLE_UNROLL = 4


def _attn_sample_kernel(x_ref, ck_ref, cv_ref, cos_ref, sin_ref, g_ref, wqkv_ref, bqkv_ref, wo_ref, bo_ref,
                        sinkcol_ref, into_ref, o_ref, kout_ref, vout_ref, q_s, kn_s, vn_s, kk_s, vv_s, oa_s, *, dt):
    del into_ref
    bb = ck_ref.shape[0]
    w = WINDOW
    x = x_ref[...]
    u = _rms(x, g_ref[...]).astype(BF16)
    qkv = jnp.dot(u, wqkv_ref[...], preferred_element_type=F32) + bqkv_ref[...]
    cos, sin = cos_ref[...], sin_ref[...]
    lane_t = lax.broadcasted_iota(jnp.int32, (bb * dt, LANES), 1)
    first_half = (lane_t % HEAD_DIM) < (HEAD_DIM // 2)
    for c in range(Q_COLS):
        qc = _rope_col(qkv[:, c * LANES:(c + 1) * LANES], cos, sin, first_half)
        q_s[:, c * LANES:(c + 1) * LANES] = qc * QK_SCALE_LOG2
    for c in range(KV_COLS):
        kn_s[:, c * LANES:(c + 1) * LANES] = _rope_col(
            qkv[:, Q_DIM + c * LANES:Q_DIM + (c + 1) * LANES], cos, sin, first_half)
    vn_s[...] = qkv[:, Q_DIM + KV_DIM:]

    kk_s[:, w + dt:, :] = jnp.zeros((SAMPLE_UNROLL, SAMPLE_KEYS - w - dt, KV_DIM), BF16)
    vv_s[:, w + dt:, :] = jnp.zeros((SAMPLE_UNROLL, SAMPLE_KEYS - w - dt, KV_DIM), BF16)

    heads_per_col = N_HEADS // KV_COLS
    qcols_per_col = heads_per_col // HEADS_PER_VREG
    rows = heads_per_col * dt
    lane8 = lax.broadcasted_iota(jnp.int32, (dt, LANES), 1)
    lo8 = lane8 < HEAD_DIM
    t_row = lax.broadcasted_iota(jnp.int32, (rows, SAMPLE_KEYS), 0) % dt
    key = lax.broadcasted_iota(jnp.int32, (rows, SAMPLE_KEYS), 1)
    valid = (key > t_row) & (key <= t_row + w)

    def stage(b, slot):
        r0 = pl.multiple_of(b * dt, dt)
        knew = kn_s[pl.ds(r0, dt), :]
        vnew = vn_s[pl.ds(r0, dt), :]
        kout_ref[b, 0:w - dt, :] = ck_ref[b, dt:w, :]
        kout_ref[b, w - dt:w, :] = knew
        vout_ref[b, 0:w - dt, :] = cv_ref[b, dt:w, :]
        vout_ref[b, w - dt:w, :] = vnew
        kk_s[slot, 0:w, :] = ck_ref[b].astype(BF16)
        kk_s[slot, w:w + dt, :] = knew.astype(BF16)
        vv_s[slot, 0:w, :] = cv_ref[b].astype(BF16)
        vv_s[slot, w:w + dt, :] = vnew.astype(BF16)

    def scores(b, slot, c):
        r0 = pl.multiple_of(b * dt, dt)
        pieces = []
        for h8 in range(heads_per_col):
            qi = c * qcols_per_col + h8 // HEADS_PER_VREG
            qcol = q_s[pl.ds(r0, dt), qi * LANES:(qi + 1) * LANES]
            a, hb = h8 % HEADS_PER_VREG, h8 // GROUP
            src = qcol if a == hb else pltpu.roll(qcol, HEAD_DIM, axis=1)
            keep = lo8 if hb == 0 else jnp.logical_not(lo8)
            pieces.append(jnp.where(keep, src, 0.0))
        qop = jnp.concatenate(pieces, axis=0).astype(BF16)
        return lax.dot_general(qop, kk_s[slot, :, c * LANES:(c + 1) * LANES],
                               (((1,), (1,)), ((), ())), preferred_element_type=F32)

    def place(b, c, o):
        r0 = pl.multiple_of(b * dt, dt)
        for mm in range(qcols_per_col):
            halves = []
            for a in range(HEADS_PER_VREG):
                h8 = mm * HEADS_PER_VREG + a
                hb = h8 // GROUP
                piece = o[h8 * dt:(h8 + 1) * dt, :]
                halves.append(piece if a == hb else pltpu.roll(piece, HEAD_DIM, axis=1))
            oi = c * qcols_per_col + mm
            oa_s[pl.ds(r0, dt), oi * LANES:(oi + 1) * LANES] = jnp.where(lo8, halves[0], halves[1])

    def body(gidx, carry):
        work = [(gidx * SAMPLE_UNROLL + slot, slot, c) for slot in range(SAMPLE_UNROLL) for c in range(KV_COLS)]
        for slot in range(SAMPLE_UNROLL):
            stage(gidx * SAMPLE_UNROLL + slot, slot)
        ss = [scores(b, slot, c) for b, slot, c in work]
        sm = [_softmax_with_sink(jnp.where(valid, s, NEG_INF), sinkcol_ref[c * rows:(c + 1) * rows, :] * LOG2_E)
              for s, (b, slot, c) in zip(ss, work)]
        for (e, rinv), (b, slot, c) in zip(sm, work):
            o = jnp.dot(e, vv_s[slot, :, c * LANES:(c + 1) * LANES], preferred_element_type=F32) * rinv
            place(b, c, o)
        return carry

    lax.fori_loop(0, bb // SAMPLE_UNROLL, body, 0)

    o_ref[...] = x + jnp.dot(oa_s[...].astype(BF16), wo_ref[...], preferred_element_type=F32) + bo_ref[...]


def _attn_sample(x, cache_k, cache_v, g, w_qkv, b_qkv, w_o, b_o, sinks, layer, into, *, row0, dt, bb):
    b = cache_k.shape[0]
    rows = _row_blocks(row0, b, dt, bb, dt)
    rows1 = lambda i: rows(i, 0)
    cos, sin = _rope_tables(PAST_LEN + jnp.arange(dt))
    cos, sin = jnp.tile(cos, (bb, 1)), jnp.tile(sin, (bb, 1))
    sinkcol = jnp.repeat(sinks, dt).reshape(N_HEADS * dt, 1)
    n = bb * dt
    return pl.pallas_call(
        functools.partial(_attn_sample_kernel, dt=dt),
        out_shape=(jax.ShapeDtypeStruct(x.shape, F32),
                   jax.ShapeDtypeStruct((b, WINDOW, KV_DIM), F32),
                   jax.ShapeDtypeStruct((b, WINDOW, KV_DIM), F32)),
        grid=(b // bb,),
        in_specs=[
            pl.BlockSpec((n, D_MODEL), rows1),
            pl.BlockSpec((bb, WINDOW, KV_DIM), lambda i: (i, 0, 0)),
            pl.BlockSpec((bb, WINDOW, KV_DIM), lambda i: (i, 0, 0)),
            _resident((n, LANES)),
            _resident((n, LANES)),
            _resident((1, D_MODEL)),
            _resident_layer((D_MODEL, QKV_DIM), layer),
            _resident((1, QKV_DIM)),
            _resident_layer((Q_DIM, D_MODEL), layer),
            _resident((1, D_MODEL)),
            _resident((N_HEADS * dt, 1)),
            pl.BlockSpec(memory_space=pl.ANY),
        ],
        out_specs=(pl.BlockSpec((n, D_MODEL), rows1),
                   pl.BlockSpec((bb, WINDOW, KV_DIM), lambda i: (i, 0, 0)),
                   pl.BlockSpec((bb, WINDOW, KV_DIM), lambda i: (i, 0, 0))),
        scratch_shapes=[pltpu.VMEM((n, Q_DIM), F32),
                        pltpu.VMEM((n, KV_DIM), F32),
                        pltpu.VMEM((n, KV_DIM), F32),
                        pltpu.VMEM((SAMPLE_UNROLL, SAMPLE_KEYS, KV_DIM), BF16),
                        pltpu.VMEM((SAMPLE_UNROLL, SAMPLE_KEYS, KV_DIM), BF16),
                        pltpu.VMEM((n, Q_DIM), F32)],
        input_output_aliases={11: 0},
        compiler_params=pltpu.CompilerParams(
            dimension_semantics=("arbitrary",), vmem_limit_bytes=VMEM_LIMIT),
        name="attn_sample",
    )(x, cache_k, cache_v, cos, sin, g.reshape(1, D_MODEL), w_qkv, b_qkv.reshape(1, QKV_DIM), w_o,
      b_o.reshape(1, D_MODEL), sinkcol, into)


def kernel(x_prompt, x_sample, state_pool, cache_k, cache_v, norm_ffn1, ffn1_w_in, ffn1_w_out, norm_mix,
           norm_ffn2, ffn2_w_in, ffn2_w_out, pool_w, pool_scale, attn_w_qkv, attn_b_qkv, attn_w_o, attn_b_o,
           attn_sinks, norm_final):
    batch, seq, _ = x_prompt.shape
    dec_batch, dec_seq, _ = x_sample.shape
    depth = norm_ffn1.shape[0]
    n_mixers = 2
    w_in, w_out = ffn1_w_in[0].astype(BF16), ffn1_w_out[0].astype(BF16)
    pw = pool_w.astype(BF16)
    wqkv, wo = attn_w_qkv.astype(BF16), attn_w_o.astype(BF16)
    n_attn = cache_k.shape[0]
    ck_t = jnp.transpose(cache_k, (0, 1, 3, 4, 2)).reshape(n_attn, dec_batch, KV_DIM, WINDOW)
    cv_t = jnp.transpose(cache_v, (0, 1, 3, 4, 2)).reshape(n_attn, dec_batch, KV_DIM, WINDOW)
    ck = cv = None
    state_t = jnp.swapaxes(state_pool, 1, 2)

    n_p, n_s = batch * seq, dec_batch * dec_seq
    x = (x_prompt.reshape(n_p, D_MODEL), x_sample.reshape(n_s, D_MODEL))
    pool_p, pool_s, kp_l, vp_l, ks_l, vs_l = [], [], [], [], [], []
    for i in range(depth):
        j = i // n_mixers
        is_attn = i % n_mixers == 1
        if is_attn:
            x, w_in, w_out, cv = _ffn(x, norm_ffn1[i], w_in, w_out, n_p,
                                      next_weights=(ffn2_w_in, ffn2_w_out, i), cache_t=(cv_t, j), tm=FFN_TILE)
        else:
            x, w_in, w_out, state = _ffn(x, norm_ffn1[i], w_in, w_out, n_p,
                                         next_weights=(ffn2_w_in, ffn2_w_out, i), state_t=(state_t, j), tm=FFN_TILE)
        if not is_attn:
            y, sp = _pool(x, None, norm_mix[i], pw, pool_scale[j], j, row0=0, n_seq=batch, t=seq,
                          start_pos=0, bb=1, tt=1024)
            x, ss = _pool(x, state, norm_mix[i], pw, pool_scale[j], j, row0=n_p, n_seq=dec_batch, t=dec_seq,
                          start_pos=PAST_LEN, bb=32, tt=dec_seq, into=y)
            pool_p.append(sp)
            pool_s.append(ss)
        else:
            y, kp, vp = _attn_prompt(x, norm_mix[i], wqkv, attn_b_qkv[j], wo, attn_b_o[j], attn_sinks[j], j,
                                     n_seq=batch, t=seq, nblk=ATTN_BLOCKS_PER_STEP)
            if ck is None:
                ck = jnp.swapaxes(ck_t[j], 1, 2)
            x, kn, vn = _attn_sample(x, ck, cv, norm_mix[i], wqkv, attn_b_qkv[j], wo, attn_b_o[j],
                                     attn_sinks[j], j, y, row0=n_p, dt=dec_seq, bb=32)
            ck = None
            kp_l.append(kp.reshape(batch, WINDOW, N_KV_HEADS, HEAD_DIM))
            vp_l.append(vp.reshape(batch, WINDOW, N_KV_HEADS, HEAD_DIM))
            ks_l.append(kn.reshape(dec_batch, WINDOW, N_KV_HEADS, HEAD_DIM))
            vs_l.append(vn.reshape(dec_batch, WINDOW, N_KV_HEADS, HEAD_DIM))
        if i == depth - 1:
            xp, xs = _ffn(x, norm_ffn2[i], w_in, w_out, n_p, final_gain=norm_final, split_out=True, tm=FFN_TILE)
        elif (i + 1) % n_mixers == 1:
            x, w_in, w_out, ck = _ffn(x, norm_ffn2[i], w_in, w_out, n_p,
                                      next_weights=(ffn1_w_in, ffn1_w_out, i + 1),
                                      cache_t=(ck_t, (i + 1) // n_mixers), tm=FFN_TILE)
        else:
            x, w_in, w_out = _ffn(x, norm_ffn2[i], w_in, w_out, n_p,
                                  next_weights=(ffn1_w_in, ffn1_w_out, i + 1), tm=FFN_TILE)
    return (xp.reshape(batch, seq, D_MODEL), xs.reshape(dec_batch, dec_seq, D_MODEL),
            jnp.stack(pool_p), jnp.stack(pool_s), jnp.stack(kp_l), jnp.stack(vp_l),
            jnp.stack(ks_l), jnp.stack(vs_l))
```

```python
import functools

import jax
import jax.numpy as jnp
from jax import lax
from jax.experimental import pallas as pl
from jax.experimental.pallas import tpu as pltpu

D_MODEL = 1024
D_FF = 2816
POOL_WINDOWS = (2, 4, 8, 16)
POOL_GROUP_DIM = D_MODEL // len(POOL_WINDOWS)
POOL_PREFIX = max(POOL_WINDOWS) - 1
N_HEADS = 16
N_KV_HEADS = 4
HEAD_DIM = 64
GROUP = N_HEADS // N_KV_HEADS
WINDOW = 128
ROPE_THETA = 10000.0
Q_DIM = N_HEADS * HEAD_DIM
KV_DIM = N_KV_HEADS * HEAD_DIM
QKV_DIM = Q_DIM + 2 * KV_DIM
RMS_EPS = 1e-6
NEG_INF = -1e30
PAST_LEN = 8192

LANES = 128
HEADS_PER_VREG = LANES // HEAD_DIM
Q_COLS = Q_DIM // LANES
KV_COLS = KV_DIM // LANES
POOL_HALO = 16
FF_CHUNK = 256
FFN_TILE = 512
FFN_SUB_ROWS = 256
RELAYOUT_SEQS = 4
RESTATE_SEQS = 8
WEIGHT_CAST_STEPS = 16
ATTN_BLOCKS_PER_STEP = 8
ATTN_PROJ_BLOCKS = 2
OPROJ_CHUNKS = 1
PROJ_PIECES = QKV_DIM // (2 * LANES)
PROJ_AT_GROUP = 3
QK_LOOKAHEAD = 3
LOG2_E = 1.4426950408889634
QK_SCALE_LOG2 = HEAD_DIM ** -0.5 * LOG2_E
VMEM_LIMIT = 52 * 1024 * 1024

F32 = jnp.float32
BF16 = jnp.bfloat16


def _rms(x, g):
    ms = jnp.mean(x * x, axis=-1, keepdims=True)
    return x * lax.rsqrt(ms + RMS_EPS) * g


def _resident(shape):
    nd = len(shape)
    return pl.BlockSpec(shape, lambda *_: (0,) * nd, pipeline_mode=pl.Buffered(1))


def _resident_layer(shape, layer):
    nd = len(shape)
    return pl.BlockSpec((None,) + shape, lambda *_: (layer,) + (0,) * nd, pipeline_mode=pl.Buffered(1))


def _zero_bits_of(v):
    bits = lax.bitcast_convert_type(v, jnp.uint32)
    return (bits >> 16) >> 16


def _after(v, zero_bits):
    reps = (v.shape[0] // zero_bits.shape[0], v.shape[1] // zero_bits.shape[1])
    bits = lax.bitcast_convert_type(v, jnp.uint32) | jnp.tile(zero_bits, reps)
    return lax.bitcast_convert_type(bits, F32)


def _ffn_kernel(*refs, n_sample_tiles, split_in, split_out, final_norm, convert_next, relayout, restate):
    refs = list(refs)
    if split_in:
        xp_ref, xs_ref = refs.pop(0), refs.pop(0)
    else:
        x_ref = refs.pop(0)
    g_ref, win_ref, wout_ref = refs[:3]
    del refs[:3]
    gf_ref = refs.pop(0) if final_norm else None
    if convert_next:
        nwin_ref, nwout_ref = refs.pop(0), refs.pop(0)
    if relayout:
        ri_ref = refs.pop(0)
    if restate:
        si_ref = refs.pop(0)
    if split_out:
        op_ref, os_ref = refs.pop(0), refs.pop(0)
    else:
        op_ref = refs.pop(0)
    if convert_next:
        cwin_ref, cwout_ref = refs.pop(0), refs.pop(0)
    if relayout:
        ro_ref = refs.pop(0)
    if restate:
        so_ref = refs.pop(0)
    (h_ref,) = refs

    is_sample = pl.program_id(0) < n_sample_tiles
    x = jnp.where(is_sample, xs_ref[...], xp_ref[...]) if split_in else x_ref[...]
    xn = _rms(x, g_ref[...]).astype(BF16)
    tm = x.shape[0]
    n_chunks = D_FF // FF_CHUNK
    for c in range(n_chunks):
        lo = c * FF_CHUNK
        for r0 in range(0, tm, FFN_SUB_ROWS):
            xs_ = xn[r0:r0 + FFN_SUB_ROWS]
            gate = jnp.dot(xs_, win_ref[:, lo:lo + FF_CHUNK], preferred_element_type=F32)
            up = jnp.dot(xs_, win_ref[:, D_FF + lo:D_FF + lo + FF_CHUNK], preferred_element_type=F32)
            h_ref[r0:r0 + FFN_SUB_ROWS, lo:lo + FF_CHUNK] = (gate * jax.nn.sigmoid(gate) * up).astype(BF16)
        anchor = _zero_bits_of(gate[0:8, 0:LANES])
        if convert_next:
            cw = 2 * D_FF // n_chunks
            cwin_ref[:, c * cw:(c + 1) * cw] = _after(nwin_ref[:, c * cw:(c + 1) * cw], anchor).astype(BF16)
            cr = cwout_ref.shape[0] // n_chunks
            cwout_ref[c * cr:(c + 1) * cr, :] = _after(nwout_ref[c * cr:(c + 1) * cr, :], anchor).astype(BF16)
        if relayout and c < ri_ref.shape[0]:
            ro_ref[c] = _after(ri_ref[c], anchor).T
        if restate:
            for p in range(c, POOL_PREFIX, n_chunks):
                so_ref[:, p, :] = _after(si_ref[p], anchor)
    ys = []
    for r0 in range(0, tm, FFN_SUB_ROWS):
        rows = slice(r0, r0 + FFN_SUB_ROWS)
        yr = x[rows] + 0.5 * jnp.dot(h_ref[rows, :], wout_ref[...], preferred_element_type=F32)
        ys.append(_rms(yr, gf_ref[...]) if final_norm else yr)
    y = jnp.concatenate(ys, axis=0)

    op_ref[...] = y
    if split_out:
        @pl.when(is_sample)
        def _():
            os_ref[...] = op_ref[...]


def _ffn(x, g, w_in, w_out, n_prompt_rows, final_gain=None, next_weights=None, cache_t=None, state_t=None, *,
         split_out=False, tm):
    split_in = isinstance(x, tuple)
    n_rows = sum(a.shape[0] for a in x) if split_in else x.shape[0]
    n_p, n_s = n_prompt_rows // tm, (n_rows - n_prompt_rows) // tm
    final_norm = final_gain is not None
    convert_next = next_weights is not None
    relayout = cache_t is not None
    restate = state_t is not None

    def sample_tile(i):
        return (jnp.minimum(i, n_s - 1), 0)

    def prompt_tile(i):
        return (jnp.maximum(i - n_s, 0), 0)

    def sample_first_tile(i):
        return (jnp.where(i < n_s, n_p + i, i - n_s), 0)

    def same_tile(i):
        return (i, 0)

    reordered = split_in or split_out
    combined_tile = sample_first_tile if reordered else same_tile
    if split_in:
        in_specs = [pl.BlockSpec((tm, D_MODEL), prompt_tile), pl.BlockSpec((tm, D_MODEL), sample_tile)]
        args = list(x)
    else:
        in_specs = [pl.BlockSpec((tm, D_MODEL), combined_tile)]
        args = [x]
    in_specs += [_resident((1, D_MODEL)), _resident((D_MODEL, 2 * D_FF)), _resident((D_FF, D_MODEL))]
    args += [g.reshape(1, D_MODEL), w_in, w_out]
    if split_out:
        out_shape = [jax.ShapeDtypeStruct((n_p * tm, D_MODEL), F32), jax.ShapeDtypeStruct((n_s * tm, D_MODEL), F32)]
        out_specs = [pl.BlockSpec((tm, D_MODEL), prompt_tile), pl.BlockSpec((tm, D_MODEL), sample_tile)]
    else:
        out_shape = [jax.ShapeDtypeStruct((n_rows, D_MODEL), F32)]
        out_specs = [pl.BlockSpec((tm, D_MODEL), combined_tile)]
    if final_norm:
        in_specs.append(_resident((1, D_MODEL)))
        args.append(final_gain.reshape(1, D_MODEL))
    if convert_next:
        nw_in, nw_out, layer = next_weights
        rows_in, rows_out = D_MODEL // WEIGHT_CAST_STEPS, D_FF // WEIGHT_CAST_STEPS

        def slab(i):
            return (jnp.minimum(i, WEIGHT_CAST_STEPS - 1), 0)

        in_specs += [pl.BlockSpec((None, rows_in, 2 * D_FF), lambda i: (layer,) + slab(i)),
                     pl.BlockSpec((None, rows_out, D_MODEL), lambda i: (layer,) + slab(i))]
        args += [nw_in, nw_out]
        out_shape += [jax.ShapeDtypeStruct((D_MODEL, 2 * D_FF), BF16), jax.ShapeDtypeStruct((D_FF, D_MODEL), BF16)]
        out_specs += [pl.BlockSpec((rows_in, 2 * D_FF), slab), pl.BlockSpec((rows_out, D_MODEL), slab)]
    if relayout:
        cache, cache_layer = cache_t
        n_seq = cache.shape[1]
        last_group = n_seq // RELAYOUT_SEQS - 1

        def seq_group(i):
            return (jnp.minimum(i, last_group), 0, 0)

        in_specs.append(pl.BlockSpec((None, RELAYOUT_SEQS, KV_DIM, WINDOW), lambda i: (cache_layer,) + seq_group(i)))
        args.append(cache)
        out_shape.append(jax.ShapeDtypeStruct((n_seq, WINDOW, KV_DIM), F32))
        out_specs.append(pl.BlockSpec((RELAYOUT_SEQS, WINDOW, KV_DIM), seq_group))
    if restate:
        state, state_layer = state_t
        n_seq = state.shape[2]
        last_state_group = n_seq // RESTATE_SEQS - 1

        def state_group(i):
            return jnp.minimum(i, last_state_group)

        in_specs.append(pl.BlockSpec((None, POOL_PREFIX, RESTATE_SEQS, D_MODEL),
                                     lambda i: (state_layer, 0, state_group(i), 0)))
        args.append(state)
        out_shape.append(jax.ShapeDtypeStruct((n_seq, POOL_PREFIX, D_MODEL), F32))
        out_specs.append(pl.BlockSpec((RESTATE_SEQS, POOL_PREFIX, D_MODEL), lambda i: (state_group(i), 0, 0)))
    return pl.pallas_call(
        functools.partial(_ffn_kernel, n_sample_tiles=n_s, split_in=split_in, split_out=split_out,
                          final_norm=final_norm, convert_next=convert_next, relayout=relayout, restate=restate),
        out_shape=tuple(out_shape),
        grid=(n_p + n_s,),
        in_specs=in_specs,
        out_specs=tuple(out_specs),
        scratch_shapes=[pltpu.VMEM((tm, D_FF), BF16)],
        compiler_params=pltpu.CompilerParams(
            dimension_semantics=("arbitrary",), vmem_limit_bytes=VMEM_LIMIT),
        name="ffn_final" if final_norm else "ffn",
    )(*args)


def _pool_kernel(x_ref, *rest, has_prefix, start_pos, bb, tt):
    rest = list(rest)
    pre_ref = rest.pop(0) if has_prefix else None
    g_ref, w_ref, sc_ref, o_ref, st_ref, ext_ref = rest
    ti = pl.program_id(1)
    te = tt + POOL_HALO

    @pl.when(ti == 0)
    def _():
        ext_ref[:, 0:POOL_HALO, :] = jnp.zeros((bb, POOL_HALO, D_MODEL), F32)
        if has_prefix:
            ext_ref[:, POOL_HALO - POOL_PREFIX:POOL_HALO, :] = pre_ref[...]

    if tt >= POOL_HALO:
        @pl.when(ti > 0)
        def _():
            ext_ref[:, 0:POOL_HALO, :] = ext_ref[:, tt:te, :]

    x = x_ref[...].reshape(bb, tt, D_MODEL)
    u = _rms(x, g_ref[...])
    ext_ref[:, POOL_HALO:te, :] = u

    n_seen = start_pos + ti * tt + lax.broadcasted_iota(jnp.int32, (1, tt, 1), 1) + 1
    for gi, wg in enumerate(POOL_WINDOWS):
        sl = slice(gi * POOL_GROUP_DIM, (gi + 1) * POOL_GROUP_DIM)
        s = ext_ref[:, :, sl].reshape(bb * te, POOL_GROUP_DIM)
        span = 1
        while span < wg:
            s = s + pltpu.roll(s, span, axis=0)
            span *= 2
        s = s.reshape(bb, te, POOL_GROUP_DIM)[:, POOL_HALO:, :]
        cnt = jnp.minimum(n_seen, wg).astype(F32)
        p = s / cnt - u[:, :, sl]
        p2 = p.reshape(bb * tt, POOL_GROUP_DIM).astype(BF16)
        y = jnp.dot(p2, w_ref[gi], preferred_element_type=F32).reshape(bb, tt, POOL_GROUP_DIM)
        o_ref[:, sl] = (x[:, :, sl] + y * sc_ref[:, :, sl]).reshape(bb * tt, POOL_GROUP_DIM)

    @pl.when(ti == pl.num_programs(1) - 1)
    def _():
        st_ref[...] = ext_ref[:, te - POOL_PREFIX:te, :]


def _row_blocks(row0, n_seq, t, bb, tt):
    first, per_seq = row0 // (bb * tt), t // tt
    return lambda i, j: (first + i * per_seq + j, 0)


def _pool(x, prefix, g, w, scale, layer, *, row0, n_seq, t, start_pos, bb, tt):
    has_prefix = prefix is not None
    rows = _row_blocks(row0, n_seq, t, bb, tt)
    in_specs = [pl.BlockSpec((bb * tt, D_MODEL), rows)]
    args = [x]
    if has_prefix:
        in_specs.append(pl.BlockSpec((bb, POOL_PREFIX, D_MODEL), lambda i, j: (i, 0, 0)))
        args.append(prefix)
    in_specs += [
        _resident((1, 1, D_MODEL)),
        _resident_layer((len(POOL_WINDOWS), POOL_GROUP_DIM, POOL_GROUP_DIM), layer),
        _resident((1, 1, D_MODEL)),
    ]
    args += [g.reshape(1, 1, D_MODEL), w, scale.reshape(1, 1, D_MODEL)]
    return pl.pallas_call(
        functools.partial(_pool_kernel, has_prefix=has_prefix, start_pos=start_pos, bb=bb, tt=tt),
        out_shape=(jax.ShapeDtypeStruct(x.shape, F32),
                   jax.ShapeDtypeStruct((n_seq, POOL_PREFIX, D_MODEL), F32)),
        grid=(n_seq // bb, t // tt),
        in_specs=in_specs,
        out_specs=(pl.BlockSpec((bb * tt, D_MODEL), rows),
                   pl.BlockSpec((bb, POOL_PREFIX, D_MODEL), lambda i, j: (i, 0, 0))),
        scratch_shapes=[pltpu.VMEM((bb, tt + POOL_HALO, D_MODEL), F32)],
        input_output_aliases={0: 0},
        compiler_params=pltpu.CompilerParams(
            dimension_semantics=("arbitrary", "arbitrary"), vmem_limit_bytes=VMEM_LIMIT),
        name="pool_sample" if has_prefix else "pool_prompt",
    )(*args)


def _rope_tables(pos):
    half = HEAD_DIM // 2
    inv = ROPE_THETA ** (-jnp.arange(half, dtype=F32) / half)
    ang = pos.astype(F32)[:, None] * inv[None, :]
    cos, sin = jnp.cos(ang), jnp.sin(ang)
    cos64 = jnp.concatenate([cos, cos], axis=-1)
    sin64 = jnp.concatenate([-sin, sin], axis=-1)
    return jnp.tile(cos64, (1, HEADS_PER_VREG)), jnp.tile(sin64, (1, HEADS_PER_VREG))


def _rope_col(xc, cos, sin, first_half):
    half = HEAD_DIM // 2
    partner = jnp.where(first_half, pltpu.roll(xc, LANES - half, axis=1), pltpu.roll(xc, half, axis=1))
    return xc * cos + partner * sin


def _head_half_variants(col, lo_half):
    sw = pltpu.roll(col, HEAD_DIM, axis=1)
    zero = jnp.zeros_like(col)
    return {
        (0, 0): jnp.where(lo_half, col, zero),
        (0, 1): jnp.where(lo_half, zero, sw),
        (1, 0): jnp.where(lo_half, sw, zero),
        (1, 1): jnp.where(lo_half, zero, col),
    }


def _softmax_with_sink(s, sink):
    m = jnp.maximum(jnp.max(s, axis=-1, keepdims=True), sink)
    e = jnp.exp2(s - m)
    denom = jnp.sum(e, axis=-1, keepdims=True) + jnp.exp2(sink - m)
    return e.astype(BF16), 1.0 / denom


def _attn_prompt_kernel(x_ref, cos_ref, sin_ref, g_ref, wqkv_ref, bqkv_ref, wo_ref, bo_ref, sink_ref,
                        o_ref, kout_ref, vout_ref, kt_ref, vlo_ref, vhi_ref, *, nblk):
    j = pl.program_id(1)
    w = WINDOW
    r = nblk * w

    @pl.when(j == 0)
    def _():
        kt_ref[...] = jnp.zeros(kt_ref.shape, BF16)
        vlo_ref[:, 0:w, 0:LANES] = jnp.zeros((N_KV_HEADS, w, LANES), BF16)
        vhi_ref[:, 0:w, 0:LANES] = jnp.zeros((N_KV_HEADS, w, LANES), BF16)
        keys = vlo_ref.shape[1]
        ones_lo = (lax.broadcasted_iota(jnp.int32, (N_KV_HEADS, keys, LANES), 2) < HEAD_DIM).astype(BF16)
        vlo_ref[:, :, LANES:] = ones_lo
        vhi_ref[:, :, LANES:] = 1 - ones_lo

    @pl.when(j > 0)
    def _():
        kt_ref[:, :, 0:w] = kt_ref[:, :, r:r + w]
        vlo_ref[:, 0:w, 0:LANES] = vlo_ref[:, r:r + w, 0:LANES]
        vhi_ref[:, 0:w, 0:LANES] = vhi_ref[:, r:r + w, 0:LANES]

    rc = ATTN_PROJ_BLOCKS * w
    n_chunks = nblk // ATTN_PROJ_BLOCKS
    lane = lax.broadcasted_iota(jnp.int32, (rc, LANES), 1)
    first_half = (lane % HEAD_DIM) < (HEAD_DIM // 2)
    lo_half = lane < HEAD_DIM
    lo_half_2w = lax.broadcasted_iota(jnp.int32, (2 * w, LANES), 1) < HEAD_DIM

    u_of, qcols_of = {}, {}

    def project_piece(ch, t):
        rows = slice(ch * rc, (ch + 1) * rc)
        if t == 0:
            u_of[ch] = _rms(x_ref[rows, :], g_ref[...]).astype(BF16)
            qcols_of[ch] = []
        cols = slice(t * 2 * LANES, (t + 1) * 2 * LANES)
        tile = jnp.dot(u_of[ch], wqkv_ref[:, cols], preferred_element_type=F32) + bqkv_ref[:, cols]
        halves = [tile[:, :LANES], tile[:, LANES:]]
        if t < Q_COLS // 2:
            cos, sin = cos_ref[rows, :], sin_ref[rows, :]
            for hc in halves:
                qcols_of[ch].append((_rope_col(hc, cos, sin, first_half) * QK_SCALE_LOG2).astype(BF16))
            return
        is_k = t == Q_COLS // 2
        if is_k:
            cos, sin = cos_ref[rows, :], sin_ref[rows, :]
            halves = [_rope_col(hc, cos, sin, first_half) for hc in halves]
        krows = slice(w + ch * rc, w + (ch + 1) * rc)
        for c, col in enumerate(halves):
            if ch == n_chunks - 1:
                out_ref = kout_ref if is_k else vout_ref
                out_ref[0, :, c * LANES:(c + 1) * LANES] = col[rc - w:]
            if is_k:
                col_t = col.T.astype(BF16)
                for hb in range(HEADS_PER_VREG):
                    kvh = c * HEADS_PER_VREG + hb
                    head_t = col_t[hb * HEAD_DIM:(hb + 1) * HEAD_DIM, :]
                    for a in range(HEADS_PER_VREG):
                        kt_ref[kvh * HEADS_PER_VREG + a, a * HEAD_DIM:(a + 1) * HEAD_DIM, krows] = head_t
            else:
                var = _head_half_variants(col, lo_half)
                for hb in range(HEADS_PER_VREG):
                    kvh = c * HEADS_PER_VREG + hb
                    vlo_ref[kvh, krows, 0:LANES] = var[(hb, 0)].astype(BF16)
                    vhi_ref[kvh, krows, 0:LANES] = var[(hb, 1)].astype(BF16)

    for t in range(PROJ_PIECES):
        project_piece(0, t)

    row = lax.broadcasted_iota(jnp.int32, (2 * w, 2 * w), 0) % w
    col = lax.broadcasted_iota(jnp.int32, (2 * w, 2 * w), 1)
    band = (col > row) & (col <= row + w)
    first_key = jnp.where(j > 0, 0, w)
    band_first = band & (col >= first_key)
    top_rows = lax.broadcasted_iota(jnp.int32, (2 * w, 1), 0) < w

    groups = [(ch, kvh, bl, a) for ch in range(n_chunks) for kvh in range(N_KV_HEADS)
              for bl in range(ATTN_PROJ_BLOCKS) for a in range(HEADS_PER_VREG)]

    def scores(ch, kvh, bl, a):
        qcols = qcols_of[ch]
        qrows = slice(bl * w, (bl + 1) * w)
        blk = ch * ATTN_PROJ_BLOCKS + bl
        qop = jnp.concatenate([qcols[2 * kvh][qrows], qcols[2 * kvh + 1][qrows]], axis=0)
        return jnp.dot(qop, kt_ref[kvh * HEADS_PER_VREG + a, :, blk * w:(blk + 2) * w],
                       preferred_element_type=F32)

    groups_per_chunk = len(groups) // n_chunks
    pending = [scores(*g) for g in groups[:QK_LOOKAHEAD]]
    es, sink_terms, o_parts, held, acc = [], [], [], {}, None
    for gi, (ch, kvh, bl, a) in enumerate(groups):
        gl = gi % groups_per_chunk
        if 0 <= gl - PROJ_AT_GROUP < PROJ_PIECES and ch + 1 < n_chunks:
            project_piece(ch + 1, gl - PROJ_AT_GROUP)
        if gi + QK_LOOKAHEAD < len(groups):
            pending.append(scores(*groups[gi + QK_LOOKAHEAD]))
        blk = ch * ATTN_PROJ_BLOCKS + bl
        s = jnp.where(band_first if blk == 0 else band, pending[gi], NEG_INF)
        pending[gi] = None
        h0 = kvh * GROUP + a
        sink = jnp.where(top_rows, sink_ref[h0] * LOG2_E, sink_ref[h0 + HEADS_PER_VREG] * LOG2_E)
        m = jnp.maximum(jnp.max(s, axis=-1, keepdims=True), sink)
        es.append(jnp.exp2(s - m).astype(BF16))
        sink_terms.append(jnp.exp2(sink - m))
        if a < HEADS_PER_VREG - 1:
            continue
        krows = slice(blk * w, (blk + 2) * w)
        vstack = jnp.concatenate([vlo_ref[kvh, krows, :], vhi_ref[kvh, krows, :]], axis=0)
        res = jnp.dot(jnp.concatenate(es, axis=1), vstack, preferred_element_type=F32)
        den = res[:, LANES:] + jnp.where(lo_half_2w, sink_terms[0], sink_terms[1])
        o = res[:, :LANES] * (1.0 / den)
        o_parts.append(jnp.concatenate([o[:w], o[w:]], axis=1).astype(BF16))
        es, sink_terms = [], []
        if bl < ATTN_PROJ_BLOCKS - 1:
            continue
        held[(ch % OPROJ_CHUNKS, kvh)] = jnp.concatenate(o_parts, axis=0)
        o_parts = []
        if ch % OPROJ_CHUNKS < OPROJ_CHUNKS - 1:
            continue
        part = jnp.dot(jnp.concatenate([held.pop((cc, kvh)) for cc in range(OPROJ_CHUNKS)], axis=0),
                       wo_ref[kvh * GROUP * HEAD_DIM:(kvh + 1) * GROUP * HEAD_DIM, :],
                       preferred_element_type=F32)
        acc = part if acc is None else acc + part
        if kvh == N_KV_HEADS - 1:
            rows = slice((ch + 1 - OPROJ_CHUNKS) * rc, (ch + 1) * rc)
            o_ref[rows, :] = x_ref[rows, :] + acc + bo_ref[...]
            acc = None


def _attn_prompt(x, g, w_qkv, b_qkv, w_o, b_o, sinks, layer, *, n_seq, t, nblk):
    b = n_seq
    r = nblk * WINDOW
    cos, sin = _rope_tables(jnp.arange(t))
    keys = (nblk + 1) * WINDOW
    rows = _row_blocks(0, n_seq, t, 1, r)
    return pl.pallas_call(
        functools.partial(_attn_prompt_kernel, nblk=nblk),
        out_shape=(jax.ShapeDtypeStruct(x.shape, F32),
                   jax.ShapeDtypeStruct((b, WINDOW, KV_DIM), F32),
                   jax.ShapeDtypeStruct((b, WINDOW, KV_DIM), F32)),
        grid=(b, t // r),
        in_specs=[
            pl.BlockSpec((r, D_MODEL), rows),
            pl.BlockSpec((r, LANES), lambda i, j: (j, 0)),
            pl.BlockSpec((r, LANES), lambda i, j: (j, 0)),
            _resident((1, D_MODEL)),
            _resident_layer((D_MODEL, QKV_DIM), layer),
            _resident((1, QKV_DIM)),
            _resident_layer((Q_DIM, D_MODEL), layer),
            _resident((1, D_MODEL)),
            pl.BlockSpec(memory_space=pltpu.SMEM),
        ],
        out_specs=(pl.BlockSpec((r, D_MODEL), rows),
                   pl.BlockSpec((1, WINDOW, KV_DIM), lambda i, j: (i, 0, 0)),
                   pl.BlockSpec((1, WINDOW, KV_DIM), lambda i, j: (i, 0, 0))),
        scratch_shapes=[pltpu.VMEM((2 * N_KV_HEADS, LANES, keys), BF16),
                        pltpu.VMEM((N_KV_HEADS, keys, 2 * LANES), BF16),
                        pltpu.VMEM((N_KV_HEADS, keys, 2 * LANES), BF16)],
        input_output_aliases={0: 0},
        compiler_params=pltpu.CompilerParams(
            dimension_semantics=("arbitrary", "arbitrary"), vmem_limit_bytes=VMEM_LIMIT),
        name="attn_prompt",
    )(x, cos, sin, g.reshape(1, D_MODEL), w_qkv, b_qkv.reshape(1, QKV_DIM), w_o,
      b_o.reshape(1, D_MODEL), sinks)


SAMPLE_KEYS = 2 * WINDOW
SAMPLE_UNROLL = 4


def _attn_sample_kernel(x_ref, ck_ref, cv_ref, cos_ref, sin_ref, g_ref, wqkv_ref, bqkv_ref, wo_ref, bo_ref,
                        sinkcol_ref, o_ref, kout_ref, vout_ref, q_s, kn_s, vn_s, kk_s, vv_s, oa_s, *, dt):
    bb = ck_ref.shape[0]
    w = WINDOW
    x = x_ref[...]
    u = _rms(x, g_ref[...]).astype(BF16)
    qkv = jnp.dot(u, wqkv_ref[...], preferred_element_type=F32) + bqkv_ref[...]
    cos, sin = cos_ref[...], sin_ref[...]
    lane_t = lax.broadcasted_iota(jnp.int32, (bb * dt, LANES), 1)
    first_half = (lane_t % HEAD_DIM) < (HEAD_DIM // 2)
    for c in range(Q_COLS):
        qc = _rope_col(qkv[:, c * LANES:(c + 1) * LANES], cos, sin, first_half)
        q_s[:, c * LANES:(c + 1) * LANES] = qc * QK_SCALE_LOG2
    for c in range(KV_COLS):
        kn_s[:, c * LANES:(c + 1) * LANES] = _rope_col(
            qkv[:, Q_DIM + c * LANES:Q_DIM + (c + 1) * LANES], cos, sin, first_half)
    vn_s[...] = qkv[:, Q_DIM + KV_DIM:]

    kk_s[:, w + dt:, :] = jnp.zeros((SAMPLE_UNROLL, SAMPLE_KEYS - w - dt, KV_DIM), BF16)
    vv_s[:, w + dt:, :] = jnp.zeros((SAMPLE_UNROLL, SAMPLE_KEYS - w - dt, KV_DIM), BF16)

    heads_per_col = N_HEADS // KV_COLS
    qcols_per_col = heads_per_col // HEADS_PER_VREG
    rows = heads_per_col * dt
    lane8 = lax.broadcasted_iota(jnp.int32, (dt, LANES), 1)
    lo8 = lane8 < HEAD_DIM
    t_row = lax.broadcasted_iota(jnp.int32, (rows, SAMPLE_KEYS), 0) % dt
    key = lax.broadcasted_iota(jnp.int32, (rows, SAMPLE_KEYS), 1)
    valid = (key > t_row) & (key <= t_row + w)

    def stage(b, slot):
        r0 = pl.multiple_of(b * dt, dt)
        knew = kn_s[pl.ds(r0, dt), :]
        vnew = vn_s[pl.ds(r0, dt), :]
        kout_ref[b, 0:w - dt, :] = ck_ref[b, dt:w, :]
        kout_ref[b, w - dt:w, :] = knew
        vout_ref[b, 0:w - dt, :] = cv_ref[b, dt:w, :]
        vout_ref[b, w - dt:w, :] = vnew
        kk_s[slot, 0:w, :] = ck_ref[b].astype(BF16)
        kk_s[slot, w:w + dt, :] = knew.astype(BF16)
        vv_s[slot, 0:w, :] = cv_ref[b].astype(BF16)
        vv_s[slot, w:w + dt, :] = vnew.astype(BF16)

    def scores(b, slot, c):
        r0 = pl.multiple_of(b * dt, dt)
        pieces = []
        for h8 in range(heads_per_col):
            qi = c * qcols_per_col + h8 // HEADS_PER_VREG
            qcol = q_s[pl.ds(r0, dt), qi * LANES:(qi + 1) * LANES]
            a, hb = h8 % HEADS_PER_VREG, h8 // GROUP
            src = qcol if a == hb else pltpu.roll(qcol, HEAD_DIM, axis=1)
            keep = lo8 if hb == 0 else jnp.logical_not(lo8)
            pieces.append(jnp.where(keep, src, 0.0))
        qop = jnp.concatenate(pieces, axis=0).astype(BF16)
        return lax.dot_general(qop, kk_s[slot, :, c * LANES:(c + 1) * LANES],
                               (((1,), (1,)), ((), ())), preferred_element_type=F32)

    def place(b, c, o):
        r0 = pl.multiple_of(b * dt, dt)
        for mm in range(qcols_per_col):
            halves = []
            for a in range(HEADS_PER_VREG):
                h8 = mm * HEADS_PER_VREG + a
                hb = h8 // GROUP
                piece = o[h8 * dt:(h8 + 1) * dt, :]
                halves.append(piece if a == hb else pltpu.roll(piece, HEAD_DIM, axis=1))
            oi = c * qcols_per_col + mm
            oa_s[pl.ds(r0, dt), oi * LANES:(oi + 1) * LANES] = jnp.where(lo8, halves[0], halves[1])

    def body(gidx, carry):
        work = [(gidx * SAMPLE_UNROLL + slot, slot, c) for slot in range(SAMPLE_UNROLL) for c in range(KV_COLS)]
        for slot in range(SAMPLE_UNROLL):
            stage(gidx * SAMPLE_UNROLL + slot, slot)
        ss = [scores(b, slot, c) for b, slot, c in work]
        sm = [_softmax_with_sink(jnp.where(valid, s, NEG_INF), sinkcol_ref[c * rows:(c + 1) * rows, :] * LOG2_E)
              for s, (b, slot, c) in zip(ss, work)]
        for (e, rinv), (b, slot, c) in zip(sm, work):
            o = jnp.dot(e, vv_s[slot, :, c * LANES:(c + 1) * LANES], preferred_element_type=F32) * rinv
            place(b, c, o)
        return carry

    lax.fori_loop(0, bb // SAMPLE_UNROLL, body, 0)

    o_ref[...] = x + jnp.dot(oa_s[...].astype(BF16), wo_ref[...], preferred_element_type=F32) + bo_ref[...]


def _attn_sample(x, cache_k, cache_v, g, w_qkv, b_qkv, w_o, b_o, sinks, layer, *, row0, dt, bb):
    b = cache_k.shape[0]
    rows = _row_blocks(row0, b, dt, bb, dt)
    rows1 = lambda i: rows(i, 0)
    cos, sin = _rope_tables(PAST_LEN + jnp.arange(dt))
    cos, sin = jnp.tile(cos, (bb, 1)), jnp.tile(sin, (bb, 1))
    sinkcol = jnp.repeat(sinks, dt).reshape(N_HEADS * dt, 1)
    n = bb * dt
    return pl.pallas_call(
        functools.partial(_attn_sample_kernel, dt=dt),
        out_shape=(jax.ShapeDtypeStruct(x.shape, F32),
                   jax.ShapeDtypeStruct((b, WINDOW, KV_DIM), F32),
                   jax.ShapeDtypeStruct((b, WINDOW, KV_DIM), F32)),
        grid=(b // bb,),
        in_specs=[
            pl.BlockSpec((n, D_MODEL), rows1),
            pl.BlockSpec((bb, WINDOW, KV_DIM), lambda i: (i, 0, 0)),
            pl.BlockSpec((bb, WINDOW, KV_DIM), lambda i: (i, 0, 0)),
            _resident((n, LANES)),
            _resident((n, LANES)),
            _resident((1, D_MODEL)),
            _resident_layer((D_MODEL, QKV_DIM), layer),
            _resident((1, QKV_DIM)),
            _resident_layer((Q_DIM, D_MODEL), layer),
            _resident((1, D_MODEL)),
            _resident((N_HEADS * dt, 1)),
        ],
        out_specs=(pl.BlockSpec((n, D_MODEL), rows1),
                   pl.BlockSpec((bb, WINDOW, KV_DIM), lambda i: (i, 0, 0)),
                   pl.BlockSpec((bb, WINDOW, KV_DIM), lambda i: (i, 0, 0))),
        scratch_shapes=[pltpu.VMEM((n, Q_DIM), F32),
                        pltpu.VMEM((n, KV_DIM), F32),
                        pltpu.VMEM((n, KV_DIM), F32),
                        pltpu.VMEM((SAMPLE_UNROLL, SAMPLE_KEYS, KV_DIM), BF16),
                        pltpu.VMEM((SAMPLE_UNROLL, SAMPLE_KEYS, KV_DIM), BF16),
                        pltpu.VMEM((n, Q_DIM), F32)],
        input_output_aliases={0: 0},
        compiler_params=pltpu.CompilerParams(
            dimension_semantics=("arbitrary",), vmem_limit_bytes=VMEM_LIMIT),
        name="attn_sample",
    )(x, cache_k, cache_v, cos, sin, g.reshape(1, D_MODEL), w_qkv, b_qkv.reshape(1, QKV_DIM), w_o,
      b_o.reshape(1, D_MODEL), sinkcol)


def kernel(x_prompt, x_sample, state_pool, cache_k, cache_v, norm_ffn1, ffn1_w_in, ffn1_w_out, norm_mix,
           norm_ffn2, ffn2_w_in, ffn2_w_out, pool_w, pool_scale, attn_w_qkv, attn_b_qkv, attn_w_o, attn_b_o,
           attn_sinks, norm_final):
    batch, seq, _ = x_prompt.shape
    dec_batch, dec_seq, _ = x_sample.shape
    depth = norm_ffn1.shape[0]
    n_mixers = 2
    w_in, w_out = ffn1_w_in[0].astype(BF16), ffn1_w_out[0].astype(BF16)
    pw = pool_w.astype(BF16)
    wqkv, wo = attn_w_qkv.astype(BF16), attn_w_o.astype(BF16)
    n_attn = cache_k.shape[0]
    ck_t = jnp.transpose(cache_k, (0, 1, 3, 4, 2)).reshape(n_attn, dec_batch, KV_DIM, WINDOW)
    cv_t = jnp.transpose(cache_v, (0, 1, 3, 4, 2)).reshape(n_attn, dec_batch, KV_DIM, WINDOW)
    ck = cv = None
    state_t = jnp.swapaxes(state_pool, 1, 2)

    n_p, n_s = batch * seq, dec_batch * dec_seq
    x = (x_prompt.reshape(n_p, D_MODEL), x_sample.reshape(n_s, D_MODEL))
    pool_p, pool_s, kp_l, vp_l, ks_l, vs_l = [], [], [], [], [], []
    for i in range(depth):
        j = i // n_mixers
        is_attn = i % n_mixers == 1
        if is_attn:
            x, w_in, w_out, cv = _ffn(x, norm_ffn1[i], w_in, w_out, n_p,
                                      next_weights=(ffn2_w_in, ffn2_w_out, i), cache_t=(cv_t, j), tm=FFN_TILE)
        else:
            x, w_in, w_out, state = _ffn(x, norm_ffn1[i], w_in, w_out, n_p,
                                         next_weights=(ffn2_w_in, ffn2_w_out, i), state_t=(state_t, j), tm=FFN_TILE)
        if not is_attn:
            x, sp = _pool(x, None, norm_mix[i], pw, pool_scale[j], j, row0=0, n_seq=batch, t=seq,
                          start_pos=0, bb=1, tt=1024)
            x, ss = _pool(x, state, norm_mix[i], pw, pool_scale[j], j, row0=n_p, n_seq=dec_batch, t=dec_seq,
                          start_pos=PAST_LEN, bb=32, tt=dec_seq)
            pool_p.append(sp)
            pool_s.append(ss)
        else:
            x, kp, vp = _attn_prompt(x, norm_mix[i], wqkv, attn_b_qkv[j], wo, attn_b_o[j], attn_sinks[j], j,
                                     n_seq=batch, t=seq, nblk=ATTN_BLOCKS_PER_STEP)
            if ck is None:
                ck = jnp.swapaxes(ck_t[j], 1, 2)
            x, kn, vn = _attn_sample(x, ck, cv, norm_mix[i], wqkv, attn_b_qkv[j], wo, attn_b_o[j],
                                     attn_sinks[j], j, row0=n_p, dt=dec_seq, bb=32)
            ck = None
            kp_l.append(kp.reshape(batch, WINDOW, N_KV_HEADS, HEAD_DIM))
            vp_l.append(vp.reshape(batch, WINDOW, N_KV_HEADS, HEAD_DIM))
            ks_l.append(kn.reshape(dec_batch, WINDOW, N_KV_HEADS, HEAD_DIM))
            vs_l.append(vn.reshape(dec_batch, WINDOW, N_KV_HEADS, HEAD_DIM))
        if i == depth - 1:
            xp, xs = _ffn(x, norm_ffn2[i], w_in, w_out, n_p, final_gain=norm_final, split_out=True, tm=FFN_TILE)
        elif (i + 1) % n_mixers == 1:
            x, w_in, w_out, ck = _ffn(x, norm_ffn2[i], w_in, w_out, n_p,
                                      next_weights=(ffn1_w_in, ffn1_w_out, i + 1),
                                      cache_t=(ck_t, (i + 1) // n_mixers), tm=FFN_TILE)
        else:
            x, w_in, w_out = _ffn(x, norm_ffn2[i], w_in, w_out, n_p,
                                  next_weights=(ffn1_w_in, ffn1_w_out, i + 1), tm=FFN_TILE)
    return (xp.reshape(batch, seq, D_MODEL), xs.reshape(dec_batch, dec_seq, D_MODEL),
            jnp.stack(pool_p), jnp.stack(pool_s), jnp.stack(kp_l), jnp.stack(vp_l),
            jnp.stack(ks_l), jnp.stack(vs_l))
```

```python
import functools

import jax
import jax.numpy as jnp
from jax import lax
from jax.experimental import pallas as pl
from jax.experimental.pallas import tpu as pltpu

D_MODEL = 1024
D_FF = 2816
POOL_WINDOWS = (2, 4, 8, 16)
POOL_GROUP_DIM = D_MODEL // len(POOL_WINDOWS)
POOL_PREFIX = max(POOL_WINDOWS) - 1
N_HEADS = 16
N_KV_HEADS = 4
HEAD_DIM = 64
GROUP = N_HEADS // N_KV_HEADS
WINDOW = 128
ROPE_THETA = 10000.0
Q_DIM = N_HEADS * HEAD_DIM
KV_DIM = N_KV_HEADS * HEAD_DIM
QKV_DIM = Q_DIM + 2 * KV_DIM
RMS_EPS = 1e-6
NEG_INF = -1e30
PAST_LEN = 8192

LANES = 128
SUBLANES = 8
HEADS_PER_VREG = LANES // HEAD_DIM
Q_COLS = Q_DIM // LANES
KV_COLS = KV_DIM // LANES
POOL_HALO = 16
FF_CHUNK = 256
FFN_TILE = 512
FFN_SUB_ROWS = 256
RELAYOUT_SEQS = 4
RESTATE_SEQS = 8
WEIGHT_CAST_STEPS = 16
ATTN_BLOCKS_PER_STEP = 8
ATTN_PROJ_BLOCKS = 2
PROJ_PIECES = QKV_DIM // (2 * LANES)
PROJ_AT_GROUP = 3
QK_LOOKAHEAD = 3
LOG2_E = 1.4426950408889634
QK_SCALE_LOG2 = HEAD_DIM ** -0.5 * LOG2_E
VMEM_LIMIT = 52 * 1024 * 1024

F32 = jnp.float32
BF16 = jnp.bfloat16


def _rms(x, g):
    ms = jnp.mean(x * x, axis=-1, keepdims=True)
    return x * lax.rsqrt(ms + RMS_EPS) * g


def _resident(shape):
    nd = len(shape)
    return pl.BlockSpec(shape, lambda *_: (0,) * nd, pipeline_mode=pl.Buffered(1))


def _resident_layer(shape, layer):
    nd = len(shape)
    return pl.BlockSpec((None,) + shape, lambda *_: (layer,) + (0,) * nd, pipeline_mode=pl.Buffered(1))


def _zero_bits_of(v):
    bits = lax.bitcast_convert_type(v, jnp.uint32)
    return (bits >> 16) >> 16


def _after(v, zero_bits):
    reps = (v.shape[0] // zero_bits.shape[0], v.shape[1] // zero_bits.shape[1])
    bits = lax.bitcast_convert_type(v, jnp.uint32) | jnp.tile(zero_bits, reps)
    return lax.bitcast_convert_type(bits, F32)


def _ffn_kernel(*refs, n_sample_tiles, split_in, split_out, final_norm, convert_next, relayout, restate):
    refs = list(refs)
    if split_in:
        xp_ref, xs_ref = refs.pop(0), refs.pop(0)
    else:
        x_ref = refs.pop(0)
    g_ref, win_ref, wout_ref = refs[:3]
    del refs[:3]
    gf_ref = refs.pop(0) if final_norm else None
    if convert_next:
        nwin_ref, nwout_ref = refs.pop(0), refs.pop(0)
    if relayout:
        ri_ref = refs.pop(0)
    if restate:
        si_ref = refs.pop(0)
    if split_out:
        op_ref, os_ref = refs.pop(0), refs.pop(0)
    else:
        op_ref = refs.pop(0)
    if convert_next:
        cwin_ref, cwout_ref = refs.pop(0), refs.pop(0)
    if relayout:
        ro_ref = refs.pop(0)
    if restate:
        so_ref = refs.pop(0)
    (h_ref,) = refs

    is_sample = pl.program_id(0) < n_sample_tiles
    x = jnp.where(is_sample, xs_ref[...], xp_ref[...]) if split_in else x_ref[...]
    xn = _rms(x, g_ref[...]).astype(BF16)
    tm = x.shape[0]
    n_chunks = D_FF // FF_CHUNK
    for c in range(n_chunks):
        lo = c * FF_CHUNK
        for r0 in range(0, tm, FFN_SUB_ROWS):
            xs_ = xn[r0:r0 + FFN_SUB_ROWS]
            gate = jnp.dot(xs_, win_ref[:, lo:lo + FF_CHUNK], preferred_element_type=F32)
            up = jnp.dot(xs_, win_ref[:, D_FF + lo:D_FF + lo + FF_CHUNK], preferred_element_type=F32)
            h_ref[r0:r0 + FFN_SUB_ROWS, lo:lo + FF_CHUNK] = (gate * jax.nn.sigmoid(gate) * up).astype(BF16)
        anchor = _zero_bits_of(gate[0:SUBLANES, 0:LANES])
        if convert_next:
            cw = 2 * D_FF // n_chunks
            cwin_ref[:, c * cw:(c + 1) * cw] = _after(nwin_ref[:, c * cw:(c + 1) * cw], anchor).astype(BF16)
            cr = cwout_ref.shape[0] // n_chunks
            cwout_ref[c * cr:(c + 1) * cr, :] = _after(nwout_ref[c * cr:(c + 1) * cr, :], anchor).astype(BF16)
        if relayout and c < ri_ref.shape[0]:
            ro_ref[c] = _after(ri_ref[c], anchor).T
        if restate:
            for p in range(c, POOL_PREFIX, n_chunks):
                so_ref[:, p, :] = _after(si_ref[p], anchor)
    ys = []
    for r0 in range(0, tm, FFN_SUB_ROWS):
        rows = slice(r0, r0 + FFN_SUB_ROWS)
        yr = x[rows] + 0.5 * jnp.dot(h_ref[rows, :], wout_ref[...], preferred_element_type=F32)
        ys.append(_rms(yr, gf_ref[...]) if final_norm else yr)
    y = jnp.concatenate(ys, axis=0)

    op_ref[...] = y
    if split_out:
        @pl.when(is_sample)
        def _():
            os_ref[...] = op_ref[...]


def _ffn(x, g, w_in, w_out, n_prompt_rows, final_gain=None, next_weights=None, cache_t=None, state_t=None, *,
         split_out=False, tm):
    split_in = isinstance(x, tuple)
    n_rows = sum(a.shape[0] for a in x) if split_in else x.shape[0]
    n_p, n_s = n_prompt_rows // tm, (n_rows - n_prompt_rows) // tm
    final_norm = final_gain is not None
    convert_next = next_weights is not None
    relayout = cache_t is not None
    restate = state_t is not None

    def sample_tile(i):
        return (jnp.minimum(i, n_s - 1), 0)

    def prompt_tile(i):
        return (jnp.maximum(i - n_s, 0), 0)

    def sample_first_tile(i):
        return (jnp.where(i < n_s, n_p + i, i - n_s), 0)

    def same_tile(i):
        return (i, 0)

    reordered = split_in or split_out
    combined_tile = sample_first_tile if reordered else same_tile
    if split_in:
        in_specs = [pl.BlockSpec((tm, D_MODEL), prompt_tile), pl.BlockSpec((tm, D_MODEL), sample_tile)]
        args = list(x)
    else:
        in_specs = [pl.BlockSpec((tm, D_MODEL), combined_tile)]
        args = [x]
    in_specs += [_resident((1, D_MODEL)), _resident((D_MODEL, 2 * D_FF)), _resident((D_FF, D_MODEL))]
    args += [g.reshape(1, D_MODEL), w_in, w_out]
    if split_out:
        out_shape = [jax.ShapeDtypeStruct((n_p * tm, D_MODEL), F32), jax.ShapeDtypeStruct((n_s * tm, D_MODEL), F32)]
        out_specs = [pl.BlockSpec((tm, D_MODEL), prompt_tile), pl.BlockSpec((tm, D_MODEL), sample_tile)]
    else:
        out_shape = [jax.ShapeDtypeStruct((n_rows, D_MODEL), F32)]
        out_specs = [pl.BlockSpec((tm, D_MODEL), combined_tile)]
    if final_norm:
        in_specs.append(_resident((1, D_MODEL)))
        args.append(final_gain.reshape(1, D_MODEL))
    if convert_next:
        nw_in, nw_out, layer = next_weights
        rows_in, rows_out = D_MODEL // WEIGHT_CAST_STEPS, D_FF // WEIGHT_CAST_STEPS

        def slab(i):
            return (jnp.minimum(i, WEIGHT_CAST_STEPS - 1), 0)

        in_specs += [pl.BlockSpec((None, rows_in, 2 * D_FF), lambda i: (layer,) + slab(i)),
                     pl.BlockSpec((None, rows_out, D_MODEL), lambda i: (layer,) + slab(i))]
        args += [nw_in, nw_out]
        out_shape += [jax.ShapeDtypeStruct((D_MODEL, 2 * D_FF), BF16), jax.ShapeDtypeStruct((D_FF, D_MODEL), BF16)]
        out_specs += [pl.BlockSpec((rows_in, 2 * D_FF), slab), pl.BlockSpec((rows_out, D_MODEL), slab)]
    if relayout:
        cache, cache_layer = cache_t
        n_seq = cache.shape[1]
        last_group = n_seq // RELAYOUT_SEQS - 1

        def seq_group(i):
            return (jnp.minimum(i, last_group), 0, 0)

        in_specs.append(pl.BlockSpec((None, RELAYOUT_SEQS, KV_DIM, WINDOW), lambda i: (cache_layer,) + seq_group(i)))
        args.append(cache)
        out_shape.append(jax.ShapeDtypeStruct((n_seq, WINDOW, KV_DIM), F32))
        out_specs.append(pl.BlockSpec((RELAYOUT_SEQS, WINDOW, KV_DIM), seq_group))
    if restate:
        state, state_layer = state_t
        n_seq = state.shape[2]
        last_state_group = n_seq // RESTATE_SEQS - 1

        def state_group(i):
            return jnp.minimum(i, last_state_group)

        in_specs.append(pl.BlockSpec((None, POOL_PREFIX, RESTATE_SEQS, D_MODEL),
                                     lambda i: (state_layer, 0, state_group(i), 0)))
        args.append(state)
        out_shape.append(jax.ShapeDtypeStruct((n_seq, POOL_PREFIX, D_MODEL), F32))
        out_specs.append(pl.BlockSpec((RESTATE_SEQS, POOL_PREFIX, D_MODEL), lambda i: (state_group(i), 0, 0)))
    return pl.pallas_call(
        functools.partial(_ffn_kernel, n_sample_tiles=n_s, split_in=split_in, split_out=split_out,
                          final_norm=final_norm, convert_next=convert_next, relayout=relayout, restate=restate),
        out_shape=tuple(out_shape),
        grid=(n_p + n_s,),
        in_specs=in_specs,
        out_specs=tuple(out_specs),
        scratch_shapes=[pltpu.VMEM((tm, D_FF), BF16)],
        compiler_params=pltpu.CompilerParams(
            dimension_semantics=("arbitrary",), vmem_limit_bytes=VMEM_LIMIT),
        name="ffn_final" if final_norm else "ffn",
    )(*args)


def _pool_kernel(x_ref, *rest, has_prefix, start_pos, bb, tt):
    rest = list(rest)
    pre_ref = rest.pop(0) if has_prefix else None
    g_ref, w_ref, sc_ref, o_ref, st_ref, ext_ref = rest
    ti = pl.program_id(1)
    te = tt + POOL_HALO

    @pl.when(ti == 0)
    def _():
        ext_ref[:, 0:POOL_HALO, :] = jnp.zeros((bb, POOL_HALO, D_MODEL), F32)
        if has_prefix:
            ext_ref[:, POOL_HALO - POOL_PREFIX:POOL_HALO, :] = pre_ref[...]

    if tt >= POOL_HALO:
        @pl.when(ti > 0)
        def _():
            ext_ref[:, 0:POOL_HALO, :] = ext_ref[:, tt:te, :]

    x = x_ref[...].reshape(bb, tt, D_MODEL)
    u = _rms(x, g_ref[...])
    ext_ref[:, POOL_HALO:te, :] = u

    n_seen = start_pos + ti * tt + lax.broadcasted_iota(jnp.int32, (1, tt, 1), 1) + 1
    for gi, wg in enumerate(POOL_WINDOWS):
        sl = slice(gi * POOL_GROUP_DIM, (gi + 1) * POOL_GROUP_DIM)
        s = ext_ref[:, :, sl].reshape(bb * te, POOL_GROUP_DIM)
        span = 1
        while span < wg:
            s = s + pltpu.roll(s, span, axis=0)
            span *= 2
        s = s.reshape(bb, te, POOL_GROUP_DIM)[:, POOL_HALO:, :]
        cnt = jnp.minimum(n_seen, wg).astype(F32)
        p = s / cnt - u[:, :, sl]
        p2 = p.reshape(bb * tt, POOL_GROUP_DIM).astype(BF16)
        y = jnp.dot(p2, w_ref[gi], preferred_element_type=F32).reshape(bb, tt, POOL_GROUP_DIM)
        o_ref[:, sl] = (x[:, :, sl] + y * sc_ref[:, :, sl]).reshape(bb * tt, POOL_GROUP_DIM)

    @pl.when(ti == pl.num_programs(1) - 1)
    def _():
        st_ref[...] = ext_ref[:, te - POOL_PREFIX:te, :]


def _row_blocks(row0, n_seq, t, bb, tt):
    first, per_seq = row0 // (bb * tt), t // tt
    return lambda i, j: (first + i * per_seq + j, 0)


def _pool(x, prefix, g, w, scale, layer, *, row0, n_seq, t, start_pos, bb, tt):
    has_prefix = prefix is not None
    rows = _row_blocks(row0, n_seq, t, bb, tt)
    in_specs = [pl.BlockSpec((bb * tt, D_MODEL), rows)]
    args = [x]
    if has_prefix:
        in_specs.append(pl.BlockSpec((bb, POOL_PREFIX, D_MODEL), lambda i, j: (i, 0, 0)))
        args.append(prefix)
    in_specs += [
        _resident((1, 1, D_MODEL)),
        _resident_layer((len(POOL_WINDOWS), POOL_GROUP_DIM, POOL_GROUP_DIM), layer),
        _resident((1, 1, D_MODEL)),
    ]
    args += [g.reshape(1, 1, D_MODEL), w, scale.reshape(1, 1, D_MODEL)]
    return pl.pallas_call(
        functools.partial(_pool_kernel, has_prefix=has_prefix, start_pos=start_pos, bb=bb, tt=tt),
        out_shape=(jax.ShapeDtypeStruct(x.shape, F32),
                   jax.ShapeDtypeStruct((n_seq, POOL_PREFIX, D_MODEL), F32)),
        grid=(n_seq // bb, t // tt),
        in_specs=in_specs,
        out_specs=(pl.BlockSpec((bb * tt, D_MODEL), rows),
                   pl.BlockSpec((bb, POOL_PREFIX, D_MODEL), lambda i, j: (i, 0, 0))),
        scratch_shapes=[pltpu.VMEM((bb, tt + POOL_HALO, D_MODEL), F32)],
        input_output_aliases={0: 0},
        compiler_params=pltpu.CompilerParams(
            dimension_semantics=("arbitrary", "arbitrary"), vmem_limit_bytes=VMEM_LIMIT),
        name="pool_sample" if has_prefix else "pool_prompt",
    )(*args)


def _rope_tables(pos):
    half = HEAD_DIM // 2
    inv = ROPE_THETA ** (-jnp.arange(half, dtype=F32) / half)
    ang = pos.astype(F32)[:, None] * inv[None, :]
    cos, sin = jnp.cos(ang), jnp.sin(ang)
    cos64 = jnp.concatenate([cos, cos], axis=-1)
    sin64 = jnp.concatenate([-sin, sin], axis=-1)
    return jnp.tile(cos64, (1, HEADS_PER_VREG)), jnp.tile(sin64, (1, HEADS_PER_VREG))


def _rope_col(xc, cos, sin, first_half):
    half = HEAD_DIM // 2
    partner = jnp.where(first_half, pltpu.roll(xc, LANES - half, axis=1), pltpu.roll(xc, half, axis=1))
    return xc * cos + partner * sin


def _head_half_variants(col, lo_half):
    sw = pltpu.roll(col, HEAD_DIM, axis=1)
    zero = jnp.zeros_like(col)
    return {
        (0, 0): jnp.where(lo_half, col, zero),
        (0, 1): jnp.where(lo_half, zero, sw),
        (1, 0): jnp.where(lo_half, sw, zero),
        (1, 1): jnp.where(lo_half, zero, col),
    }


def _softmax_with_sink(s, sink):
    m = jnp.maximum(jnp.max(s, axis=-1, keepdims=True), sink)
    e = jnp.exp2(s - m)
    denom = jnp.sum(e, axis=-1, keepdims=True) + jnp.exp2(sink - m)
    return e.astype(BF16), 1.0 / denom


def _attn_prompt_kernel(x_ref, cos_ref, sin_ref, g_ref, wqkv_ref, bqkv_ref, wo_ref, bo_ref, sink_ref,
                        o_ref, kout_ref, vout_ref, kt_ref, vlo_ref, vhi_ref, *, nblk):
    j = pl.program_id(1)
    w = WINDOW
    r = nblk * w

    @pl.when(j == 0)
    def _():
        kt_ref[...] = jnp.zeros(kt_ref.shape, BF16)
        vlo_ref[:, 0:w, 0:LANES] = jnp.zeros((N_KV_HEADS, w, LANES), BF16)
        vhi_ref[:, 0:w, 0:LANES] = jnp.zeros((N_KV_HEADS, w, LANES), BF16)
        keys = vlo_ref.shape[1]
        ones_lo = (lax.broadcasted_iota(jnp.int32, (N_KV_HEADS, keys, LANES), 2) < HEAD_DIM).astype(BF16)
        vlo_ref[:, :, LANES:] = ones_lo
        vhi_ref[:, :, LANES:] = 1 - ones_lo

    @pl.when(j > 0)
    def _():
        kt_ref[:, :, 0:w] = kt_ref[:, :, r:r + w]
        vlo_ref[:, 0:w, 0:LANES] = vlo_ref[:, r:r + w, 0:LANES]
        vhi_ref[:, 0:w, 0:LANES] = vhi_ref[:, r:r + w, 0:LANES]

    rc = ATTN_PROJ_BLOCKS * w
    n_chunks = nblk // ATTN_PROJ_BLOCKS
    lane = lax.broadcasted_iota(jnp.int32, (rc, LANES), 1)
    first_half = (lane % HEAD_DIM) < (HEAD_DIM // 2)
    lo_half = lane < HEAD_DIM
    lo_half_2w = lax.broadcasted_iota(jnp.int32, (2 * w, LANES), 1) < HEAD_DIM

    u_of, qcols_of = {}, {}

    def project_piece(ch, t):
        rows = slice(ch * rc, (ch + 1) * rc)
        if t == 0:
            u_of[ch] = _rms(x_ref[rows, :], g_ref[...]).astype(BF16)
            qcols_of[ch] = []
        cols = slice(t * 2 * LANES, (t + 1) * 2 * LANES)
        tile = jnp.dot(u_of[ch], wqkv_ref[:, cols], preferred_element_type=F32) + bqkv_ref[:, cols]
        halves = [tile[:, :LANES], tile[:, LANES:]]
        if t < Q_COLS // 2:
            cos, sin = cos_ref[rows, :], sin_ref[rows, :]
            for hc in halves:
                qcols_of[ch].append((_rope_col(hc, cos, sin, first_half) * QK_SCALE_LOG2).astype(BF16))
            return
        is_k = t == Q_COLS // 2
        if is_k:
            cos, sin = cos_ref[rows, :], sin_ref[rows, :]
            halves = [_rope_col(hc, cos, sin, first_half) for hc in halves]
        krows = slice(w + ch * rc, w + (ch + 1) * rc)
        for c, col in enumerate(halves):
            if ch == n_chunks - 1:
                out_ref = kout_ref if is_k else vout_ref
                out_ref[0, :, c * LANES:(c + 1) * LANES] = col[rc - w:]
            if is_k:
                col_t = col.T.astype(BF16)
                for hb in range(HEADS_PER_VREG):
                    kvh = c * HEADS_PER_VREG + hb
                    head_t = col_t[hb * HEAD_DIM:(hb + 1) * HEAD_DIM, :]
                    for a in range(HEADS_PER_VREG):
                        kt_ref[kvh * HEADS_PER_VREG + a, a * HEAD_DIM:(a + 1) * HEAD_DIM, krows] = head_t
            else:
                var = _head_half_variants(col, lo_half)
                for hb in range(HEADS_PER_VREG):
                    kvh = c * HEADS_PER_VREG + hb
                    vlo_ref[kvh, krows, 0:LANES] = var[(hb, 0)].astype(BF16)
                    vhi_ref[kvh, krows, 0:LANES] = var[(hb, 1)].astype(BF16)

    for t in range(PROJ_PIECES):
        project_piece(0, t)

    row = lax.broadcasted_iota(jnp.int32, (2 * w, 2 * w), 0) % w
    col = lax.broadcasted_iota(jnp.int32, (2 * w, 2 * w), 1)
    band = (col > row) & (col <= row + w)
    first_key = jnp.where(j > 0, 0, w)
    band_first = band & (col >= first_key)
    top_rows = lax.broadcasted_iota(jnp.int32, (2 * w, 1), 0) < w

    groups = [(ch, kvh, bl, a) for ch in range(n_chunks) for kvh in range(N_KV_HEADS)
              for bl in range(ATTN_PROJ_BLOCKS) for a in range(HEADS_PER_VREG)]

    def scores(ch, kvh, bl, a):
        qcols = qcols_of[ch]
        qrows = slice(bl * w, (bl + 1) * w)
        blk = ch * ATTN_PROJ_BLOCKS + bl
        qop = jnp.concatenate([qcols[2 * kvh][qrows], qcols[2 * kvh + 1][qrows]], axis=0)
        return jnp.dot(qop, kt_ref[kvh * HEADS_PER_VREG + a, :, blk * w:(blk + 2) * w],
                       preferred_element_type=F32)

    groups_per_chunk = len(groups) // n_chunks
    pending = [scores(*g) for g in groups[:QK_LOOKAHEAD]]
    es, sink_terms, o_parts, acc = [], [], [], None
    for gi, (ch, kvh, bl, a) in enumerate(groups):
        gl = gi % groups_per_chunk
        if 0 <= gl - PROJ_AT_GROUP < PROJ_PIECES and ch + 1 < n_chunks:
            project_piece(ch + 1, gl - PROJ_AT_GROUP)
        if gi + QK_LOOKAHEAD < len(groups):
            pending.append(scores(*groups[gi + QK_LOOKAHEAD]))
        blk = ch * ATTN_PROJ_BLOCKS + bl
        s = jnp.where(band_first if blk == 0 else band, pending[gi], NEG_INF)
        pending[gi] = None
        h0 = kvh * GROUP + a
        sink = jnp.where(top_rows, sink_ref[h0] * LOG2_E, sink_ref[h0 + HEADS_PER_VREG] * LOG2_E)
        m = jnp.maximum(jnp.max(s, axis=-1, keepdims=True), sink)
        es.append(jnp.exp2(s - m).astype(BF16))
        sink_terms.append(jnp.exp2(sink - m))
        if a < HEADS_PER_VREG - 1:
            continue
        krows = slice(blk * w, (blk + 2) * w)
        vstack = jnp.concatenate([vlo_ref[kvh, krows, :], vhi_ref[kvh, krows, :]], axis=0)
        res = jnp.dot(jnp.concatenate(es, axis=1), vstack, preferred_element_type=F32)
        den = res[:, LANES:] + jnp.where(lo_half_2w, sink_terms[0], sink_terms[1])
        o = res[:, :LANES] * (1.0 / den)
        o_parts.append(jnp.concatenate([o[:w], o[w:]], axis=1).astype(BF16))
        es, sink_terms = [], []
        if bl < ATTN_PROJ_BLOCKS - 1:
            continue
        part = jnp.dot(jnp.concatenate(o_parts, axis=0),
                       wo_ref[kvh * GROUP * HEAD_DIM:(kvh + 1) * GROUP * HEAD_DIM, :],
                       preferred_element_type=F32)
        o_parts = []
        acc = part if acc is None else acc + part
        if kvh == N_KV_HEADS - 1:
            rows = slice(ch * rc, (ch + 1) * rc)
            o_ref[rows, :] = x_ref[rows, :] + acc + bo_ref[...]
            acc = None


def _attn_prompt(x, g, w_qkv, b_qkv, w_o, b_o, sinks, layer, *, n_seq, t, nblk):
    b = n_seq
    r = nblk * WINDOW
    cos, sin = _rope_tables(jnp.arange(t))
    keys = (nblk + 1) * WINDOW
    rows = _row_blocks(0, n_seq, t, 1, r)
    return pl.pallas_call(
        functools.partial(_attn_prompt_kernel, nblk=nblk),
        out_shape=(jax.ShapeDtypeStruct(x.shape, F32),
                   jax.ShapeDtypeStruct((b, WINDOW, KV_DIM), F32),
                   jax.ShapeDtypeStruct((b, WINDOW, KV_DIM), F32)),
        grid=(b, t // r),
        in_specs=[
            pl.BlockSpec((r, D_MODEL), rows),
            pl.BlockSpec((r, LANES), lambda i, j: (j, 0)),
            pl.BlockSpec((r, LANES), lambda i, j: (j, 0)),
            _resident((1, D_MODEL)),
            _resident_layer((D_MODEL, QKV_DIM), layer),
            _resident((1, QKV_DIM)),
            _resident_layer((Q_DIM, D_MODEL), layer),
            _resident((1, D_MODEL)),
            pl.BlockSpec(memory_space=pltpu.SMEM),
        ],
        out_specs=(pl.BlockSpec((r, D_MODEL), rows),
                   pl.BlockSpec((1, WINDOW, KV_DIM), lambda i, j: (i, 0, 0)),
                   pl.BlockSpec((1, WINDOW, KV_DIM), lambda i, j: (i, 0, 0))),
        scratch_shapes=[pltpu.VMEM((2 * N_KV_HEADS, LANES, keys), BF16),
                        pltpu.VMEM((N_KV_HEADS, keys, 2 * LANES), BF16),
                        pltpu.VMEM((N_KV_HEADS, keys, 2 * LANES), BF16)],
        input_output_aliases={0: 0},
        compiler_params=pltpu.CompilerParams(
            dimension_semantics=("arbitrary", "arbitrary"), vmem_limit_bytes=VMEM_LIMIT),
        name="attn_prompt",
    )(x, cos, sin, g.reshape(1, D_MODEL), w_qkv, b_qkv.reshape(1, QKV_DIM), w_o,
      b_o.reshape(1, D_MODEL), sinks)


SAMPLE_KEYS = 2 * WINDOW
SAMPLE_UNROLL = 4


def _attn_sample_kernel(x_ref, ck_ref, cv_ref, cos_ref, sin_ref, g_ref, wqkv_ref, bqkv_ref, wo_ref, bo_ref,
                        sinkcol_ref, o_ref, kout_ref, vout_ref, q_s, kn_s, vn_s, kk_s, vv_s, oa_s, *, dt):
    bb = ck_ref.shape[0]
    w = WINDOW
    x = x_ref[...]
    u = _rms(x, g_ref[...]).astype(BF16)
    qkv = jnp.dot(u, wqkv_ref[...], preferred_element_type=F32) + bqkv_ref[...]
    cos, sin = cos_ref[...], sin_ref[...]
    lane_t = lax.broadcasted_iota(jnp.int32, (bb * dt, LANES), 1)
    first_half = (lane_t % HEAD_DIM) < (HEAD_DIM // 2)
    for c in range(Q_COLS):
        qc = _rope_col(qkv[:, c * LANES:(c + 1) * LANES], cos, sin, first_half)
        q_s[:, c * LANES:(c + 1) * LANES] = qc * QK_SCALE_LOG2
    for c in range(KV_COLS):
        kn_s[:, c * LANES:(c + 1) * LANES] = _rope_col(
            qkv[:, Q_DIM + c * LANES:Q_DIM + (c + 1) * LANES], cos, sin, first_half)
    vn_s[...] = qkv[:, Q_DIM + KV_DIM:]

    kk_s[:, w + dt:, :] = jnp.zeros((SAMPLE_UNROLL, SAMPLE_KEYS - w - dt, KV_DIM), BF16)
    vv_s[:, w + dt:, :] = jnp.zeros((SAMPLE_UNROLL, SAMPLE_KEYS - w - dt, KV_DIM), BF16)

    heads_per_col = N_HEADS // KV_COLS
    qcols_per_col = heads_per_col // HEADS_PER_VREG
    rows = heads_per_col * dt
    lane8 = lax.broadcasted_iota(jnp.int32, (dt, LANES), 1)
    lo8 = lane8 < HEAD_DIM
    t_row = lax.broadcasted_iota(jnp.int32, (rows, SAMPLE_KEYS), 0) % dt
    key = lax.broadcasted_iota(jnp.int32, (rows, SAMPLE_KEYS), 1)
    valid = (key > t_row) & (key <= t_row + w)

    def stage(b, slot):
        r0 = pl.multiple_of(b * dt, dt)
        knew = kn_s[pl.ds(r0, dt), :]
        vnew = vn_s[pl.ds(r0, dt), :]
        kout_ref[b, 0:w - dt, :] = ck_ref[b, dt:w, :]
        kout_ref[b, w - dt:w, :] = knew
        vout_ref[b, 0:w - dt, :] = cv_ref[b, dt:w, :]
        vout_ref[b, w - dt:w, :] = vnew
        kk_s[slot, 0:w, :] = ck_ref[b].astype(BF16)
        kk_s[slot, w:w + dt, :] = knew.astype(BF16)
        vv_s[slot, 0:w, :] = cv_ref[b].astype(BF16)
        vv_s[slot, w:w + dt, :] = vnew.astype(BF16)

    def scores(b, slot, c):
        r0 = pl.multiple_of(b * dt, dt)
        pieces = []
        for h8 in range(heads_per_col):
            qi = c * qcols_per_col + h8 // HEADS_PER_VREG
            qcol = q_s[pl.ds(r0, dt), qi * LANES:(qi + 1) * LANES]
            a, hb = h8 % HEADS_PER_VREG, h8 // GROUP
            src = qcol if a == hb else pltpu.roll(qcol, HEAD_DIM, axis=1)
            keep = lo8 if hb == 0 else jnp.logical_not(lo8)
            pieces.append(jnp.where(keep, src, 0.0))
        qop = jnp.concatenate(pieces, axis=0).astype(BF16)
        return lax.dot_general(qop, kk_s[slot, :, c * LANES:(c + 1) * LANES],
                               (((1,), (1,)), ((), ())), preferred_element_type=F32)

    def place(b, c, o):
        r0 = pl.multiple_of(b * dt, dt)
        for mm in range(qcols_per_col):
            halves = []
            for a in range(HEADS_PER_VREG):
                h8 = mm * HEADS_PER_VREG + a
                hb = h8 // GROUP
                piece = o[h8 * dt:(h8 + 1) * dt, :]
                halves.append(piece if a == hb else pltpu.roll(piece, HEAD_DIM, axis=1))
            oi = c * qcols_per_col + mm
            oa_s[pl.ds(r0, dt), oi * LANES:(oi + 1) * LANES] = jnp.where(lo8, halves[0], halves[1])

    def body(gidx, carry):
        work = [(gidx * SAMPLE_UNROLL + slot, slot, c) for slot in range(SAMPLE_UNROLL) for c in range(KV_COLS)]
        for slot in range(SAMPLE_UNROLL):
            stage(gidx * SAMPLE_UNROLL + slot, slot)
        ss = [scores(b, slot, c) for b, slot, c in work]
        sm = [_softmax_with_sink(jnp.where(valid, s, NEG_INF), sinkcol_ref[c * rows:(c + 1) * rows, :] * LOG2_E)
              for s, (b, slot, c) in zip(ss, work)]
        for (e, rinv), (b, slot, c) in zip(sm, work):
            o = jnp.dot(e, vv_s[slot, :, c * LANES:(c + 1) * LANES], preferred_element_type=F32) * rinv
            place(b, c, o)
        return carry

    lax.fori_loop(0, bb // SAMPLE_UNROLL, body, 0)

    o_ref[...] = x + jnp.dot(oa_s[...].astype(BF16), wo_ref[...], preferred_element_type=F32) + bo_ref[...]


def _attn_sample(x, cache_k, cache_v, g, w_qkv, b_qkv, w_o, b_o, sinks, layer, *, row0, dt, bb):
    b = cache_k.shape[0]
    rows = _row_blocks(row0, b, dt, bb, dt)
    rows1 = lambda i: rows(i, 0)
    cos, sin = _rope_tables(PAST_LEN + jnp.arange(dt))
    cos, sin = jnp.tile(cos, (bb, 1)), jnp.tile(sin, (bb, 1))
    sinkcol = jnp.repeat(sinks, dt).reshape(N_HEADS * dt, 1)
    n = bb * dt
    return pl.pallas_call(
        functools.partial(_attn_sample_kernel, dt=dt),
        out_shape=(jax.ShapeDtypeStruct(x.shape, F32),
                   jax.ShapeDtypeStruct((b, WINDOW, KV_DIM), F32),
                   jax.ShapeDtypeStruct((b, WINDOW, KV_DIM), F32)),
        grid=(b // bb,),
        in_specs=[
            pl.BlockSpec((n, D_MODEL), rows1),
            pl.BlockSpec((bb, WINDOW, KV_DIM), lambda i: (i, 0, 0)),
            pl.BlockSpec((bb, WINDOW, KV_DIM), lambda i: (i, 0, 0)),
            _resident((n, LANES)),
            _resident((n, LANES)),
            _resident((1, D_MODEL)),
            _resident_layer((D_MODEL, QKV_DIM), layer),
            _resident((1, QKV_DIM)),
            _resident_layer((Q_DIM, D_MODEL), layer),
            _resident((1, D_MODEL)),
            _resident((N_HEADS * dt, 1)),
        ],
        out_specs=(pl.BlockSpec((n, D_MODEL), rows1),
                   pl.BlockSpec((bb, WINDOW, KV_DIM), lambda i: (i, 0, 0)),
                   pl.BlockSpec((bb, WINDOW, KV_DIM), lambda i: (i, 0, 0))),
        scratch_shapes=[pltpu.VMEM((n, Q_DIM), F32),
                        pltpu.VMEM((n, KV_DIM), F32),
                        pltpu.VMEM((n, KV_DIM), F32),
                        pltpu.VMEM((SAMPLE_UNROLL, SAMPLE_KEYS, KV_DIM), BF16),
                        pltpu.VMEM((SAMPLE_UNROLL, SAMPLE_KEYS, KV_DIM), BF16),
                        pltpu.VMEM((n, Q_DIM), F32)],
        input_output_aliases={0: 0},
        compiler_params=pltpu.CompilerParams(
            dimension_semantics=("arbitrary",), vmem_limit_bytes=VMEM_LIMIT),
        name="attn_sample",
    )(x, cache_k, cache_v, cos, sin, g.reshape(1, D_MODEL), w_qkv, b_qkv.reshape(1, QKV_DIM), w_o,
      b_o.reshape(1, D_MODEL), sinkcol)


def kernel(x_prompt, x_sample, state_pool, cache_k, cache_v, norm_ffn1, ffn1_w_in, ffn1_w_out, norm_mix,
           norm_ffn2, ffn2_w_in, ffn2_w_out, pool_w, pool_scale, attn_w_qkv, attn_b_qkv, attn_w_o, attn_b_o,
           attn_sinks, norm_final):
    batch, seq, _ = x_prompt.shape
    dec_batch, dec_seq, _ = x_sample.shape
    depth = norm_ffn1.shape[0]
    n_mixers = 2
    w_in, w_out = ffn1_w_in[0].astype(BF16), ffn1_w_out[0].astype(BF16)
    pw = pool_w.astype(BF16)
    wqkv, wo = attn_w_qkv.astype(BF16), attn_w_o.astype(BF16)
    n_attn = cache_k.shape[0]
    ck_t = jnp.transpose(cache_k, (0, 1, 3, 4, 2)).reshape(n_attn, dec_batch, KV_DIM, WINDOW)
    cv_t = jnp.transpose(cache_v, (0, 1, 3, 4, 2)).reshape(n_attn, dec_batch, KV_DIM, WINDOW)
    ck = cv = None
    state_t = jnp.swapaxes(state_pool, 1, 2)

    n_p, n_s = batch * seq, dec_batch * dec_seq
    x = (x_prompt.reshape(n_p, D_MODEL), x_sample.reshape(n_s, D_MODEL))
    pool_p, pool_s, kp_l, vp_l, ks_l, vs_l = [], [], [], [], [], []
    for i in range(depth):
        j = i // n_mixers
        is_attn = i % n_mixers == 1
        if is_attn:
            x, w_in, w_out, cv = _ffn(x, norm_ffn1[i], w_in, w_out, n_p,
                                      next_weights=(ffn2_w_in, ffn2_w_out, i), cache_t=(cv_t, j), tm=FFN_TILE)
        else:
            x, w_in, w_out, state = _ffn(x, norm_ffn1[i], w_in, w_out, n_p,
                                         next_weights=(ffn2_w_in, ffn2_w_out, i), state_t=(state_t, j), tm=FFN_TILE)
        if not is_attn:
            x, sp = _pool(x, None, norm_mix[i], pw, pool_scale[j], j, row0=0, n_seq=batch, t=seq,
                          start_pos=0, bb=1, tt=1024)
            x, ss = _pool(x, state, norm_mix[i], pw, pool_scale[j], j, row0=n_p, n_seq=dec_batch, t=dec_seq,
                          start_pos=PAST_LEN, bb=32, tt=dec_seq)
            pool_p.append(sp)
            pool_s.append(ss)
        else:
            x, kp, vp = _attn_prompt(x, norm_mix[i], wqkv, attn_b_qkv[j], wo, attn_b_o[j], attn_sinks[j], j,
                                     n_seq=batch, t=seq, nblk=ATTN_BLOCKS_PER_STEP)
            if ck is None:
                ck = jnp.swapaxes(ck_t[j], 1, 2)
            x, kn, vn = _attn_sample(x, ck, cv, norm_mix[i], wqkv, attn_b_qkv[j], wo, attn_b_o[j],
                                     attn_sinks[j], j, row0=n_p, dt=dec_seq, bb=32)
            ck = None
            kp_l.append(kp.reshape(batch, WINDOW, N_KV_HEADS, HEAD_DIM))
            vp_l.append(vp.reshape(batch, WINDOW, N_KV_HEADS, HEAD_DIM))
            ks_l.append(kn.reshape(dec_batch, WINDOW, N_KV_HEADS, HEAD_DIM))
            vs_l.append(vn.reshape(dec_batch, WINDOW, N_KV_HEADS, HEAD_DIM))
        if i == depth - 1:
            xp, xs = _ffn(x, norm_ffn2[i], w_in, w_out, n_p, final_gain=norm_final, split_out=True, tm=FFN_TILE)
        elif (i + 1) % n_mixers == 1:
            x, w_in, w_out, ck = _ffn(x, norm_ffn2[i], w_in, w_out, n_p,
                                      next_weights=(ffn1_w_in, ffn1_w_out, i + 1),
                                      cache_t=(ck_t, (i + 1) // n_mixers), tm=FFN_TILE)
        else:
            x, w_in, w_out = _ffn(x, norm_ffn2[i], w_in, w_out, n_p,
                                  next_weights=(ffn1_w_in, ffn1_w_out, i + 1), tm=FFN_TILE)
    return (xp.reshape(batch, seq, D_MODEL), xs.reshape(dec_batch, dec_seq, D_MODEL),
            jnp.stack(pool_p), jnp.stack(pool_s), jnp.stack(kp_l), jnp.stack(vp_l),
            jnp.stack(ks_l), jnp.stack(vs_l))
```

```python
import functools

import jax
import jax.numpy as jnp
from jax import lax
from jax.experimental import pallas as pl
from jax.experimental.pallas import tpu as pltpu

D_MODEL = 1024
D_FF = 2816
POOL_WINDOWS = (2, 4, 8, 16)
POOL_GROUP_DIM = D_MODEL // len(POOL_WINDOWS)
POOL_PREFIX = max(POOL_WINDOWS) - 1
N_HEADS = 16
N_KV_HEADS = 4
HEAD_DIM = 64
GROUP = N_HEADS // N_KV_HEADS
WINDOW = 128
ROPE_THETA = 10000.0
Q_DIM = N_HEADS * HEAD_DIM
KV_DIM = N_KV_HEADS * HEAD_DIM
QKV_DIM = Q_DIM + 2 * KV_DIM
RMS_EPS = 1e-6
NEG_INF = -1e30
PAST_LEN = 8192

LANES = 128
SUBLANES = 8
HEADS_PER_VREG = LANES // HEAD_DIM
Q_COLS = Q_DIM // LANES
KV_COLS = KV_DIM // LANES
POOL_HALO = 16
FF_CHUNK = 256
FFN_TILE = 512
FFN_SUB_ROWS = 256
RELAYOUT_SEQS = 4
RESTATE_SEQS = 8
WEIGHT_CAST_STEPS = 16
ATTN_BLOCKS_PER_STEP = 8
ATTN_PROJ_BLOCKS = 2
PROJ_PIECES = QKV_DIM // (2 * LANES)
PROJ_AT_GROUP = 3
QK_LOOKAHEAD = 3
LOG2_E = 1.4426950408889634
QK_SCALE_LOG2 = HEAD_DIM ** -0.5 * LOG2_E
VMEM_LIMIT = 52 * 1024 * 1024

F32 = jnp.float32
BF16 = jnp.bfloat16


def _rms(x, g):
    ms = jnp.mean(x * x, axis=-1, keepdims=True)
    return x * lax.rsqrt(ms + RMS_EPS) * g


def _resident(shape):
    nd = len(shape)
    return pl.BlockSpec(shape, lambda *_: (0,) * nd, pipeline_mode=pl.Buffered(1))


def _resident_layer(shape, layer):
    nd = len(shape)
    return pl.BlockSpec((None,) + shape, lambda *_: (layer,) + (0,) * nd, pipeline_mode=pl.Buffered(1))


def _zero_bits_of(v):
    bits = lax.bitcast_convert_type(v, jnp.uint32)
    return (bits >> 16) >> 16


def _after(v, zero_bits):
    reps = (v.shape[0] // zero_bits.shape[0], v.shape[1] // zero_bits.shape[1])
    bits = lax.bitcast_convert_type(v, jnp.uint32) | jnp.tile(zero_bits, reps)
    return lax.bitcast_convert_type(bits, F32)


def _ffn_kernel(*refs, n_sample_tiles, split_in, split_out, final_norm, convert_next, relayout, restate,
                raw_layer):
    refs = list(refs)
    if split_in:
        xp_ref, xs_ref = refs.pop(0), refs.pop(0)
    else:
        x_ref = refs.pop(0)
    g_ref, win_ref, wout_ref = refs[:3]
    del refs[:3]
    gf_ref = refs.pop(0) if final_norm else None
    if convert_next:
        nwin_ref, nwout_ref = refs.pop(0), refs.pop(0)
    if relayout:
        ri_ref = refs.pop(0)
    if restate:
        si_ref = refs.pop(0)
    if split_out:
        op_ref, os_ref = refs.pop(0), refs.pop(0)
    else:
        op_ref = refs.pop(0)
    if convert_next:
        cwin_ref, cwout_ref = refs.pop(0), refs.pop(0)
    if relayout:
        ro_ref = refs.pop(0)
    if restate:
        so_ref = refs.pop(0)
    h_ref = refs.pop(0)
    if raw_layer is not None:
        win_hbm, wout_hbm = win_ref, wout_ref
        win_ref, wout_ref, stage_in, stage_out, sem = refs
        slab_cols = stage_in.shape[2]
        slab_rows = stage_out.shape[1]
        n_slabs = 2 * D_FF // slab_cols

        def in_copy(c):
            return pltpu.make_async_copy(win_hbm.at[raw_layer, :, pl.ds(c * slab_cols, slab_cols)],
                                         stage_in.at[c % 2], sem.at[0, c % 2])

        def out_copy(c):
            return pltpu.make_async_copy(wout_hbm.at[raw_layer, pl.ds(c * slab_rows, slab_rows), :],
                                         stage_out.at[c % 2], sem.at[1, c % 2])

        @pl.when(pl.program_id(0) == 0)
        def _():
            in_copy(0).start()
            out_copy(0).start()
            for c in range(n_slabs):
                if c + 1 < n_slabs:
                    in_copy(c + 1).start()
                    out_copy(c + 1).start()
                in_copy(c).wait()
                win_ref[:, c * slab_cols:(c + 1) * slab_cols] = stage_in[c % 2].astype(BF16)
                out_copy(c).wait()
                wout_ref[c * slab_rows:(c + 1) * slab_rows, :] = stage_out[c % 2].astype(BF16)

    is_sample = pl.program_id(0) < n_sample_tiles
    x = jnp.where(is_sample, xs_ref[...], xp_ref[...]) if split_in else x_ref[...]
    xn = _rms(x, g_ref[...]).astype(BF16)
    tm = x.shape[0]
    n_chunks = D_FF // FF_CHUNK
    for c in range(n_chunks):
        lo = c * FF_CHUNK
        for r0 in range(0, tm, FFN_SUB_ROWS):
            xs_ = xn[r0:r0 + FFN_SUB_ROWS]
            gate = jnp.dot(xs_, win_ref[:, lo:lo + FF_CHUNK], preferred_element_type=F32)
            up = jnp.dot(xs_, win_ref[:, D_FF + lo:D_FF + lo + FF_CHUNK], preferred_element_type=F32)
            h_ref[r0:r0 + FFN_SUB_ROWS, lo:lo + FF_CHUNK] = (gate * jax.nn.sigmoid(gate) * up).astype(BF16)
        anchor = _zero_bits_of(gate[0:SUBLANES, 0:LANES])
        if convert_next:
            cw = 2 * D_FF // n_chunks
            cwin_ref[:, c * cw:(c + 1) * cw] = _after(nwin_ref[:, c * cw:(c + 1) * cw], anchor).astype(BF16)
            cr = cwout_ref.shape[0] // n_chunks
            cwout_ref[c * cr:(c + 1) * cr, :] = _after(nwout_ref[c * cr:(c + 1) * cr, :], anchor).astype(BF16)
        if relayout and c < ri_ref.shape[0]:
            ro_ref[c] = _after(ri_ref[c], anchor).T
        if restate:
            for p in range(c, POOL_PREFIX, n_chunks):
                so_ref[:, p, :] = _after(si_ref[p], anchor)
    ys = []
    for r0 in range(0, tm, FFN_SUB_ROWS):
        rows = slice(r0, r0 + FFN_SUB_ROWS)
        yr = x[rows] + 0.5 * jnp.dot(h_ref[rows, :], wout_ref[...], preferred_element_type=F32)
        ys.append(_rms(yr, gf_ref[...]) if final_norm else yr)
    y = jnp.concatenate(ys, axis=0)

    op_ref[...] = y
    if split_out:
        @pl.when(is_sample)
        def _():
            os_ref[...] = op_ref[...]


def _ffn(x, g, w_in, w_out, n_prompt_rows, final_gain=None, next_weights=None, cache_t=None, state_t=None, *,
         raw_layer=None, split_out=False, tm):
    split_in = isinstance(x, tuple)
    n_rows = sum(a.shape[0] for a in x) if split_in else x.shape[0]
    n_p, n_s = n_prompt_rows // tm, (n_rows - n_prompt_rows) // tm
    final_norm = final_gain is not None
    convert_next = next_weights is not None
    relayout = cache_t is not None
    restate = state_t is not None

    def sample_tile(i):
        return (jnp.minimum(i, n_s - 1), 0)

    def prompt_tile(i):
        return (jnp.maximum(i - n_s, 0), 0)

    def sample_first_tile(i):
        return (jnp.where(i < n_s, n_p + i, i - n_s), 0)

    def same_tile(i):
        return (i, 0)

    reordered = split_in or split_out
    combined_tile = sample_first_tile if reordered else same_tile
    if split_in:
        in_specs = [pl.BlockSpec((tm, D_MODEL), prompt_tile), pl.BlockSpec((tm, D_MODEL), sample_tile)]
        args = list(x)
    else:
        in_specs = [pl.BlockSpec((tm, D_MODEL), combined_tile)]
        args = [x]
    scratch_shapes = [pltpu.VMEM((tm, D_FF), BF16)]
    if raw_layer is None:
        in_specs += [_resident((1, D_MODEL)), _resident((D_MODEL, 2 * D_FF)), _resident((D_FF, D_MODEL))]
    else:
        in_specs += [_resident((1, D_MODEL)), pl.BlockSpec(memory_space=pl.ANY), pl.BlockSpec(memory_space=pl.ANY)]
        n_slabs = D_FF // FF_CHUNK
        scratch_shapes += [pltpu.VMEM((D_MODEL, 2 * D_FF), BF16), pltpu.VMEM((D_FF, D_MODEL), BF16),
                           pltpu.VMEM((2, D_MODEL, 2 * D_FF // n_slabs), F32),
                           pltpu.VMEM((2, D_FF // n_slabs, D_MODEL), F32),
                           pltpu.SemaphoreType.DMA((2, 2))]
    args += [g.reshape(1, D_MODEL), w_in, w_out]
    if split_out:
        out_shape = [jax.ShapeDtypeStruct((n_p * tm, D_MODEL), F32), jax.ShapeDtypeStruct((n_s * tm, D_MODEL), F32)]
        out_specs = [pl.BlockSpec((tm, D_MODEL), prompt_tile), pl.BlockSpec((tm, D_MODEL), sample_tile)]
    else:
        out_shape = [jax.ShapeDtypeStruct((n_rows, D_MODEL), F32)]
        out_specs = [pl.BlockSpec((tm, D_MODEL), combined_tile)]
    if final_norm:
        in_specs.append(_resident((1, D_MODEL)))
        args.append(final_gain.reshape(1, D_MODEL))
    if convert_next:
        nw_in, nw_out, layer = next_weights
        rows_in, rows_out = D_MODEL // WEIGHT_CAST_STEPS, D_FF // WEIGHT_CAST_STEPS

        def slab(i):
            return (jnp.minimum(i, WEIGHT_CAST_STEPS - 1), 0)

        in_specs += [pl.BlockSpec((None, rows_in, 2 * D_FF), lambda i: (layer,) + slab(i)),
                     pl.BlockSpec((None, rows_out, D_MODEL), lambda i: (layer,) + slab(i))]
        args += [nw_in, nw_out]
        out_shape += [jax.ShapeDtypeStruct((D_MODEL, 2 * D_FF), BF16), jax.ShapeDtypeStruct((D_FF, D_MODEL), BF16)]
        out_specs += [pl.BlockSpec((rows_in, 2 * D_FF), slab), pl.BlockSpec((rows_out, D_MODEL), slab)]
    if relayout:
        cache, cache_layer = cache_t
        n_seq = cache.shape[1]
        last_group = n_seq // RELAYOUT_SEQS - 1

        def seq_group(i):
            return (jnp.minimum(i, last_group), 0, 0)

        in_specs.append(pl.BlockSpec((None, RELAYOUT_SEQS, KV_DIM, WINDOW), lambda i: (cache_layer,) + seq_group(i)))
        args.append(cache)
        out_shape.append(jax.ShapeDtypeStruct((n_seq, WINDOW, KV_DIM), F32))
        out_specs.append(pl.BlockSpec((RELAYOUT_SEQS, WINDOW, KV_DIM), seq_group))
    if restate:
        state, state_layer = state_t
        n_seq = state.shape[2]
        last_state_group = n_seq // RESTATE_SEQS - 1

        def state_group(i):
            return jnp.minimum(i, last_state_group)

        in_specs.append(pl.BlockSpec((None, POOL_PREFIX, RESTATE_SEQS, D_MODEL),
                                     lambda i: (state_layer, 0, state_group(i), 0)))
        args.append(state)
        out_shape.append(jax.ShapeDtypeStruct((n_seq, POOL_PREFIX, D_MODEL), F32))
        out_specs.append(pl.BlockSpec((RESTATE_SEQS, POOL_PREFIX, D_MODEL), lambda i: (state_group(i), 0, 0)))
    return pl.pallas_call(
        functools.partial(_ffn_kernel, n_sample_tiles=n_s, split_in=split_in, split_out=split_out,
                          final_norm=final_norm, convert_next=convert_next, relayout=relayout, restate=restate,
                          raw_layer=raw_layer),
        out_shape=tuple(out_shape),
        grid=(n_p + n_s,),
        in_specs=in_specs,
        out_specs=tuple(out_specs),
        scratch_shapes=scratch_shapes,
        compiler_params=pltpu.CompilerParams(
            dimension_semantics=("arbitrary",), vmem_limit_bytes=VMEM_LIMIT),
        name="ffn_final" if final_norm else "ffn",
    )(*args)


def _pool_kernel(x_ref, *rest, has_prefix, start_pos, bb, tt):
    rest = list(rest)
    pre_ref = rest.pop(0) if has_prefix else None
    g_ref, w_ref, sc_ref, o_ref, st_ref, ext_ref = rest
    ti = pl.program_id(1)
    te = tt + POOL_HALO

    @pl.when(ti == 0)
    def _():
        ext_ref[:, 0:POOL_HALO, :] = jnp.zeros((bb, POOL_HALO, D_MODEL), F32)
        if has_prefix:
            ext_ref[:, POOL_HALO - POOL_PREFIX:POOL_HALO, :] = pre_ref[...]

    if tt >= POOL_HALO:
        @pl.when(ti > 0)
        def _():
            ext_ref[:, 0:POOL_HALO, :] = ext_ref[:, tt:te, :]

    x = x_ref[...].reshape(bb, tt, D_MODEL)
    u = _rms(x, g_ref[...])
    ext_ref[:, POOL_HALO:te, :] = u

    n_seen = start_pos + ti * tt + lax.broadcasted_iota(jnp.int32, (1, tt, 1), 1) + 1
    for gi, wg in enumerate(POOL_WINDOWS):
        sl = slice(gi * POOL_GROUP_DIM, (gi + 1) * POOL_GROUP_DIM)
        s = ext_ref[:, :, sl].reshape(bb * te, POOL_GROUP_DIM)
        span = 1
        while span < wg:
            s = s + pltpu.roll(s, span, axis=0)
            span *= 2
        s = s.reshape(bb, te, POOL_GROUP_DIM)[:, POOL_HALO:, :]
        cnt = jnp.minimum(n_seen, wg).astype(F32)
        p = s / cnt - u[:, :, sl]
        p2 = p.reshape(bb * tt, POOL_GROUP_DIM).astype(BF16)
        y = jnp.dot(p2, w_ref[gi], preferred_element_type=F32).reshape(bb, tt, POOL_GROUP_DIM)
        o_ref[:, sl] = (x[:, :, sl] + y * sc_ref[:, :, sl]).reshape(bb * tt, POOL_GROUP_DIM)

    @pl.when(ti == pl.num_programs(1) - 1)
    def _():
        st_ref[...] = ext_ref[:, te - POOL_PREFIX:te, :]


def _row_blocks(row0, n_seq, t, bb, tt):
    first, per_seq = row0 // (bb * tt), t // tt
    return lambda i, j: (first + i * per_seq + j, 0)


def _pool(x, prefix, g, w, scale, layer, *, row0, n_seq, t, start_pos, bb, tt):
    has_prefix = prefix is not None
    rows = _row_blocks(row0, n_seq, t, bb, tt)
    in_specs = [pl.BlockSpec((bb * tt, D_MODEL), rows)]
    args = [x]
    if has_prefix:
        in_specs.append(pl.BlockSpec((bb, POOL_PREFIX, D_MODEL), lambda i, j: (i, 0, 0)))
        args.append(prefix)
    in_specs += [
        _resident((1, 1, D_MODEL)),
        _resident_layer((len(POOL_WINDOWS), POOL_GROUP_DIM, POOL_GROUP_DIM), layer),
        _resident((1, 1, D_MODEL)),
    ]
    args += [g.reshape(1, 1, D_MODEL), w, scale.reshape(1, 1, D_MODEL)]
    return pl.pallas_call(
        functools.partial(_pool_kernel, has_prefix=has_prefix, start_pos=start_pos, bb=bb, tt=tt),
        out_shape=(jax.ShapeDtypeStruct(x.shape, F32),
                   jax.ShapeDtypeStruct((n_seq, POOL_PREFIX, D_MODEL), F32)),
        grid=(n_seq // bb, t // tt),
        in_specs=in_specs,
        out_specs=(pl.BlockSpec((bb * tt, D_MODEL), rows),
                   pl.BlockSpec((bb, POOL_PREFIX, D_MODEL), lambda i, j: (i, 0, 0))),
        scratch_shapes=[pltpu.VMEM((bb, tt + POOL_HALO, D_MODEL), F32)],
        input_output_aliases={0: 0},
        compiler_params=pltpu.CompilerParams(
            dimension_semantics=("arbitrary", "arbitrary"), vmem_limit_bytes=VMEM_LIMIT),
        name="pool_sample" if has_prefix else "pool_prompt",
    )(*args)


def _rope_tables(pos):
    half = HEAD_DIM // 2
    inv = ROPE_THETA ** (-jnp.arange(half, dtype=F32) / half)
    ang = pos.astype(F32)[:, None] * inv[None, :]
    cos, sin = jnp.cos(ang), jnp.sin(ang)
    cos64 = jnp.concatenate([cos, cos], axis=-1)
    sin64 = jnp.concatenate([-sin, sin], axis=-1)
    return jnp.tile(cos64, (1, HEADS_PER_VREG)), jnp.tile(sin64, (1, HEADS_PER_VREG))


def _rope_col(xc, cos, sin, first_half):
    half = HEAD_DIM // 2
    partner = jnp.where(first_half, pltpu.roll(xc, LANES - half, axis=1), pltpu.roll(xc, half, axis=1))
    return xc * cos + partner * sin


def _head_half_variants(col, lo_half):
    sw = pltpu.roll(col, HEAD_DIM, axis=1)
    zero = jnp.zeros_like(col)
    return {
        (0, 0): jnp.where(lo_half, col, zero),
        (0, 1): jnp.where(lo_half, zero, sw),
        (1, 0): jnp.where(lo_half, sw, zero),
        (1, 1): jnp.where(lo_half, zero, col),
    }


def _softmax_with_sink(s, sink):
    m = jnp.maximum(jnp.max(s, axis=-1, keepdims=True), sink)
    e = jnp.exp2(s - m)
    denom = jnp.sum(e, axis=-1, keepdims=True) + jnp.exp2(sink - m)
    return e.astype(BF16), 1.0 / denom


def _attn_prompt_kernel(x_ref, cos_ref, sin_ref, g_ref, wqkv_ref, bqkv_ref, wo_ref, bo_ref, sink_ref,
                        o_ref, kout_ref, vout_ref, kt_ref, vlo_ref, vhi_ref, *, nblk):
    j = pl.program_id(1)
    w = WINDOW
    r = nblk * w

    @pl.when(j == 0)
    def _():
        kt_ref[...] = jnp.zeros(kt_ref.shape, BF16)
        vlo_ref[:, 0:w, 0:LANES] = jnp.zeros((N_KV_HEADS, w, LANES), BF16)
        vhi_ref[:, 0:w, 0:LANES] = jnp.zeros((N_KV_HEADS, w, LANES), BF16)
        keys = vlo_ref.shape[1]
        ones_lo = (lax.broadcasted_iota(jnp.int32, (N_KV_HEADS, keys, LANES), 2) < HEAD_DIM).astype(BF16)
        vlo_ref[:, :, LANES:] = ones_lo
        vhi_ref[:, :, LANES:] = 1 - ones_lo

    @pl.when(j > 0)
    def _():
        kt_ref[:, :, 0:w] = kt_ref[:, :, r:r + w]
        vlo_ref[:, 0:w, 0:LANES] = vlo_ref[:, r:r + w, 0:LANES]
        vhi_ref[:, 0:w, 0:LANES] = vhi_ref[:, r:r + w, 0:LANES]

    rc = ATTN_PROJ_BLOCKS * w
    n_chunks = nblk // ATTN_PROJ_BLOCKS
    lane = lax.broadcasted_iota(jnp.int32, (rc, LANES), 1)
    first_half = (lane % HEAD_DIM) < (HEAD_DIM // 2)
    lo_half = lane < HEAD_DIM
    lo_half_2w = lax.broadcasted_iota(jnp.int32, (2 * w, LANES), 1) < HEAD_DIM

    u_of, qcols_of = {}, {}

    def project_piece(ch, t):
        rows = slice(ch * rc, (ch + 1) * rc)
        if t == 0:
            u_of[ch] = _rms(x_ref[rows, :], g_ref[...]).astype(BF16)
            qcols_of[ch] = []
        cols = slice(t * 2 * LANES, (t + 1) * 2 * LANES)
        tile = jnp.dot(u_of[ch], wqkv_ref[:, cols], preferred_element_type=F32) + bqkv_ref[:, cols]
        halves = [tile[:, :LANES], tile[:, LANES:]]
        if t < Q_COLS // 2:
            cos, sin = cos_ref[rows, :], sin_ref[rows, :]
            for hc in halves:
                qcols_of[ch].append((_rope_col(hc, cos, sin, first_half) * QK_SCALE_LOG2).astype(BF16))
            return
        is_k = t == Q_COLS // 2
        if is_k:
            cos, sin = cos_ref[rows, :], sin_ref[rows, :]
            halves = [_rope_col(hc, cos, sin, first_half) for hc in halves]
        krows = slice(w + ch * rc, w + (ch + 1) * rc)
        for c, col in enumerate(halves):
            if ch == n_chunks - 1:
                out_ref = kout_ref if is_k else vout_ref
                out_ref[0, :, c * LANES:(c + 1) * LANES] = col[rc - w:]
            if is_k:
                col_t = col.T.astype(BF16)
                for hb in range(HEADS_PER_VREG):
                    kvh = c * HEADS_PER_VREG + hb
                    head_t = col_t[hb * HEAD_DIM:(hb + 1) * HEAD_DIM, :]
                    for a in range(HEADS_PER_VREG):
                        kt_ref[kvh * HEADS_PER_VREG + a, a * HEAD_DIM:(a + 1) * HEAD_DIM, krows] = head_t
            else:
                var = _head_half_variants(col, lo_half)
                for hb in range(HEADS_PER_VREG):
                    kvh = c * HEADS_PER_VREG + hb
                    vlo_ref[kvh, krows, 0:LANES] = var[(hb, 0)].astype(BF16)
                    vhi_ref[kvh, krows, 0:LANES] = var[(hb, 1)].astype(BF16)

    for t in range(PROJ_PIECES):
        project_piece(0, t)

    row = lax.broadcasted_iota(jnp.int32, (2 * w, 2 * w), 0) % w
    col = lax.broadcasted_iota(jnp.int32, (2 * w, 2 * w), 1)
    band = (col > row) & (col <= row + w)
    first_key = jnp.where(j > 0, 0, w)
    band_first = band & (col >= first_key)
    top_rows = lax.broadcasted_iota(jnp.int32, (2 * w, 1), 0) < w

    groups = [(ch, kvh, bl, a) for ch in range(n_chunks) for kvh in range(N_KV_HEADS)
              for bl in range(ATTN_PROJ_BLOCKS) for a in range(HEADS_PER_VREG)]

    def scores(ch, kvh, bl, a):
        qcols = qcols_of[ch]
        qrows = slice(bl * w, (bl + 1) * w)
        blk = ch * ATTN_PROJ_BLOCKS + bl
        qop = jnp.concatenate([qcols[2 * kvh][qrows], qcols[2 * kvh + 1][qrows]], axis=0)
        return jnp.dot(qop, kt_ref[kvh * HEADS_PER_VREG + a, :, blk * w:(blk + 2) * w],
                       preferred_element_type=F32)

    groups_per_chunk = len(groups) // n_chunks
    pending = [scores(*g) for g in groups[:QK_LOOKAHEAD]]
    es, sink_terms, o_parts, acc = [], [], [], None
    for gi, (ch, kvh, bl, a) in enumerate(groups):
        gl = gi % groups_per_chunk
        if 0 <= gl - PROJ_AT_GROUP < PROJ_PIECES and ch + 1 < n_chunks:
            project_piece(ch + 1, gl - PROJ_AT_GROUP)
        if gi + QK_LOOKAHEAD < len(groups):
            pending.append(scores(*groups[gi + QK_LOOKAHEAD]))
        blk = ch * ATTN_PROJ_BLOCKS + bl
        s = jnp.where(band_first if blk == 0 else band, pending[gi], NEG_INF)
        pending[gi] = None
        h0 = kvh * GROUP + a
        sink = jnp.where(top_rows, sink_ref[h0] * LOG2_E, sink_ref[h0 + HEADS_PER_VREG] * LOG2_E)
        m = jnp.maximum(jnp.max(s, axis=-1, keepdims=True), sink)
        es.append(jnp.exp2(s - m).astype(BF16))
        sink_terms.append(jnp.exp2(sink - m))
        if a < HEADS_PER_VREG - 1:
            continue
        krows = slice(blk * w, (blk + 2) * w)
        vstack = jnp.concatenate([vlo_ref[kvh, krows, :], vhi_ref[kvh, krows, :]], axis=0)
        res = jnp.dot(jnp.concatenate(es, axis=1), vstack, preferred_element_type=F32)
        den = res[:, LANES:] + jnp.where(lo_half_2w, sink_terms[0], sink_terms[1])
        o = res[:, :LANES] * (1.0 / den)
        o_parts.append(jnp.concatenate([o[:w], o[w:]], axis=1).astype(BF16))
        es, sink_terms = [], []
        if bl < ATTN_PROJ_BLOCKS - 1:
            continue
        part = jnp.dot(jnp.concatenate(o_parts, axis=0),
                       wo_ref[kvh * GROUP * HEAD_DIM:(kvh + 1) * GROUP * HEAD_DIM, :],
                       preferred_element_type=F32)
        o_parts = []
        acc = part if acc is None else acc + part
        if kvh == N_KV_HEADS - 1:
            rows = slice(ch * rc, (ch + 1) * rc)
            o_ref[rows, :] = x_ref[rows, :] + acc + bo_ref[...]
            acc = None


def _attn_prompt(x, g, w_qkv, b_qkv, w_o, b_o, sinks, layer, *, n_seq, t, nblk):
    b = n_seq
    r = nblk * WINDOW
    cos, sin = _rope_tables(jnp.arange(t))
    keys = (nblk + 1) * WINDOW
    rows = _row_blocks(0, n_seq, t, 1, r)
    return pl.pallas_call(
        functools.partial(_attn_prompt_kernel, nblk=nblk),
        out_shape=(jax.ShapeDtypeStruct(x.shape, F32),
                   jax.ShapeDtypeStruct((b, WINDOW, KV_DIM), F32),
                   jax.ShapeDtypeStruct((b, WINDOW, KV_DIM), F32)),
        grid=(b, t // r),
        in_specs=[
            pl.BlockSpec((r, D_MODEL), rows),
            pl.BlockSpec((r, LANES), lambda i, j: (j, 0)),
            pl.BlockSpec((r, LANES), lambda i, j: (j, 0)),
            _resident((1, D_MODEL)),
            _resident_layer((D_MODEL, QKV_DIM), layer),
            _resident((1, QKV_DIM)),
            _resident_layer((Q_DIM, D_MODEL), layer),
            _resident((1, D_MODEL)),
            pl.BlockSpec(memory_space=pltpu.SMEM),
        ],
        out_specs=(pl.BlockSpec((r, D_MODEL), rows),
                   pl.BlockSpec((1, WINDOW, KV_DIM), lambda i, j: (i, 0, 0)),
                   pl.BlockSpec((1, WINDOW, KV_DIM), lambda i, j: (i, 0, 0))),
        scratch_shapes=[pltpu.VMEM((2 * N_KV_HEADS, LANES, keys), BF16),
                        pltpu.VMEM((N_KV_HEADS, keys, 2 * LANES), BF16),
                        pltpu.VMEM((N_KV_HEADS, keys, 2 * LANES), BF16)],
        input_output_aliases={0: 0},
        compiler_params=pltpu.CompilerParams(
            dimension_semantics=("arbitrary", "arbitrary"), vmem_limit_bytes=VMEM_LIMIT),
        name="attn_prompt",
    )(x, cos, sin, g.reshape(1, D_MODEL), w_qkv, b_qkv.reshape(1, QKV_DIM), w_o,
      b_o.reshape(1, D_MODEL), sinks)


SAMPLE_KEYS = 2 * WINDOW
SAMPLE_UNROLL = 4


def _attn_sample_kernel(x_ref, ck_ref, cv_ref, cos_ref, sin_ref, g_ref, wqkv_ref, bqkv_ref, wo_ref, bo_ref,
                        sinkcol_ref, o_ref, kout_ref, vout_ref, q_s, kn_s, vn_s, kk_s, vv_s, oa_s, *, dt):
    bb = ck_ref.shape[0]
    w = WINDOW
    x = x_ref[...]
    u = _rms(x, g_ref[...]).astype(BF16)
    qkv = jnp.dot(u, wqkv_ref[...], preferred_element_type=F32) + bqkv_ref[...]
    cos, sin = cos_ref[...], sin_ref[...]
    lane_t = lax.broadcasted_iota(jnp.int32, (bb * dt, LANES), 1)
    first_half = (lane_t % HEAD_DIM) < (HEAD_DIM // 2)
    for c in range(Q_COLS):
        qc = _rope_col(qkv[:, c * LANES:(c + 1) * LANES], cos, sin, first_half)
        q_s[:, c * LANES:(c + 1) * LANES] = qc * QK_SCALE_LOG2
    for c in range(KV_COLS):
        kn_s[:, c * LANES:(c + 1) * LANES] = _rope_col(
            qkv[:, Q_DIM + c * LANES:Q_DIM + (c + 1) * LANES], cos, sin, first_half)
    vn_s[...] = qkv[:, Q_DIM + KV_DIM:]

    kk_s[:, w + dt:, :] = jnp.zeros((SAMPLE_UNROLL, SAMPLE_KEYS - w - dt, KV_DIM), BF16)
    vv_s[:, w + dt:, :] = jnp.zeros((SAMPLE_UNROLL, SAMPLE_KEYS - w - dt, KV_DIM), BF16)

    heads_per_col = N_HEADS // KV_COLS
    qcols_per_col = heads_per_col // HEADS_PER_VREG
    rows = heads_per_col * dt
    lane8 = lax.broadcasted_iota(jnp.int32, (dt, LANES), 1)
    lo8 = lane8 < HEAD_DIM
    t_row = lax.broadcasted_iota(jnp.int32, (rows, SAMPLE_KEYS), 0) % dt
    key = lax.broadcasted_iota(jnp.int32, (rows, SAMPLE_KEYS), 1)
    valid = (key > t_row) & (key <= t_row + w)

    def stage(b, slot):
        r0 = pl.multiple_of(b * dt, dt)
        knew = kn_s[pl.ds(r0, dt), :]
        vnew = vn_s[pl.ds(r0, dt), :]
        kout_ref[b, 0:w - dt, :] = ck_ref[b, dt:w, :]
        kout_ref[b, w - dt:w, :] = knew
        vout_ref[b, 0:w - dt, :] = cv_ref[b, dt:w, :]
        vout_ref[b, w - dt:w, :] = vnew
        kk_s[slot, 0:w, :] = ck_ref[b].astype(BF16)
        kk_s[slot, w:w + dt, :] = knew.astype(BF16)
        vv_s[slot, 0:w, :] = cv_ref[b].astype(BF16)
        vv_s[slot, w:w + dt, :] = vnew.astype(BF16)

    def scores(b, slot, c):
        r0 = pl.multiple_of(b * dt, dt)
        pieces = []
        for h8 in range(heads_per_col):
            qi = c * qcols_per_col + h8 // HEADS_PER_VREG
            qcol = q_s[pl.ds(r0, dt), qi * LANES:(qi + 1) * LANES]
            a, hb = h8 % HEADS_PER_VREG, h8 // GROUP
            src = qcol if a == hb else pltpu.roll(qcol, HEAD_DIM, axis=1)
            keep = lo8 if hb == 0 else jnp.logical_not(lo8)
            pieces.append(jnp.where(keep, src, 0.0))
        qop = jnp.concatenate(pieces, axis=0).astype(BF16)
        return lax.dot_general(qop, kk_s[slot, :, c * LANES:(c + 1) * LANES],
                               (((1,), (1,)), ((), ())), preferred_element_type=F32)

    def place(b, c, o):
        r0 = pl.multiple_of(b * dt, dt)
        for mm in range(qcols_per_col):
            halves = []
            for a in range(HEADS_PER_VREG):
                h8 = mm * HEADS_PER_VREG + a
                hb = h8 // GROUP
                piece = o[h8 * dt:(h8 + 1) * dt, :]
                halves.append(piece if a == hb else pltpu.roll(piece, HEAD_DIM, axis=1))
            oi = c * qcols_per_col + mm
            oa_s[pl.ds(r0, dt), oi * LANES:(oi + 1) * LANES] = jnp.where(lo8, halves[0], halves[1])

    def body(gidx, carry):
        work = [(gidx * SAMPLE_UNROLL + slot, slot, c) for slot in range(SAMPLE_UNROLL) for c in range(KV_COLS)]
        for slot in range(SAMPLE_UNROLL):
            stage(gidx * SAMPLE_UNROLL + slot, slot)
        ss = [scores(b, slot, c) for b, slot, c in work]
        sm = [_softmax_with_sink(jnp.where(valid, s, NEG_INF), sinkcol_ref[c * rows:(c + 1) * rows, :] * LOG2_E)
              for s, (b, slot, c) in zip(ss, work)]
        for (e, rinv), (b, slot, c) in zip(sm, work):
            o = jnp.dot(e, vv_s[slot, :, c * LANES:(c + 1) * LANES], preferred_element_type=F32) * rinv
            place(b, c, o)
        return carry

    lax.fori_loop(0, bb // SAMPLE_UNROLL, body, 0)

    o_ref[...] = x + jnp.dot(oa_s[...].astype(BF16), wo_ref[...], preferred_element_type=F32) + bo_ref[...]


def _attn_sample(x, cache_k, cache_v, g, w_qkv, b_qkv, w_o, b_o, sinks, layer, *, row0, dt, bb):
    b = cache_k.shape[0]
    rows = _row_blocks(row0, b, dt, bb, dt)
    rows1 = lambda i: rows(i, 0)
    cos, sin = _rope_tables(PAST_LEN + jnp.arange(dt))
    cos, sin = jnp.tile(cos, (bb, 1)), jnp.tile(sin, (bb, 1))
    sinkcol = jnp.repeat(sinks, dt).reshape(N_HEADS * dt, 1)
    n = bb * dt
    return pl.pallas_call(
        functools.partial(_attn_sample_kernel, dt=dt),
        out_shape=(jax.ShapeDtypeStruct(x.shape, F32),
                   jax.ShapeDtypeStruct((b, WINDOW, KV_DIM), F32),
                   jax.ShapeDtypeStruct((b, WINDOW, KV_DIM), F32)),
        grid=(b // bb,),
        in_specs=[
            pl.BlockSpec((n, D_MODEL), rows1),
            pl.BlockSpec((bb, WINDOW, KV_DIM), lambda i: (i, 0, 0)),
            pl.BlockSpec((bb, WINDOW, KV_DIM), lambda i: (i, 0, 0)),
            _resident((n, LANES)),
            _resident((n, LANES)),
            _resident((1, D_MODEL)),
            _resident_layer((D_MODEL, QKV_DIM), layer),
            _resident((1, QKV_DIM)),
            _resident_layer((Q_DIM, D_MODEL), layer),
            _resident((1, D_MODEL)),
            _resident((N_HEADS * dt, 1)),
        ],
        out_specs=(pl.BlockSpec((n, D_MODEL), rows1),
                   pl.BlockSpec((bb, WINDOW, KV_DIM), lambda i: (i, 0, 0)),
                   pl.BlockSpec((bb, WINDOW, KV_DIM), lambda i: (i, 0, 0))),
        scratch_shapes=[pltpu.VMEM((n, Q_DIM), F32),
                        pltpu.VMEM((n, KV_DIM), F32),
                        pltpu.VMEM((n, KV_DIM), F32),
                        pltpu.VMEM((SAMPLE_UNROLL, SAMPLE_KEYS, KV_DIM), BF16),
                        pltpu.VMEM((SAMPLE_UNROLL, SAMPLE_KEYS, KV_DIM), BF16),
                        pltpu.VMEM((n, Q_DIM), F32)],
        input_output_aliases={0: 0},
        compiler_params=pltpu.CompilerParams(
            dimension_semantics=("arbitrary",), vmem_limit_bytes=VMEM_LIMIT),
        name="attn_sample",
    )(x, cache_k, cache_v, cos, sin, g.reshape(1, D_MODEL), w_qkv, b_qkv.reshape(1, QKV_DIM), w_o,
      b_o.reshape(1, D_MODEL), sinkcol)


def kernel(x_prompt, x_sample, state_pool, cache_k, cache_v, norm_ffn1, ffn1_w_in, ffn1_w_out, norm_mix,
           norm_ffn2, ffn2_w_in, ffn2_w_out, pool_w, pool_scale, attn_w_qkv, attn_b_qkv, attn_w_o, attn_b_o,
           attn_sinks, norm_final):
    batch, seq, _ = x_prompt.shape
    dec_batch, dec_seq, _ = x_sample.shape
    depth = norm_ffn1.shape[0]
    n_mixers = 2
    w_in, w_out = ffn1_w_in, ffn1_w_out
    pw = pool_w.astype(BF16)
    wqkv, wo = attn_w_qkv.astype(BF16), attn_w_o.astype(BF16)
    n_attn = cache_k.shape[0]
    ck_t = jnp.transpose(cache_k, (0, 1, 3, 4, 2)).reshape(n_attn, dec_batch, KV_DIM, WINDOW)
    cv_t = jnp.transpose(cache_v, (0, 1, 3, 4, 2)).reshape(n_attn, dec_batch, KV_DIM, WINDOW)
    ck = cv = None
    state_t = jnp.swapaxes(state_pool, 1, 2)

    n_p, n_s = batch * seq, dec_batch * dec_seq
    x = (x_prompt.reshape(n_p, D_MODEL), x_sample.reshape(n_s, D_MODEL))
    pool_p, pool_s, kp_l, vp_l, ks_l, vs_l = [], [], [], [], [], []
    for i in range(depth):
        j = i // n_mixers
        is_attn = i % n_mixers == 1
        raw = 0 if i == 0 else None
        if is_attn:
            x, w_in, w_out, cv = _ffn(x, norm_ffn1[i], w_in, w_out, n_p, next_weights=(ffn2_w_in, ffn2_w_out, i),
                                      cache_t=(cv_t, j), raw_layer=raw, tm=FFN_TILE)
        else:
            x, w_in, w_out, state = _ffn(x, norm_ffn1[i], w_in, w_out, n_p, next_weights=(ffn2_w_in, ffn2_w_out, i),
                                         state_t=(state_t, j), raw_layer=raw, tm=FFN_TILE)
        if not is_attn:
            x, sp = _pool(x, None, norm_mix[i], pw, pool_scale[j], j, row0=0, n_seq=batch, t=seq,
                          start_pos=0, bb=1, tt=1024)
            x, ss = _pool(x, state, norm_mix[i], pw, pool_scale[j], j, row0=n_p, n_seq=dec_batch, t=dec_seq,
                          start_pos=PAST_LEN, bb=32, tt=dec_seq)
            pool_p.append(sp)
            pool_s.append(ss)
        else:
            x, kp, vp = _attn_prompt(x, norm_mix[i], wqkv, attn_b_qkv[j], wo, attn_b_o[j], attn_sinks[j], j,
                                     n_seq=batch, t=seq, nblk=ATTN_BLOCKS_PER_STEP)
            if ck is None:
                ck = jnp.swapaxes(ck_t[j], 1, 2)
            x, kn, vn = _attn_sample(x, ck, cv, norm_mix[i], wqkv, attn_b_qkv[j], wo, attn_b_o[j],
                                     attn_sinks[j], j, row0=n_p, dt=dec_seq, bb=32)
            ck = None
            kp_l.append(kp.reshape(batch, WINDOW, N_KV_HEADS, HEAD_DIM))
            vp_l.append(vp.reshape(batch, WINDOW, N_KV_HEADS, HEAD_DIM))
            ks_l.append(kn.reshape(dec_batch, WINDOW, N_KV_HEADS, HEAD_DIM))
            vs_l.append(vn.reshape(dec_batch, WINDOW, N_KV_HEADS, HEAD_DIM))
        if i == depth - 1:
            xp, xs = _ffn(x, norm_ffn2[i], w_in, w_out, n_p, final_gain=norm_final, split_out=True, tm=FFN_TILE)
        elif (i + 1) % n_mixers == 1:
            x, w_in, w_out, ck = _ffn(x, norm_ffn2[i], w_in, w_out, n_p,
                                      next_weights=(ffn1_w_in, ffn1_w_out, i + 1),
                                      cache_t=(ck_t, (i + 1) // n_mixers), tm=FFN_TILE)
        else:
            x, w_in, w_out = _ffn(x, norm_ffn2[i], w_in, w_out, n_p,
                                  next_weights=(ffn1_w_in, ffn1_w_out, i + 1), tm=FFN_TILE)
    return (xp.reshape(batch, seq, D_MODEL), xs.reshape(dec_batch, dec_seq, D_MODEL),
            jnp.stack(pool_p), jnp.stack(pool_s), jnp.stack(kp_l), jnp.stack(vp_l),
            jnp.stack(ks_l), jnp.stack(vs_l))
```

```python
import functools

import jax
import jax.numpy as jnp
from jax import lax
from jax.experimental import pallas as pl
from jax.experimental.pallas import tpu as pltpu

D_MODEL = 1024
D_FF = 2816
POOL_WINDOWS = (2, 4, 8, 16)
POOL_GROUP_DIM = D_MODEL // len(POOL_WINDOWS)
POOL_PREFIX = max(POOL_WINDOWS) - 1
N_HEADS = 16
N_KV_HEADS = 4
HEAD_DIM = 64
GROUP = N_HEADS // N_KV_HEADS
WINDOW = 128
ROPE_THETA = 10000.0
Q_DIM = N_HEADS * HEAD_DIM
KV_DIM = N_KV_HEADS * HEAD_DIM
QKV_DIM = Q_DIM + 2 * KV_DIM
RMS_EPS = 1e-6
NEG_INF = -1e30
PAST_LEN = 8192

LANES = 128
SUBLANES = 8
HEADS_PER_VREG = LANES // HEAD_DIM
Q_COLS = Q_DIM // LANES
KV_COLS = KV_DIM // LANES
POOL_HALO = 16
FF_CHUNK = 256
FFN_TILE = 512
FFN_SUB_ROWS = 256
RELAYOUT_SEQS = 4
RESTATE_SEQS = 8
WEIGHT_CAST_STEPS = 16
ATTN_BLOCKS_PER_STEP = 8
ATTN_PROJ_BLOCKS = 2
PROJ_PIECES = QKV_DIM // (2 * LANES)
PROJ_AT_GROUP = 3
QK_LOOKAHEAD = 3
LOG2_E = 1.4426950408889634
QK_SCALE_LOG2 = HEAD_DIM ** -0.5 * LOG2_E
VMEM_LIMIT = 52 * 1024 * 1024

F32 = jnp.float32
BF16 = jnp.bfloat16


def _rms(x, g):
    ms = jnp.mean(x * x, axis=-1, keepdims=True)
    return x * lax.rsqrt(ms + RMS_EPS) * g


def _resident(shape):
    nd = len(shape)
    return pl.BlockSpec(shape, lambda *_: (0,) * nd, pipeline_mode=pl.Buffered(1))


def _resident_layer(shape, layer):
    nd = len(shape)
    return pl.BlockSpec((None,) + shape, lambda *_: (layer,) + (0,) * nd, pipeline_mode=pl.Buffered(1))


def _zero_bits_of(v):
    bits = lax.bitcast_convert_type(v, jnp.uint32)
    return (bits >> 16) >> 16


def _after(v, zero_bits):
    reps = (v.shape[0] // zero_bits.shape[0], v.shape[1] // zero_bits.shape[1])
    bits = lax.bitcast_convert_type(v, jnp.uint32) | jnp.tile(zero_bits, reps)
    return lax.bitcast_convert_type(bits, F32)


def _ffn_kernel(*refs, n_sample_tiles, split_in, split_out, final_norm, convert_next, relayout, restate,
                raw_layer):
    refs = list(refs)
    if split_in:
        xp_ref, xs_ref = refs.pop(0), refs.pop(0)
    else:
        x_ref = refs.pop(0)
    g_ref, win_ref, wout_ref = refs[:3]
    del refs[:3]
    gf_ref = refs.pop(0) if final_norm else None
    if convert_next:
        nwin_ref, nwout_ref = refs.pop(0), refs.pop(0)
    if relayout:
        ri_ref = refs.pop(0)
    if restate:
        si_ref = refs.pop(0)
    if split_out:
        op_ref, os_ref = refs.pop(0), refs.pop(0)
    else:
        op_ref = refs.pop(0)
    if convert_next:
        cwin_ref, cwout_ref = refs.pop(0), refs.pop(0)
    if relayout:
        ro_ref = refs.pop(0)
    if restate:
        so_ref = refs.pop(0)
    h_ref = refs.pop(0)
    if raw_layer is not None:
        win_hbm, wout_hbm = win_ref, wout_ref
        win_ref, wout_ref, stage_in, stage_out, sem = refs
        slab_cols = stage_in.shape[2]
        slab_rows = stage_out.shape[1]
        n_slabs = 2 * D_FF // slab_cols

        def in_copy(c):
            return pltpu.make_async_copy(win_hbm.at[raw_layer, :, pl.ds(c * slab_cols, slab_cols)],
                                         stage_in.at[c % 2], sem.at[0, c % 2])

        def out_copy(c):
            return pltpu.make_async_copy(wout_hbm.at[raw_layer, pl.ds(c * slab_rows, slab_rows), :],
                                         stage_out.at[c % 2], sem.at[1, c % 2])

        @pl.when(pl.program_id(0) == 0)
        def _():
            in_copy(0).start()
            out_copy(0).start()
            for c in range(n_slabs):
                if c + 1 < n_slabs:
                    in_copy(c + 1).start()
                    out_copy(c + 1).start()
                in_copy(c).wait()
                win_ref[:, c * slab_cols:(c + 1) * slab_cols] = stage_in[c % 2].astype(BF16)
                out_copy(c).wait()
                wout_ref[c * slab_rows:(c + 1) * slab_rows, :] = stage_out[c % 2].astype(BF16)

    is_sample = pl.program_id(0) < n_sample_tiles
    x = jnp.where(is_sample, xs_ref[...], xp_ref[...]) if split_in else x_ref[...]
    xn = _rms(x, g_ref[...]).astype(BF16)
    tm = x.shape[0]
    n_chunks = D_FF // FF_CHUNK
    for c in range(n_chunks):
        lo = c * FF_CHUNK
        for r0 in range(0, tm, FFN_SUB_ROWS):
            xs_ = xn[r0:r0 + FFN_SUB_ROWS]
            gate = jnp.dot(xs_, win_ref[:, lo:lo + FF_CHUNK], preferred_element_type=F32)
            up = jnp.dot(xs_, win_ref[:, D_FF + lo:D_FF + lo + FF_CHUNK], preferred_element_type=F32)
            h_ref[r0:r0 + FFN_SUB_ROWS, lo:lo + FF_CHUNK] = (gate * jax.nn.sigmoid(gate) * up).astype(BF16)
        anchor = _zero_bits_of(gate[0:SUBLANES, 0:LANES])
        if convert_next:
            cw = 2 * D_FF // n_chunks
            cwin_ref[:, c * cw:(c + 1) * cw] = _after(nwin_ref[:, c * cw:(c + 1) * cw], anchor).astype(BF16)
            cr = cwout_ref.shape[0] // n_chunks
            cwout_ref[c * cr:(c + 1) * cr, :] = _after(nwout_ref[c * cr:(c + 1) * cr, :], anchor).astype(BF16)
        if relayout and c < ri_ref.shape[0]:
            ro_ref[c] = _after(ri_ref[c], anchor).T
        if restate:
            for p in range(c, POOL_PREFIX, n_chunks):
                so_ref[:, p, :] = _after(si_ref[p], anchor)
    ys = []
    for r0 in range(0, tm, FFN_SUB_ROWS):
        rows = slice(r0, r0 + FFN_SUB_ROWS)
        yr = x[rows] + 0.5 * jnp.dot(h_ref[rows, :], wout_ref[...], preferred_element_type=F32)
        ys.append(_rms(yr, gf_ref[...]) if final_norm else yr)
    y = jnp.concatenate(ys, axis=0)

    op_ref[...] = y
    if split_out:
        @pl.when(is_sample)
        def _():
            os_ref[...] = op_ref[...]


def _ffn(x, g, w_in, w_out, n_prompt_rows, final_gain=None, next_weights=None, cache_t=None, state_t=None, *,
         raw_layer=None, split_out=False, tm):
    split_in = isinstance(x, tuple)
    n_rows = sum(a.shape[0] for a in x) if split_in else x.shape[0]
    n_p, n_s = n_prompt_rows // tm, (n_rows - n_prompt_rows) // tm
    final_norm = final_gain is not None
    convert_next = next_weights is not None
    relayout = cache_t is not None
    restate = state_t is not None

    def sample_tile(i):
        return (jnp.minimum(i, n_s - 1), 0)

    def prompt_tile(i):
        return (jnp.maximum(i - n_s, 0), 0)

    def sample_first_tile(i):
        return (jnp.where(i < n_s, n_p + i, i - n_s), 0)

    def same_tile(i):
        return (i, 0)

    reordered = split_in or split_out
    combined_tile = sample_first_tile if reordered else same_tile
    if split_in:
        in_specs = [pl.BlockSpec((tm, D_MODEL), prompt_tile), pl.BlockSpec((tm, D_MODEL), sample_tile)]
        args = list(x)
    else:
        in_specs = [pl.BlockSpec((tm, D_MODEL), combined_tile)]
        args = [x]
    scratch_shapes = [pltpu.VMEM((tm, D_FF), BF16)]
    if raw_layer is None:
        in_specs += [_resident((1, D_MODEL)), _resident((D_MODEL, 2 * D_FF)), _resident((D_FF, D_MODEL))]
    else:
        in_specs += [_resident((1, D_MODEL)), pl.BlockSpec(memory_space=pl.ANY), pl.BlockSpec(memory_space=pl.ANY)]
        n_slabs = D_FF // FF_CHUNK
        scratch_shapes += [pltpu.VMEM((D_MODEL, 2 * D_FF), BF16), pltpu.VMEM((D_FF, D_MODEL), BF16),
                           pltpu.VMEM((2, D_MODEL, 2 * D_FF // n_slabs), F32),
                           pltpu.VMEM((2, D_FF // n_slabs, D_MODEL), F32),
                           pltpu.SemaphoreType.DMA((2, 2))]
    args += [g.reshape(1, D_MODEL), w_in, w_out]
    if split_out:
        out_shape = [jax.ShapeDtypeStruct((n_p * tm, D_MODEL), F32), jax.ShapeDtypeStruct((n_s * tm, D_MODEL), F32)]
        out_specs = [pl.BlockSpec((tm, D_MODEL), prompt_tile), pl.BlockSpec((tm, D_MODEL), sample_tile)]
    else:
        out_shape = [jax.ShapeDtypeStruct((n_rows, D_MODEL), F32)]
        out_specs = [pl.BlockSpec((tm, D_MODEL), combined_tile)]
    if final_norm:
        in_specs.append(_resident((1, D_MODEL)))
        args.append(final_gain.reshape(1, D_MODEL))
    if convert_next:
        nw_in, nw_out, layer = next_weights
        rows_in, rows_out = D_MODEL // WEIGHT_CAST_STEPS, D_FF // WEIGHT_CAST_STEPS

        def slab(i):
            return (jnp.minimum(i, WEIGHT_CAST_STEPS - 1), 0)

        in_specs += [pl.BlockSpec((None, rows_in, 2 * D_FF), lambda i: (layer,) + slab(i)),
                     pl.BlockSpec((None, rows_out, D_MODEL), lambda i: (layer,) + slab(i))]
        args += [nw_in, nw_out]
        out_shape += [jax.ShapeDtypeStruct((D_MODEL, 2 * D_FF), BF16), jax.ShapeDtypeStruct((D_FF, D_MODEL), BF16)]
        out_specs += [pl.BlockSpec((rows_in, 2 * D_FF), slab), pl.BlockSpec((rows_out, D_MODEL), slab)]
    if relayout:
        cache, cache_layer = cache_t
        n_seq = cache.shape[1]
        last_group = n_seq // RELAYOUT_SEQS - 1

        def seq_group(i):
            return (jnp.minimum(i, last_group), 0, 0)

        in_specs.append(pl.BlockSpec((None, RELAYOUT_SEQS, KV_DIM, WINDOW), lambda i: (cache_layer,) + seq_group(i)))
        args.append(cache)
        out_shape.append(jax.ShapeDtypeStruct((n_seq, WINDOW, KV_DIM), F32))
        out_specs.append(pl.BlockSpec((RELAYOUT_SEQS, WINDOW, KV_DIM), seq_group))
    if restate:
        state, state_layer = state_t
        n_seq = state.shape[2]
        last_state_group = n_seq // RESTATE_SEQS - 1

        def state_group(i):
            return jnp.minimum(i, last_state_group)

        in_specs.append(pl.BlockSpec((None, POOL_PREFIX, RESTATE_SEQS, D_MODEL),
                                     lambda i: (state_layer, 0, state_group(i), 0)))
        args.append(state)
        out_shape.append(jax.ShapeDtypeStruct((n_seq, POOL_PREFIX, D_MODEL), F32))
        out_specs.append(pl.BlockSpec((RESTATE_SEQS, POOL_PREFIX, D_MODEL), lambda i: (state_group(i), 0, 0)))
    return pl.pallas_call(
        functools.partial(_ffn_kernel, n_sample_tiles=n_s, split_in=split_in, split_out=split_out,
                          final_norm=final_norm, convert_next=convert_next, relayout=relayout, restate=restate,
                          raw_layer=raw_layer),
        out_shape=tuple(out_shape),
        grid=(n_p + n_s,),
        in_specs=in_specs,
        out_specs=tuple(out_specs),
        scratch_shapes=scratch_shapes,
        compiler_params=pltpu.CompilerParams(
            dimension_semantics=("arbitrary",), vmem_limit_bytes=VMEM_LIMIT),
        name="ffn_final" if final_norm else "ffn",
    )(*args)


def _pool_kernel(x_ref, *rest, has_prefix, start_pos, bb, tt):
    rest = list(rest)
    pre_ref = rest.pop(0) if has_prefix else None
    g_ref, w_ref, sc_ref, o_ref, st_ref, ext_ref = rest
    ti = pl.program_id(1)
    te = tt + POOL_HALO

    @pl.when(ti == 0)
    def _():
        ext_ref[:, 0:POOL_HALO, :] = jnp.zeros((bb, POOL_HALO, D_MODEL), F32)
        if has_prefix:
            ext_ref[:, POOL_HALO - POOL_PREFIX:POOL_HALO, :] = pre_ref[...]

    if tt >= POOL_HALO:
        @pl.when(ti > 0)
        def _():
            ext_ref[:, 0:POOL_HALO, :] = ext_ref[:, tt:te, :]

    x = x_ref[...].reshape(bb, tt, D_MODEL)
    u = _rms(x, g_ref[...])
    ext_ref[:, POOL_HALO:te, :] = u

    n_seen = start_pos + ti * tt + lax.broadcasted_iota(jnp.int32, (1, tt, 1), 1) + 1
    for gi, wg in enumerate(POOL_WINDOWS):
        sl = slice(gi * POOL_GROUP_DIM, (gi + 1) * POOL_GROUP_DIM)
        s = ext_ref[:, :, sl].reshape(bb * te, POOL_GROUP_DIM)
        span = 1
        while span < wg:
            s = s + pltpu.roll(s, span, axis=0)
            span *= 2
        s = s.reshape(bb, te, POOL_GROUP_DIM)[:, POOL_HALO:, :]
        cnt = jnp.minimum(n_seen, wg).astype(F32)
        p = s / cnt - u[:, :, sl]
        p2 = p.reshape(bb * tt, POOL_GROUP_DIM).astype(BF16)
        y = jnp.dot(p2, w_ref[gi], preferred_element_type=F32).reshape(bb, tt, POOL_GROUP_DIM)
        o_ref[:, sl] = (x[:, :, sl] + y * sc_ref[:, :, sl]).reshape(bb * tt, POOL_GROUP_DIM)

    @pl.when(ti == pl.num_programs(1) - 1)
    def _():
        st_ref[...] = ext_ref[:, te - POOL_PREFIX:te, :]


def _row_blocks(row0, n_seq, t, bb, tt):
    first, per_seq = row0 // (bb * tt), t // tt
    return lambda i, j: (first + i * per_seq + j, 0)


def _pool(x, prefix, g, w, scale, layer, *, row0, n_seq, t, start_pos, bb, tt):
    has_prefix = prefix is not None
    rows = _row_blocks(row0, n_seq, t, bb, tt)
    in_specs = [pl.BlockSpec((bb * tt, D_MODEL), rows)]
    args = [x]
    if has_prefix:
        in_specs.append(pl.BlockSpec((bb, POOL_PREFIX, D_MODEL), lambda i, j: (i, 0, 0)))
        args.append(prefix)
    in_specs += [
        _resident((1, 1, D_MODEL)),
        _resident_layer((len(POOL_WINDOWS), POOL_GROUP_DIM, POOL_GROUP_DIM), layer),
        _resident((1, 1, D_MODEL)),
    ]
    args += [g.reshape(1, 1, D_MODEL), w, scale.reshape(1, 1, D_MODEL)]
    return pl.pallas_call(
        functools.partial(_pool_kernel, has_prefix=has_prefix, start_pos=start_pos, bb=bb, tt=tt),
        out_shape=(jax.ShapeDtypeStruct(x.shape, F32),
                   jax.ShapeDtypeStruct((n_seq, POOL_PREFIX, D_MODEL), F32)),
        grid=(n_seq // bb, t // tt),
        in_specs=in_specs,
        out_specs=(pl.BlockSpec((bb * tt, D_MODEL), rows),
                   pl.BlockSpec((bb, POOL_PREFIX, D_MODEL), lambda i, j: (i, 0, 0))),
        scratch_shapes=[pltpu.VMEM((bb, tt + POOL_HALO, D_MODEL), F32)],
        input_output_aliases={0: 0},
        compiler_params=pltpu.CompilerParams(
            dimension_semantics=("arbitrary", "arbitrary"), vmem_limit_bytes=VMEM_LIMIT),
        name="pool_sample" if has_prefix else "pool_prompt",
    )(*args)


def _rope_tables(pos):
    half = HEAD_DIM // 2
    lane = jnp.arange(LANES)
    inv = ROPE_THETA ** (-(lane % half).astype(F32) / half)
    ang = pos.astype(F32)[:, None] * inv[None, :]
    sign = jnp.where(lane % HEAD_DIM < half, -1.0, 1.0).astype(F32)
    return jnp.cos(ang), jnp.sin(ang) * sign[None, :]


def _rope_col(xc, cos, sin, first_half):
    half = HEAD_DIM // 2
    partner = jnp.where(first_half, pltpu.roll(xc, LANES - half, axis=1), pltpu.roll(xc, half, axis=1))
    return xc * cos + partner * sin


def _head_half_variants(col, lo_half):
    sw = pltpu.roll(col, HEAD_DIM, axis=1)
    zero = jnp.zeros_like(col)
    return {
        (0, 0): jnp.where(lo_half, col, zero),
        (0, 1): jnp.where(lo_half, zero, sw),
        (1, 0): jnp.where(lo_half, sw, zero),
        (1, 1): jnp.where(lo_half, zero, col),
    }


def _softmax_with_sink(s, sink):
    m = jnp.maximum(jnp.max(s, axis=-1, keepdims=True), sink)
    e = jnp.exp2(s - m)
    denom = jnp.sum(e, axis=-1, keepdims=True) + jnp.exp2(sink - m)
    return e.astype(BF16), 1.0 / denom


def _attn_prompt_kernel(x_ref, cos_ref, sin_ref, g_ref, wqkv_ref, bqkv_ref, wo_ref, bo_ref, sink_ref,
                        o_ref, kout_ref, vout_ref, kt_ref, vlo_ref, vhi_ref, *, nblk):
    j = pl.program_id(1)
    w = WINDOW
    r = nblk * w

    @pl.when(j == 0)
    def _():
        kt_ref[...] = jnp.zeros(kt_ref.shape, BF16)
        vlo_ref[:, 0:w, 0:LANES] = jnp.zeros((N_KV_HEADS, w, LANES), BF16)
        vhi_ref[:, 0:w, 0:LANES] = jnp.zeros((N_KV_HEADS, w, LANES), BF16)
        keys = vlo_ref.shape[1]
        ones_lo = (lax.broadcasted_iota(jnp.int32, (N_KV_HEADS, keys, LANES), 2) < HEAD_DIM).astype(BF16)
        vlo_ref[:, :, LANES:] = ones_lo
        vhi_ref[:, :, LANES:] = 1 - ones_lo

    @pl.when(j > 0)
    def _():
        kt_ref[:, :, 0:w] = kt_ref[:, :, r:r + w]
        vlo_ref[:, 0:w, 0:LANES] = vlo_ref[:, r:r + w, 0:LANES]
        vhi_ref[:, 0:w, 0:LANES] = vhi_ref[:, r:r + w, 0:LANES]

    rc = ATTN_PROJ_BLOCKS * w
    n_chunks = nblk // ATTN_PROJ_BLOCKS
    lane = lax.broadcasted_iota(jnp.int32, (rc, LANES), 1)
    first_half = (lane % HEAD_DIM) < (HEAD_DIM // 2)
    lo_half = lane < HEAD_DIM
    lo_half_2w = lax.broadcasted_iota(jnp.int32, (2 * w, LANES), 1) < HEAD_DIM

    u_of, qcols_of = {}, {}

    def project_piece(ch, t):
        rows = slice(ch * rc, (ch + 1) * rc)
        if t == 0:
            u_of[ch] = _rms(x_ref[rows, :], g_ref[...]).astype(BF16)
            qcols_of[ch] = []
        cols = slice(t * 2 * LANES, (t + 1) * 2 * LANES)
        tile = jnp.dot(u_of[ch], wqkv_ref[:, cols], preferred_element_type=F32) + bqkv_ref[:, cols]
        halves = [tile[:, :LANES], tile[:, LANES:]]
        if t < Q_COLS // 2:
            cos, sin = cos_ref[rows, :], sin_ref[rows, :]
            for hc in halves:
                qcols_of[ch].append((_rope_col(hc, cos, sin, first_half) * QK_SCALE_LOG2).astype(BF16))
            return
        is_k = t == Q_COLS // 2
        if is_k:
            cos, sin = cos_ref[rows, :], sin_ref[rows, :]
            halves = [_rope_col(hc, cos, sin, first_half) for hc in halves]
        krows = slice(w + ch * rc, w + (ch + 1) * rc)
        for c, col in enumerate(halves):
            if ch == n_chunks - 1:
                out_ref = kout_ref if is_k else vout_ref
                out_ref[0, :, c * LANES:(c + 1) * LANES] = col[rc - w:]
            if is_k:
                col_t = col.T.astype(BF16)
                for hb in range(HEADS_PER_VREG):
                    kvh = c * HEADS_PER_VREG + hb
                    head_t = col_t[hb * HEAD_DIM:(hb + 1) * HEAD_DIM, :]
                    for a in range(HEADS_PER_VREG):
                        kt_ref[kvh * HEADS_PER_VREG + a, a * HEAD_DIM:(a + 1) * HEAD_DIM, krows] = head_t
            else:
                var = _head_half_variants(col, lo_half)
                for hb in range(HEADS_PER_VREG):
                    kvh = c * HEADS_PER_VREG + hb
                    vlo_ref[kvh, krows, 0:LANES] = var[(hb, 0)].astype(BF16)
                    vhi_ref[kvh, krows, 0:LANES] = var[(hb, 1)].astype(BF16)

    for t in range(PROJ_PIECES):
        project_piece(0, t)

    row = lax.broadcasted_iota(jnp.int32, (2 * w, 2 * w), 0) % w
    col = lax.broadcasted_iota(jnp.int32, (2 * w, 2 * w), 1)
    band = (col > row) & (col <= row + w)
    first_key = jnp.where(j > 0, 0, w)
    band_first = band & (col >= first_key)
    top_rows = lax.broadcasted_iota(jnp.int32, (2 * w, 1), 0) < w

    groups = [(ch, kvh, bl, a) for ch in range(n_chunks) for kvh in range(N_KV_HEADS)
              for bl in range(ATTN_PROJ_BLOCKS) for a in range(HEADS_PER_VREG)]

    def scores(ch, kvh, bl, a):
        qcols = qcols_of[ch]
        qrows = slice(bl * w, (bl + 1) * w)
        blk = ch * ATTN_PROJ_BLOCKS + bl
        qop = jnp.concatenate([qcols[2 * kvh][qrows], qcols[2 * kvh + 1][qrows]], axis=0)
        return jnp.dot(qop, kt_ref[kvh * HEADS_PER_VREG + a, :, blk * w:(blk + 2) * w],
                       preferred_element_type=F32)

    groups_per_chunk = len(groups) // n_chunks
    pending = [scores(*g) for g in groups[:QK_LOOKAHEAD]]
    es, sink_terms, o_parts, acc = [], [], [], None
    for gi, (ch, kvh, bl, a) in enumerate(groups):
        gl = gi % groups_per_chunk
        if 0 <= gl - PROJ_AT_GROUP < PROJ_PIECES and ch + 1 < n_chunks:
            project_piece(ch + 1, gl - PROJ_AT_GROUP)
        if gi + QK_LOOKAHEAD < len(groups):
            pending.append(scores(*groups[gi + QK_LOOKAHEAD]))
        blk = ch * ATTN_PROJ_BLOCKS + bl
        s = jnp.where(band_first if blk == 0 else band, pending[gi], NEG_INF)
        pending[gi] = None
        h0 = kvh * GROUP + a
        sink = jnp.where(top_rows, sink_ref[h0] * LOG2_E, sink_ref[h0 + HEADS_PER_VREG] * LOG2_E)
        m = jnp.maximum(jnp.max(s, axis=-1, keepdims=True), sink)
        es.append(jnp.exp2(s - m).astype(BF16))
        sink_terms.append(jnp.exp2(sink - m))
        if a < HEADS_PER_VREG - 1:
            continue
        krows = slice(blk * w, (blk + 2) * w)
        vstack = jnp.concatenate([vlo_ref[kvh, krows, :], vhi_ref[kvh, krows, :]], axis=0)
        res = jnp.dot(jnp.concatenate(es, axis=1), vstack, preferred_element_type=F32)
        den = res[:, LANES:] + jnp.where(lo_half_2w, sink_terms[0], sink_terms[1])
        o = res[:, :LANES] * (1.0 / den)
        o_parts.append(jnp.concatenate([o[:w], o[w:]], axis=1).astype(BF16))
        es, sink_terms = [], []
        if bl < ATTN_PROJ_BLOCKS - 1:
            continue
        part = jnp.dot(jnp.concatenate(o_parts, axis=0),
                       wo_ref[kvh * GROUP * HEAD_DIM:(kvh + 1) * GROUP * HEAD_DIM, :],
                       preferred_element_type=F32)
        o_parts = []
        acc = part if acc is None else acc + part
        if kvh == N_KV_HEADS - 1:
            rows = slice(ch * rc, (ch + 1) * rc)
            o_ref[rows, :] = x_ref[rows, :] + acc + bo_ref[...]
            acc = None


def _attn_prompt(x, g, w_qkv, b_qkv, w_o, b_o, sinks, layer, *, n_seq, t, nblk):
    b = n_seq
    r = nblk * WINDOW
    cos, sin = _rope_tables(jnp.arange(t))
    keys = (nblk + 1) * WINDOW
    rows = _row_blocks(0, n_seq, t, 1, r)
    return pl.pallas_call(
        functools.partial(_attn_prompt_kernel, nblk=nblk),
        out_shape=(jax.ShapeDtypeStruct(x.shape, F32),
                   jax.ShapeDtypeStruct((b, WINDOW, KV_DIM), F32),
                   jax.ShapeDtypeStruct((b, WINDOW, KV_DIM), F32)),
        grid=(b, t // r),
        in_specs=[
            pl.BlockSpec((r, D_MODEL), rows),
            pl.BlockSpec((r, LANES), lambda i, j: (j, 0)),
            pl.BlockSpec((r, LANES), lambda i, j: (j, 0)),
            _resident((1, D_MODEL)),
            _resident_layer((D_MODEL, QKV_DIM), layer),
            _resident((1, QKV_DIM)),
            _resident_layer((Q_DIM, D_MODEL), layer),
            _resident((1, D_MODEL)),
            pl.BlockSpec(memory_space=pltpu.SMEM),
        ],
        out_specs=(pl.BlockSpec((r, D_MODEL), rows),
                   pl.BlockSpec((1, WINDOW, KV_DIM), lambda i, j: (i, 0, 0)),
                   pl.BlockSpec((1, WINDOW, KV_DIM), lambda i, j: (i, 0, 0))),
        scratch_shapes=[pltpu.VMEM((2 * N_KV_HEADS, LANES, keys), BF16),
                        pltpu.VMEM((N_KV_HEADS, keys, 2 * LANES), BF16),
                        pltpu.VMEM((N_KV_HEADS, keys, 2 * LANES), BF16)],
        input_output_aliases={0: 0},
        compiler_params=pltpu.CompilerParams(
            dimension_semantics=("arbitrary", "arbitrary"), vmem_limit_bytes=VMEM_LIMIT),
        name="attn_prompt",
    )(x, cos, sin, g.reshape(1, D_MODEL), w_qkv, b_qkv.reshape(1, QKV_DIM), w_o,
      b_o.reshape(1, D_MODEL), sinks)


SAMPLE_KEYS = 2 * WINDOW
SAMPLE_UNROLL = 4


def _attn_sample_kernel(x_ref, ck_ref, cv_ref, cos_ref, sin_ref, g_ref, wqkv_ref, bqkv_ref, wo_ref, bo_ref,
                        sinkcol_ref, o_ref, kout_ref, vout_ref, q_s, kn_s, vn_s, kk_s, vv_s, oa_s, *, dt):
    bb = ck_ref.shape[0]
    w = WINDOW
    x = x_ref[...]
    u = _rms(x, g_ref[...]).astype(BF16)
    qkv = jnp.dot(u, wqkv_ref[...], preferred_element_type=F32) + bqkv_ref[...]
    cos, sin = cos_ref[...], sin_ref[...]
    lane_t = lax.broadcasted_iota(jnp.int32, (bb * dt, LANES), 1)
    first_half = (lane_t % HEAD_DIM) < (HEAD_DIM // 2)
    for c in range(Q_COLS):
        qc = _rope_col(qkv[:, c * LANES:(c + 1) * LANES], cos, sin, first_half)
        q_s[:, c * LANES:(c + 1) * LANES] = qc * QK_SCALE_LOG2
    for c in range(KV_COLS):
        kn_s[:, c * LANES:(c + 1) * LANES] = _rope_col(
            qkv[:, Q_DIM + c * LANES:Q_DIM + (c + 1) * LANES], cos, sin, first_half)
    vn_s[...] = qkv[:, Q_DIM + KV_DIM:]

    kk_s[:, w + dt:, :] = jnp.zeros((SAMPLE_UNROLL, SAMPLE_KEYS - w - dt, KV_DIM), BF16)
    vv_s[:, w + dt:, :] = jnp.zeros((SAMPLE_UNROLL, SAMPLE_KEYS - w - dt, KV_DIM), BF16)

    heads_per_col = N_HEADS // KV_COLS
    qcols_per_col = heads_per_col // HEADS_PER_VREG
    rows = heads_per_col * dt
    lane8 = lax.broadcasted_iota(jnp.int32, (dt, LANES), 1)
    lo8 = lane8 < HEAD_DIM
    t_row = lax.broadcasted_iota(jnp.int32, (rows, SAMPLE_KEYS), 0) % dt
    key = lax.broadcasted_iota(jnp.int32, (rows, SAMPLE_KEYS), 1)
    valid = (key > t_row) & (key <= t_row + w)

    def stage(b, slot):
        r0 = pl.multiple_of(b * dt, dt)
        knew = kn_s[pl.ds(r0, dt), :]
        vnew = vn_s[pl.ds(r0, dt), :]
        kout_ref[b, 0:w - dt, :] = ck_ref[b, dt:w, :]
        kout_ref[b, w - dt:w, :] = knew
        vout_ref[b, 0:w - dt, :] = cv_ref[b, dt:w, :]
        vout_ref[b, w - dt:w, :] = vnew
        kk_s[slot, 0:w, :] = ck_ref[b].astype(BF16)
        kk_s[slot, w:w + dt, :] = knew.astype(BF16)
        vv_s[slot, 0:w, :] = cv_ref[b].astype(BF16)
        vv_s[slot, w:w + dt, :] = vnew.astype(BF16)

    def scores(b, slot, c):
        r0 = pl.multiple_of(b * dt, dt)
        pieces = []
        for h8 in range(heads_per_col):
            qi = c * qcols_per_col + h8 // HEADS_PER_VREG
            qcol = q_s[pl.ds(r0, dt), qi * LANES:(qi + 1) * LANES]
            a, hb = h8 % HEADS_PER_VREG, h8 // GROUP
            src = qcol if a == hb else pltpu.roll(qcol, HEAD_DIM, axis=1)
            keep = lo8 if hb == 0 else jnp.logical_not(lo8)
            pieces.append(jnp.where(keep, src, 0.0))
        qop = jnp.concatenate(pieces, axis=0).astype(BF16)
        return lax.dot_general(qop, kk_s[slot, :, c * LANES:(c + 1) * LANES],
                               (((1,), (1,)), ((), ())), preferred_element_type=F32)

    def place(b, c, o):
        r0 = pl.multiple_of(b * dt, dt)
        for mm in range(qcols_per_col):
            halves = []
            for a in range(HEADS_PER_VREG):
                h8 = mm * HEADS_PER_VREG + a
                hb = h8 // GROUP
                piece = o[h8 * dt:(h8 + 1) * dt, :]
                halves.append(piece if a == hb else pltpu.roll(piece, HEAD_DIM, axis=1))
            oi = c * qcols_per_col + mm
            oa_s[pl.ds(r0, dt), oi * LANES:(oi + 1) * LANES] = jnp.where(lo8, halves[0], halves[1])

    def body(gidx, carry):
        work = [(gidx * SAMPLE_UNROLL + slot, slot, c) for slot in range(SAMPLE_UNROLL) for c in range(KV_COLS)]
        for slot in range(SAMPLE_UNROLL):
            stage(gidx * SAMPLE_UNROLL + slot, slot)
        ss = [scores(b, slot, c) for b, slot, c in work]
        sm = [_softmax_with_sink(jnp.where(valid, s, NEG_INF), sinkcol_ref[c * rows:(c + 1) * rows, :] * LOG2_E)
              for s, (b, slot, c) in zip(ss, work)]
        for (e, rinv), (b, slot, c) in zip(sm, work):
            o = jnp.dot(e, vv_s[slot, :, c * LANES:(c + 1) * LANES], preferred_element_type=F32) * rinv
            place(b, c, o)
        return carry

    lax.fori_loop(0, bb // SAMPLE_UNROLL, body, 0)

    o_ref[...] = x + jnp.dot(oa_s[...].astype(BF16), wo_ref[...], preferred_element_type=F32) + bo_ref[...]


def _attn_sample(x, cache_k, cache_v, g, w_qkv, b_qkv, w_o, b_o, sinks, layer, *, row0, dt, bb):
    b = cache_k.shape[0]
    rows = _row_blocks(row0, b, dt, bb, dt)
    rows1 = lambda i: rows(i, 0)
    cos, sin = _rope_tables(PAST_LEN + jnp.arange(dt))
    cos, sin = jnp.tile(cos, (bb, 1)), jnp.tile(sin, (bb, 1))
    sinkcol = jnp.repeat(sinks, dt).reshape(N_HEADS * dt, 1)
    n = bb * dt
    return pl.pallas_call(
        functools.partial(_attn_sample_kernel, dt=dt),
        out_shape=(jax.ShapeDtypeStruct(x.shape, F32),
                   jax.ShapeDtypeStruct((b, WINDOW, KV_DIM), F32),
                   jax.ShapeDtypeStruct((b, WINDOW, KV_DIM), F32)),
        grid=(b // bb,),
        in_specs=[
            pl.BlockSpec((n, D_MODEL), rows1),
            pl.BlockSpec((bb, WINDOW, KV_DIM), lambda i: (i, 0, 0)),
            pl.BlockSpec((bb, WINDOW, KV_DIM), lambda i: (i, 0, 0)),
            _resident((n, LANES)),
            _resident((n, LANES)),
            _resident((1, D_MODEL)),
            _resident_layer((D_MODEL, QKV_DIM), layer),
            _resident((1, QKV_DIM)),
            _resident_layer((Q_DIM, D_MODEL), layer),
            _resident((1, D_MODEL)),
            _resident((N_HEADS * dt, 1)),
        ],
        out_specs=(pl.BlockSpec((n, D_MODEL), rows1),
                   pl.BlockSpec((bb, WINDOW, KV_DIM), lambda i: (i, 0, 0)),
                   pl.BlockSpec((bb, WINDOW, KV_DIM), lambda i: (i, 0, 0))),
        scratch_shapes=[pltpu.VMEM((n, Q_DIM), F32),
                        pltpu.VMEM((n, KV_DIM), F32),
                        pltpu.VMEM((n, KV_DIM), F32),
                        pltpu.VMEM((SAMPLE_UNROLL, SAMPLE_KEYS, KV_DIM), BF16),
                        pltpu.VMEM((SAMPLE_UNROLL, SAMPLE_KEYS, KV_DIM), BF16),
                        pltpu.VMEM((n, Q_DIM), F32)],
        input_output_aliases={0: 0},
        compiler_params=pltpu.CompilerParams(
            dimension_semantics=("arbitrary",), vmem_limit_bytes=VMEM_LIMIT),
        name="attn_sample",
    )(x, cache_k, cache_v, cos, sin, g.reshape(1, D_MODEL), w_qkv, b_qkv.reshape(1, QKV_DIM), w_o,
      b_o.reshape(1, D_MODEL), sinkcol)


def kernel(x_prompt, x_sample, state_pool, cache_k, cache_v, norm_ffn1, ffn1_w_in, ffn1_w_out, norm_mix,
           norm_ffn2, ffn2_w_in, ffn2_w_out, pool_w, pool_scale, attn_w_qkv, attn_b_qkv, attn_w_o, attn_b_o,
           attn_sinks, norm_final):
    batch, seq, _ = x_prompt.shape
    dec_batch, dec_seq, _ = x_sample.shape
    depth = norm_ffn1.shape[0]
    n_mixers = 2
    w_in, w_out = ffn1_w_in, ffn1_w_out
    pw = pool_w.astype(BF16)
    wqkv, wo = attn_w_qkv.astype(BF16), attn_w_o.astype(BF16)
    n_attn = cache_k.shape[0]
    ck_t = jnp.transpose(cache_k, (0, 1, 3, 4, 2)).reshape(n_attn, dec_batch, KV_DIM, WINDOW)
    cv_t = jnp.transpose(cache_v, (0, 1, 3, 4, 2)).reshape(n_attn, dec_batch, KV_DIM, WINDOW)
    ck = cv = None
    state_t = jnp.swapaxes(state_pool, 1, 2)

    n_p, n_s = batch * seq, dec_batch * dec_seq
    x = (x_prompt.reshape(n_p, D_MODEL), x_sample.reshape(n_s, D_MODEL))
    pool_p, pool_s, kp_l, vp_l, ks_l, vs_l = [], [], [], [], [], []
    for i in range(depth):
        j = i // n_mixers
        is_attn = i % n_mixers == 1
        raw = 0 if i == 0 else None
        if is_attn:
            x, w_in, w_out, cv = _ffn(x, norm_ffn1[i], w_in, w_out, n_p, next_weights=(ffn2_w_in, ffn2_w_out, i),
                                      cache_t=(cv_t, j), raw_layer=raw, tm=FFN_TILE)
        else:
            x, w_in, w_out, state = _ffn(x, norm_ffn1[i], w_in, w_out, n_p, next_weights=(ffn2_w_in, ffn2_w_out, i),
                                         state_t=(state_t, j), raw_layer=raw, tm=FFN_TILE)
        if not is_attn:
            x, sp = _pool(x, None, norm_mix[i], pw, pool_scale[j], j, row0=0, n_seq=batch, t=seq,
                          start_pos=0, bb=1, tt=1024)
            x, ss = _pool(x, state, norm_mix[i], pw, pool_scale[j], j, row0=n_p, n_seq=dec_batch, t=dec_seq,
                          start_pos=PAST_LEN, bb=32, tt=dec_seq)
            pool_p.append(sp)
            pool_s.append(ss)
        else:
            x, kp, vp = _attn_prompt(x, norm_mix[i], wqkv, attn_b_qkv[j], wo, attn_b_o[j], attn_sinks[j], j,
                                     n_seq=batch, t=seq, nblk=ATTN_BLOCKS_PER_STEP)
            if ck is None:
                ck = jnp.swapaxes(ck_t[j], 1, 2)
            x, kn, vn = _attn_sample(x, ck, cv, norm_mix[i], wqkv, attn_b_qkv[j], wo, attn_b_o[j],
                                     attn_sinks[j], j, row0=n_p, dt=dec_seq, bb=32)
            ck = None
            kp_l.append(kp.reshape(batch, WINDOW, N_KV_HEADS, HEAD_DIM))
            vp_l.append(vp.reshape(batch, WINDOW, N_KV_HEADS, HEAD_DIM))
            ks_l.append(kn.reshape(dec_batch, WINDOW, N_KV_HEADS, HEAD_DIM))
            vs_l.append(vn.reshape(dec_batch, WINDOW, N_KV_HEADS, HEAD_DIM))
        if i == depth - 1:
            xp, xs = _ffn(x, norm_ffn2[i], w_in, w_out, n_p, final_gain=norm_final, split_out=True, tm=FFN_TILE)
        elif (i + 1) % n_mixers == 1:
            x, w_in, w_out, ck = _ffn(x, norm_ffn2[i], w_in, w_out, n_p,
                                      next_weights=(ffn1_w_in, ffn1_w_out, i + 1),
                                      cache_t=(ck_t, (i + 1) // n_mixers), tm=FFN_TILE)
        else:
            x, w_in, w_out = _ffn(x, norm_ffn2[i], w_in, w_out, n_p,
                                  next_weights=(ffn1_w_in, ffn1_w_out, i + 1), tm=FFN_TILE)
    return (xp.reshape(batch, seq, D_MODEL), xs.reshape(dec_batch, dec_seq, D_MODEL),
            jnp.stack(pool_p), jnp.stack(pool_s), jnp.stack(kp_l), jnp.stack(vp_l),
            jnp.stack(ks_l), jnp.stack(vs_l))
```

```python
import functools

import jax
import jax.numpy as jnp
from jax import lax
from jax.experimental import pallas as pl
from jax.experimental.pallas import tpu as pltpu

D_MODEL = 1024
D_FF = 2816
POOL_WINDOWS = (2, 4, 8, 16)
POOL_GROUP_DIM = D_MODEL // len(POOL_WINDOWS)
POOL_PREFIX = max(POOL_WINDOWS) - 1
N_HEADS = 16
N_KV_HEADS = 4
HEAD_DIM = 64
GROUP = N_HEADS // N_KV_HEADS
WINDOW = 128
ROPE_THETA = 10000.0
Q_DIM = N_HEADS * HEAD_DIM
KV_DIM = N_KV_HEADS * HEAD_DIM
QKV_DIM = Q_DIM + 2 * KV_DIM
RMS_EPS = 1e-6
NEG_INF = -1e30
PAST_LEN = 8192

LANES = 128
SUBLANES = 8
HEADS_PER_VREG = LANES // HEAD_DIM
Q_COLS = Q_DIM // LANES
KV_COLS = KV_DIM // LANES
POOL_HALO = 16
FF_CHUNK = 256
FFN_TILE = 512
FFN_SUB_ROWS = 256
RELAYOUT_SEQS = 4
RESTATE_SEQS = 8
WEIGHT_CAST_STEPS = 16
ATTN_BLOCKS_PER_STEP = 8
ATTN_PROJ_BLOCKS = 2
PROJ_PIECES = QKV_DIM // (2 * LANES)
PROJ_AT_GROUP = 5
QK_LOOKAHEAD = 3
LOG2_E = 1.4426950408889634
QK_SCALE_LOG2 = HEAD_DIM ** -0.5 * LOG2_E
VMEM_LIMIT = 52 * 1024 * 1024

F32 = jnp.float32
BF16 = jnp.bfloat16


def _rms(x, g):
    ms = jnp.mean(x * x, axis=-1, keepdims=True)
    return x * lax.rsqrt(ms + RMS_EPS) * g


def _resident(shape):
    nd = len(shape)
    return pl.BlockSpec(shape, lambda *_: (0,) * nd, pipeline_mode=pl.Buffered(1))


def _resident_layer(shape, layer):
    nd = len(shape)
    return pl.BlockSpec((None,) + shape, lambda *_: (layer,) + (0,) * nd, pipeline_mode=pl.Buffered(1))


def _zero_bits_of(v):
    bits = lax.bitcast_convert_type(v, jnp.uint32)
    return (bits >> 16) >> 16


def _after(v, zero_bits):
    reps = (v.shape[0] // zero_bits.shape[0], v.shape[1] // zero_bits.shape[1])
    bits = lax.bitcast_convert_type(v, jnp.uint32) | jnp.tile(zero_bits, reps)
    return lax.bitcast_convert_type(bits, F32)


def _ffn_kernel(*refs, n_sample_tiles, split_in, split_out, final_norm, convert_next, relayout, restate,
                raw_layer):
    refs = list(refs)
    if split_in:
        xp_ref, xs_ref = refs.pop(0), refs.pop(0)
    else:
        x_ref = refs.pop(0)
    g_ref, win_ref, wout_ref = refs[:3]
    del refs[:3]
    gf_ref = refs.pop(0) if final_norm else None
    if convert_next:
        nwin_ref, nwout_ref = refs.pop(0), refs.pop(0)
    if relayout:
        ri_ref = refs.pop(0)
    if restate:
        si_ref = refs.pop(0)
    if split_out:
        op_ref, os_ref = refs.pop(0), refs.pop(0)
    else:
        op_ref = refs.pop(0)
    if convert_next:
        cwin_ref, cwout_ref = refs.pop(0), refs.pop(0)
    if relayout:
        ro_ref = refs.pop(0)
    if restate:
        so_ref = refs.pop(0)
    h_ref = refs.pop(0)
    if raw_layer is not None:
        win_hbm, wout_hbm = win_ref, wout_ref
        win_ref, wout_ref, stage_in, stage_out, sem = refs
        slab_cols = stage_in.shape[2]
        slab_rows = stage_out.shape[1]
        n_slabs = 2 * D_FF // slab_cols

        def in_copy(c):
            return pltpu.make_async_copy(win_hbm.at[raw_layer, :, pl.ds(c * slab_cols, slab_cols)],
                                         stage_in.at[c % 2], sem.at[0, c % 2])

        def out_copy(c):
            return pltpu.make_async_copy(wout_hbm.at[raw_layer, pl.ds(c * slab_rows, slab_rows), :],
                                         stage_out.at[c % 2], sem.at[1, c % 2])

        @pl.when(pl.program_id(0) == 0)
        def _():
            in_copy(0).start()
            out_copy(0).start()
            for c in range(n_slabs):
                if c + 1 < n_slabs:
                    in_copy(c + 1).start()
                    out_copy(c + 1).start()
                in_copy(c).wait()
                win_ref[:, c * slab_cols:(c + 1) * slab_cols] = stage_in[c % 2].astype(BF16)
                out_copy(c).wait()
                wout_ref[c * slab_rows:(c + 1) * slab_rows, :] = stage_out[c % 2].astype(BF16)

    is_sample = pl.program_id(0) < n_sample_tiles
    x = jnp.where(is_sample, xs_ref[...], xp_ref[...]) if split_in else x_ref[...]
    xn = _rms(x, g_ref[...]).astype(BF16)
    tm = x.shape[0]
    n_chunks = D_FF // FF_CHUNK
    for c in range(n_chunks):
        lo = c * FF_CHUNK
        for r0 in range(0, tm, FFN_SUB_ROWS):
            xs_ = xn[r0:r0 + FFN_SUB_ROWS]
            gate = jnp.dot(xs_, win_ref[:, lo:lo + FF_CHUNK], preferred_element_type=F32)
            up = jnp.dot(xs_, win_ref[:, D_FF + lo:D_FF + lo + FF_CHUNK], preferred_element_type=F32)
            h_ref[r0:r0 + FFN_SUB_ROWS, lo:lo + FF_CHUNK] = (gate * jax.nn.sigmoid(gate) * up).astype(BF16)
        anchor = _zero_bits_of(gate[0:SUBLANES, 0:LANES])
        if convert_next:
            cw = 2 * D_FF // n_chunks
            cwin_ref[:, c * cw:(c + 1) * cw] = _after(nwin_ref[:, c * cw:(c + 1) * cw], anchor).astype(BF16)
            cr = cwout_ref.shape[0] // n_chunks
            cwout_ref[c * cr:(c + 1) * cr, :] = _after(nwout_ref[c * cr:(c + 1) * cr, :], anchor).astype(BF16)
        if relayout and c < ri_ref.shape[0]:
            ro_ref[c] = _after(ri_ref[c], anchor).T
        if restate:
            for p in range(c, POOL_PREFIX, n_chunks):
                so_ref[:, p, :] = _after(si_ref[p], anchor)
    ys = []
    for r0 in range(0, tm, FFN_SUB_ROWS):
        rows = slice(r0, r0 + FFN_SUB_ROWS)
        yr = x[rows] + 0.5 * jnp.dot(h_ref[rows, :], wout_ref[...], preferred_element_type=F32)
        ys.append(_rms(yr, gf_ref[...]) if final_norm else yr)
    y = jnp.concatenate(ys, axis=0)

    op_ref[...] = y
    if split_out:
        @pl.when(is_sample)
        def _():
            os_ref[...] = op_ref[...]


def _ffn(x, g, w_in, w_out, n_prompt_rows, final_gain=None, next_weights=None, cache_t=None, state_t=None, *,
         raw_layer=None, split_out=False, tm):
    split_in = isinstance(x, tuple)
    n_rows = sum(a.shape[0] for a in x) if split_in else x.shape[0]
    n_p, n_s = n_prompt_rows // tm, (n_rows - n_prompt_rows) // tm
    final_norm = final_gain is not None
    convert_next = next_weights is not None
    relayout = cache_t is not None
    restate = state_t is not None

    def sample_tile(i):
        return (jnp.minimum(i, n_s - 1), 0)

    def prompt_tile(i):
        return (jnp.maximum(i - n_s, 0), 0)

    def sample_first_tile(i):
        return (jnp.where(i < n_s, n_p + i, i - n_s), 0)

    def same_tile(i):
        return (i, 0)

    reordered = split_in or split_out
    combined_tile = sample_first_tile if reordered else same_tile
    if split_in:
        in_specs = [pl.BlockSpec((tm, D_MODEL), prompt_tile), pl.BlockSpec((tm, D_MODEL), sample_tile)]
        args = list(x)
    else:
        in_specs = [pl.BlockSpec((tm, D_MODEL), combined_tile)]
        args = [x]
    scratch_shapes = [pltpu.VMEM((tm, D_FF), BF16)]
    if raw_layer is None:
        in_specs += [_resident((1, D_MODEL)), _resident((D_MODEL, 2 * D_FF)), _resident((D_FF, D_MODEL))]
    else:
        in_specs += [_resident((1, D_MODEL)), pl.BlockSpec(memory_space=pl.ANY), pl.BlockSpec(memory_space=pl.ANY)]
        n_slabs = D_FF // FF_CHUNK
        scratch_shapes += [pltpu.VMEM((D_MODEL, 2 * D_FF), BF16), pltpu.VMEM((D_FF, D_MODEL), BF16),
                           pltpu.VMEM((2, D_MODEL, 2 * D_FF // n_slabs), F32),
                           pltpu.VMEM((2, D_FF // n_slabs, D_MODEL), F32),
                           pltpu.SemaphoreType.DMA((2, 2))]
    args += [g.reshape(1, D_MODEL), w_in, w_out]
    if split_out:
        out_shape = [jax.ShapeDtypeStruct((n_p * tm, D_MODEL), F32), jax.ShapeDtypeStruct((n_s * tm, D_MODEL), F32)]
        out_specs = [pl.BlockSpec((tm, D_MODEL), prompt_tile), pl.BlockSpec((tm, D_MODEL), sample_tile)]
    else:
        out_shape = [jax.ShapeDtypeStruct((n_rows, D_MODEL), F32)]
        out_specs = [pl.BlockSpec((tm, D_MODEL), combined_tile)]
    if final_norm:
        in_specs.append(_resident((1, D_MODEL)))
        args.append(final_gain.reshape(1, D_MODEL))
    if convert_next:
        nw_in, nw_out, layer = next_weights
        rows_in, rows_out = D_MODEL // WEIGHT_CAST_STEPS, D_FF // WEIGHT_CAST_STEPS

        def slab(i):
            return (jnp.minimum(i, WEIGHT_CAST_STEPS - 1), 0)

        in_specs += [pl.BlockSpec((None, rows_in, 2 * D_FF), lambda i: (layer,) + slab(i)),
                     pl.BlockSpec((None, rows_out, D_MODEL), lambda i: (layer,) + slab(i))]
        args += [nw_in, nw_out]
        out_shape += [jax.ShapeDtypeStruct((D_MODEL, 2 * D_FF), BF16), jax.ShapeDtypeStruct((D_FF, D_MODEL), BF16)]
        out_specs += [pl.BlockSpec((rows_in, 2 * D_FF), slab), pl.BlockSpec((rows_out, D_MODEL), slab)]
    if relayout:
        cache, cache_layer = cache_t
        n_seq = cache.shape[1]
        last_group = n_seq // RELAYOUT_SEQS - 1

        def seq_group(i):
            return (jnp.minimum(i, last_group), 0, 0)

        in_specs.append(pl.BlockSpec((None, RELAYOUT_SEQS, KV_DIM, WINDOW), lambda i: (cache_layer,) + seq_group(i)))
        args.append(cache)
        out_shape.append(jax.ShapeDtypeStruct((n_seq, WINDOW, KV_DIM), F32))
        out_specs.append(pl.BlockSpec((RELAYOUT_SEQS, WINDOW, KV_DIM), seq_group))
    if restate:
        state, state_layer = state_t
        n_seq = state.shape[2]
        last_state_group = n_seq // RESTATE_SEQS - 1

        def state_group(i):
            return jnp.minimum(i, last_state_group)

        in_specs.append(pl.BlockSpec((None, POOL_PREFIX, RESTATE_SEQS, D_MODEL),
                                     lambda i: (state_layer, 0, state_group(i), 0)))
        args.append(state)
        out_shape.append(jax.ShapeDtypeStruct((n_seq, POOL_PREFIX, D_MODEL), F32))
        out_specs.append(pl.BlockSpec((RESTATE_SEQS, POOL_PREFIX, D_MODEL), lambda i: (state_group(i), 0, 0)))
    return pl.pallas_call(
        functools.partial(_ffn_kernel, n_sample_tiles=n_s, split_in=split_in, split_out=split_out,
                          final_norm=final_norm, convert_next=convert_next, relayout=relayout, restate=restate,
                          raw_layer=raw_layer),
        out_shape=tuple(out_shape),
        grid=(n_p + n_s,),
        in_specs=in_specs,
        out_specs=tuple(out_specs),
        scratch_shapes=scratch_shapes,
        compiler_params=pltpu.CompilerParams(
            dimension_semantics=("arbitrary",), vmem_limit_bytes=VMEM_LIMIT),
        name="ffn_final" if final_norm else "ffn",
    )(*args)


def _pool_kernel(x_ref, *rest, has_prefix, start_pos, bb, tt):
    rest = list(rest)
    pre_ref = rest.pop(0) if has_prefix else None
    g_ref, w_ref, sc_ref, o_ref, st_ref, ext_ref = rest
    ti = pl.program_id(1)
    te = tt + POOL_HALO

    @pl.when(ti == 0)
    def _():
        ext_ref[:, 0:POOL_HALO, :] = jnp.zeros((bb, POOL_HALO, D_MODEL), F32)
        if has_prefix:
            ext_ref[:, POOL_HALO - POOL_PREFIX:POOL_HALO, :] = pre_ref[...]

    if tt >= POOL_HALO:
        @pl.when(ti > 0)
        def _():
            ext_ref[:, 0:POOL_HALO, :] = ext_ref[:, tt:te, :]

    x = x_ref[...].reshape(bb, tt, D_MODEL)
    u = _rms(x, g_ref[...])
    ext_ref[:, POOL_HALO:te, :] = u

    n_seen = start_pos + ti * tt + lax.broadcasted_iota(jnp.int32, (1, tt, 1), 1) + 1
    for gi, wg in enumerate(POOL_WINDOWS):
        sl = slice(gi * POOL_GROUP_DIM, (gi + 1) * POOL_GROUP_DIM)
        s = ext_ref[:, :, sl].reshape(bb * te, POOL_GROUP_DIM)
        span = 1
        while span < wg:
            s = s + pltpu.roll(s, span, axis=0)
            span *= 2
        s = s.reshape(bb, te, POOL_GROUP_DIM)[:, POOL_HALO:, :]
        cnt = jnp.minimum(n_seen, wg).astype(F32)
        p = s / cnt - u[:, :, sl]
        p2 = p.reshape(bb * tt, POOL_GROUP_DIM).astype(BF16)
        y = jnp.dot(p2, w_ref[gi], preferred_element_type=F32).reshape(bb, tt, POOL_GROUP_DIM)
        o_ref[:, sl] = (x[:, :, sl] + y * sc_ref[:, :, sl]).reshape(bb * tt, POOL_GROUP_DIM)

    @pl.when(ti == pl.num_programs(1) - 1)
    def _():
        st_ref[...] = ext_ref[:, te - POOL_PREFIX:te, :]


def _row_blocks(row0, n_seq, t, bb, tt):
    first, per_seq = row0 // (bb * tt), t // tt
    return lambda i, j: (first + i * per_seq + j, 0)


def _pool(x, prefix, g, w, scale, layer, *, row0, n_seq, t, start_pos, bb, tt):
    has_prefix = prefix is not None
    rows = _row_blocks(row0, n_seq, t, bb, tt)
    in_specs = [pl.BlockSpec((bb * tt, D_MODEL), rows)]
    args = [x]
    if has_prefix:
        in_specs.append(pl.BlockSpec((bb, POOL_PREFIX, D_MODEL), lambda i, j: (i, 0, 0)))
        args.append(prefix)
    in_specs += [
        _resident((1, 1, D_MODEL)),
        _resident_layer((len(POOL_WINDOWS), POOL_GROUP_DIM, POOL_GROUP_DIM), layer),
        _resident((1, 1, D_MODEL)),
    ]
    args += [g.reshape(1, 1, D_MODEL), w, scale.reshape(1, 1, D_MODEL)]
    return pl.pallas_call(
        functools.partial(_pool_kernel, has_prefix=has_prefix, start_pos=start_pos, bb=bb, tt=tt),
        out_shape=(jax.ShapeDtypeStruct(x.shape, F32),
                   jax.ShapeDtypeStruct((n_seq, POOL_PREFIX, D_MODEL), F32)),
        grid=(n_seq // bb, t // tt),
        in_specs=in_specs,
        out_specs=(pl.BlockSpec((bb * tt, D_MODEL), rows),
                   pl.BlockSpec((bb, POOL_PREFIX, D_MODEL), lambda i, j: (i, 0, 0))),
        scratch_shapes=[pltpu.VMEM((bb, tt + POOL_HALO, D_MODEL), F32)],
        input_output_aliases={0: 0},
        compiler_params=pltpu.CompilerParams(
            dimension_semantics=("arbitrary", "arbitrary"), vmem_limit_bytes=VMEM_LIMIT),
        name="pool_sample" if has_prefix else "pool_prompt",
    )(*args)


def _rope_tables(pos):
    half = HEAD_DIM // 2
    lane = jnp.arange(LANES)
    inv = ROPE_THETA ** (-(lane % half).astype(F32) / half)
    ang = pos.astype(F32)[:, None] * inv[None, :]
    sign = jnp.where(lane % HEAD_DIM < half, -1.0, 1.0).astype(F32)
    return jnp.cos(ang), jnp.sin(ang) * sign[None, :]


def _rope_col(xc, cos, sin, first_half):
    half = HEAD_DIM // 2
    partner = jnp.where(first_half, pltpu.roll(xc, LANES - half, axis=1), pltpu.roll(xc, half, axis=1))
    return xc * cos + partner * sin


def _head_half_variants(col, lo_half):
    sw = pltpu.roll(col, HEAD_DIM, axis=1)
    zero = jnp.zeros_like(col)
    return {
        (0, 0): jnp.where(lo_half, col, zero),
        (0, 1): jnp.where(lo_half, zero, sw),
        (1, 0): jnp.where(lo_half, sw, zero),
        (1, 1): jnp.where(lo_half, zero, col),
    }


def _softmax_with_sink(s, sink):
    m = jnp.maximum(jnp.max(s, axis=-1, keepdims=True), sink)
    e = jnp.exp2(s - m)
    denom = jnp.sum(e, axis=-1, keepdims=True) + jnp.exp2(sink - m)
    return e.astype(BF16), 1.0 / denom


def _attn_prompt_kernel(x_ref, cos_ref, sin_ref, g_ref, wqkv_ref, bqkv_ref, wo_ref, bo_ref, sink_ref,
                        o_ref, kout_ref, vout_ref, kt_ref, vlo_ref, vhi_ref, *, nblk):
    j = pl.program_id(1)
    w = WINDOW
    r = nblk * w

    @pl.when(j == 0)
    def _():
        kt_ref[...] = jnp.zeros(kt_ref.shape, BF16)
        vlo_ref[:, 0:w, 0:LANES] = jnp.zeros((N_KV_HEADS, w, LANES), BF16)
        vhi_ref[:, 0:w, 0:LANES] = jnp.zeros((N_KV_HEADS, w, LANES), BF16)
        keys = vlo_ref.shape[1]
        ones_lo = (lax.broadcasted_iota(jnp.int32, (N_KV_HEADS, keys, LANES), 2) < HEAD_DIM).astype(BF16)
        vlo_ref[:, :, LANES:] = ones_lo
        vhi_ref[:, :, LANES:] = 1 - ones_lo

    @pl.when(j > 0)
    def _():
        kt_ref[:, :, 0:w] = kt_ref[:, :, r:r + w]
        vlo_ref[:, 0:w, 0:LANES] = vlo_ref[:, r:r + w, 0:LANES]
        vhi_ref[:, 0:w, 0:LANES] = vhi_ref[:, r:r + w, 0:LANES]

    rc = ATTN_PROJ_BLOCKS * w
    n_chunks = nblk // ATTN_PROJ_BLOCKS
    lane = lax.broadcasted_iota(jnp.int32, (rc, LANES), 1)
    first_half = (lane % HEAD_DIM) < (HEAD_DIM // 2)
    lo_half = lane < HEAD_DIM
    lo_half_2w = lax.broadcasted_iota(jnp.int32, (2 * w, LANES), 1) < HEAD_DIM

    u_of, qcols_of = {}, {}

    def project_piece(ch, t):
        rows = slice(ch * rc, (ch + 1) * rc)
        if t == 0:
            u_of[ch] = _rms(x_ref[rows, :], g_ref[...]).astype(BF16)
            qcols_of[ch] = []
        cols = slice(t * 2 * LANES, (t + 1) * 2 * LANES)
        tile = jnp.dot(u_of[ch], wqkv_ref[:, cols], preferred_element_type=F32) + bqkv_ref[:, cols]
        halves = [tile[:, :LANES], tile[:, LANES:]]
        if t < Q_COLS // 2:
            cos, sin = cos_ref[rows, :], sin_ref[rows, :]
            for hc in halves:
                qcols_of[ch].append((_rope_col(hc, cos, sin, first_half) * QK_SCALE_LOG2).astype(BF16))
            return
        is_k = t == Q_COLS // 2
        if is_k:
            cos, sin = cos_ref[rows, :], sin_ref[rows, :]
            halves = [_rope_col(hc, cos, sin, first_half) for hc in halves]
        krows = slice(w + ch * rc, w + (ch + 1) * rc)
        for c, col in enumerate(halves):
            if ch == n_chunks - 1:
                out_ref = kout_ref if is_k else vout_ref
                out_ref[0, :, c * LANES:(c + 1) * LANES] = col[rc - w:]
            if is_k:
                col_t = col.T.astype(BF16)
                for hb in range(HEADS_PER_VREG):
                    kvh = c * HEADS_PER_VREG + hb
                    head_t = col_t[hb * HEAD_DIM:(hb + 1) * HEAD_DIM, :]
                    for a in range(HEADS_PER_VREG):
                        kt_ref[kvh * HEADS_PER_VREG + a, a * HEAD_DIM:(a + 1) * HEAD_DIM, krows] = head_t
            else:
                var = _head_half_variants(col, lo_half)
                for hb in range(HEADS_PER_VREG):
                    kvh = c * HEADS_PER_VREG + hb
                    vlo_ref[kvh, krows, 0:LANES] = var[(hb, 0)].astype(BF16)
                    vhi_ref[kvh, krows, 0:LANES] = var[(hb, 1)].astype(BF16)

    for t in range(PROJ_PIECES):
        project_piece(0, t)

    row = lax.broadcasted_iota(jnp.int32, (2 * w, 2 * w), 0) % w
    col = lax.broadcasted_iota(jnp.int32, (2 * w, 2 * w), 1)
    band = (col > row) & (col <= row + w)
    first_key = jnp.where(j > 0, 0, w)
    band_first = band & (col >= first_key)
    top_rows = lax.broadcasted_iota(jnp.int32, (2 * w, 1), 0) < w

    groups = [(ch, kvh, bl, a) for ch in range(n_chunks) for kvh in range(N_KV_HEADS)
              for bl in range(ATTN_PROJ_BLOCKS) for a in range(HEADS_PER_VREG)]

    def scores(ch, kvh, bl, a):
        qcols = qcols_of[ch]
        qrows = slice(bl * w, (bl + 1) * w)
        blk = ch * ATTN_PROJ_BLOCKS + bl
        qop = jnp.concatenate([qcols[2 * kvh][qrows], qcols[2 * kvh + 1][qrows]], axis=0)
        return jnp.dot(qop, kt_ref[kvh * HEADS_PER_VREG + a, :, blk * w:(blk + 2) * w],
                       preferred_element_type=F32)

    groups_per_chunk = len(groups) // n_chunks
    pending = [scores(*g) for g in groups[:QK_LOOKAHEAD]]
    es, sink_terms, o_parts, acc = [], [], [], None
    for gi, (ch, kvh, bl, a) in enumerate(groups):
        gl = gi % groups_per_chunk
        if 0 <= gl - PROJ_AT_GROUP < PROJ_PIECES and ch + 1 < n_chunks:
            project_piece(ch + 1, gl - PROJ_AT_GROUP)
        if gi + QK_LOOKAHEAD < len(groups):
            pending.append(scores(*groups[gi + QK_LOOKAHEAD]))
        blk = ch * ATTN_PROJ_BLOCKS + bl
        s = jnp.where(band_first if blk == 0 else band, pending[gi], NEG_INF)
        pending[gi] = None
        h0 = kvh * GROUP + a
        sink = jnp.where(top_rows, sink_ref[h0] * LOG2_E, sink_ref[h0 + HEADS_PER_VREG] * LOG2_E)
        m = jnp.maximum(jnp.max(s, axis=-1, keepdims=True), sink)
        es.append(jnp.exp2(s - m).astype(BF16))
        sink_terms.append(jnp.exp2(sink - m))
        if a < HEADS_PER_VREG - 1:
            continue
        krows = slice(blk * w, (blk + 2) * w)
        vstack = jnp.concatenate([vlo_ref[kvh, krows, :], vhi_ref[kvh, krows, :]], axis=0)
        res = jnp.dot(jnp.concatenate(es, axis=1), vstack, preferred_element_type=F32)
        den = res[:, LANES:] + jnp.where(lo_half_2w, sink_terms[0], sink_terms[1])
        o = res[:, :LANES] * (1.0 / den)
        o_parts.append(jnp.concatenate([o[:w], o[w:]], axis=1).astype(BF16))
        es, sink_terms = [], []
        if bl < ATTN_PROJ_BLOCKS - 1:
            continue
        part = jnp.dot(jnp.concatenate(o_parts, axis=0),
                       wo_ref[kvh * GROUP * HEAD_DIM:(kvh + 1) * GROUP * HEAD_DIM, :],
                       preferred_element_type=F32)
        o_parts = []
        acc = part if acc is None else acc + part
        if kvh == N_KV_HEADS - 1:
            rows = slice(ch * rc, (ch + 1) * rc)
            o_ref[rows, :] = x_ref[rows, :] + acc + bo_ref[...]
            acc = None


def _attn_prompt(x, g, w_qkv, b_qkv, w_o, b_o, sinks, layer, *, n_seq, t, nblk):
    b = n_seq
    r = nblk * WINDOW
    cos, sin = _rope_tables(jnp.arange(t))
    keys = (nblk + 1) * WINDOW
    rows = _row_blocks(0, n_seq, t, 1, r)
    return pl.pallas_call(
        functools.partial(_attn_prompt_kernel, nblk=nblk),
        out_shape=(jax.ShapeDtypeStruct(x.shape, F32),
                   jax.ShapeDtypeStruct((b, WINDOW, KV_DIM), F32),
                   jax.ShapeDtypeStruct((b, WINDOW, KV_DIM), F32)),
        grid=(b, t // r),
        in_specs=[
            pl.BlockSpec((r, D_MODEL), rows),
            pl.BlockSpec((r, LANES), lambda i, j: (j, 0)),
            pl.BlockSpec((r, LANES), lambda i, j: (j, 0)),
            _resident((1, D_MODEL)),
            _resident_layer((D_MODEL, QKV_DIM), layer),
            _resident((1, QKV_DIM)),
            _resident_layer((Q_DIM, D_MODEL), layer),
            _resident((1, D_MODEL)),
            pl.BlockSpec(memory_space=pltpu.SMEM),
        ],
        out_specs=(pl.BlockSpec((r, D_MODEL), rows),
                   pl.BlockSpec((1, WINDOW, KV_DIM), lambda i, j: (i, 0, 0)),
                   pl.BlockSpec((1, WINDOW, KV_DIM), lambda i, j: (i, 0, 0))),
        scratch_shapes=[pltpu.VMEM((2 * N_KV_HEADS, LANES, keys), BF16),
                        pltpu.VMEM((N_KV_HEADS, keys, 2 * LANES), BF16),
                        pltpu.VMEM((N_KV_HEADS, keys, 2 * LANES), BF16)],
        input_output_aliases={0: 0},
        compiler_params=pltpu.CompilerParams(
            dimension_semantics=("arbitrary", "arbitrary"), vmem_limit_bytes=VMEM_LIMIT),
        name="attn_prompt",
    )(x, cos, sin, g.reshape(1, D_MODEL), w_qkv, b_qkv.reshape(1, QKV_DIM), w_o,
      b_o.reshape(1, D_MODEL), sinks)


SAMPLE_KEYS = 2 * WINDOW
SAMPLE_UNROLL = 4


def _attn_sample_kernel(x_ref, ck_ref, cv_ref, cos_ref, sin_ref, g_ref, wqkv_ref, bqkv_ref, wo_ref, bo_ref,
                        sinkcol_ref, o_ref, kout_ref, vout_ref, q_s, kn_s, vn_s, kk_s, vv_s, oa_s, *, dt):
    bb = ck_ref.shape[0]
    w = WINDOW
    x = x_ref[...]
    u = _rms(x, g_ref[...]).astype(BF16)
    qkv = jnp.dot(u, wqkv_ref[...], preferred_element_type=F32) + bqkv_ref[...]
    cos, sin = cos_ref[...], sin_ref[...]
    lane_t = lax.broadcasted_iota(jnp.int32, (bb * dt, LANES), 1)
    first_half = (lane_t % HEAD_DIM) < (HEAD_DIM // 2)
    for c in range(Q_COLS):
        qc = _rope_col(qkv[:, c * LANES:(c + 1) * LANES], cos, sin, first_half)
        q_s[:, c * LANES:(c + 1) * LANES] = qc * QK_SCALE_LOG2
    for c in range(KV_COLS):
        kn_s[:, c * LANES:(c + 1) * LANES] = _rope_col(
            qkv[:, Q_DIM + c * LANES:Q_DIM + (c + 1) * LANES], cos, sin, first_half)
    vn_s[...] = qkv[:, Q_DIM + KV_DIM:]

    kk_s[:, w + dt:, :] = jnp.zeros((SAMPLE_UNROLL, SAMPLE_KEYS - w - dt, KV_DIM), BF16)
    vv_s[:, w + dt:, :] = jnp.zeros((SAMPLE_UNROLL, SAMPLE_KEYS - w - dt, KV_DIM), BF16)

    heads_per_col = N_HEADS // KV_COLS
    qcols_per_col = heads_per_col // HEADS_PER_VREG
    rows = heads_per_col * dt
    lane8 = lax.broadcasted_iota(jnp.int32, (dt, LANES), 1)
    lo8 = lane8 < HEAD_DIM
    t_row = lax.broadcasted_iota(jnp.int32, (rows, SAMPLE_KEYS), 0) % dt
    key = lax.broadcasted_iota(jnp.int32, (rows, SAMPLE_KEYS), 1)
    valid = (key > t_row) & (key <= t_row + w)

    def stage(b, slot):
        r0 = pl.multiple_of(b * dt, dt)
        knew = kn_s[pl.ds(r0, dt), :]
        vnew = vn_s[pl.ds(r0, dt), :]
        kout_ref[b, 0:w - dt, :] = ck_ref[b, dt:w, :]
        kout_ref[b, w - dt:w, :] = knew
        vout_ref[b, 0:w - dt, :] = cv_ref[b, dt:w, :]
        vout_ref[b, w - dt:w, :] = vnew
        kk_s[slot, 0:w, :] = ck_ref[b].astype(BF16)
        kk_s[slot, w:w + dt, :] = knew.astype(BF16)
        vv_s[slot, 0:w, :] = cv_ref[b].astype(BF16)
        vv_s[slot, w:w + dt, :] = vnew.astype(BF16)

    def scores(b, slot, c):
        r0 = pl.multiple_of(b * dt, dt)
        pieces = []
        for h8 in range(heads_per_col):
            qi = c * qcols_per_col + h8 // HEADS_PER_VREG
            qcol = q_s[pl.ds(r0, dt), qi * LANES:(qi + 1) * LANES]
            a, hb = h8 % HEADS_PER_VREG, h8 // GROUP
            src = qcol if a == hb else pltpu.roll(qcol, HEAD_DIM, axis=1)
            keep = lo8 if hb == 0 else jnp.logical_not(lo8)
            pieces.append(jnp.where(keep, src, 0.0))
        qop = jnp.concatenate(pieces, axis=0).astype(BF16)
        return lax.dot_general(qop, kk_s[slot, :, c * LANES:(c + 1) * LANES],
                               (((1,), (1,)), ((), ())), preferred_element_type=F32)

    def place(b, c, o):
        r0 = pl.multiple_of(b * dt, dt)
        for mm in range(qcols_per_col):
            halves = []
            for a in range(HEADS_PER_VREG):
                h8 = mm * HEADS_PER_VREG + a
                hb = h8 // GROUP
                piece = o[h8 * dt:(h8 + 1) * dt, :]
                halves.append(piece if a == hb else pltpu.roll(piece, HEAD_DIM, axis=1))
            oi = c * qcols_per_col + mm
            oa_s[pl.ds(r0, dt), oi * LANES:(oi + 1) * LANES] = jnp.where(lo8, halves[0], halves[1])

    def body(gidx, carry):
        work = [(gidx * SAMPLE_UNROLL + slot, slot, c) for slot in range(SAMPLE_UNROLL) for c in range(KV_COLS)]
        for slot in range(SAMPLE_UNROLL):
            stage(gidx * SAMPLE_UNROLL + slot, slot)
        ss = [scores(b, slot, c) for b, slot, c in work]
        sm = [_softmax_with_sink(jnp.where(valid, s, NEG_INF), sinkcol_ref[c * rows:(c + 1) * rows, :] * LOG2_E)
              for s, (b, slot, c) in zip(ss, work)]
        for (e, rinv), (b, slot, c) in zip(sm, work):
            o = jnp.dot(e, vv_s[slot, :, c * LANES:(c + 1) * LANES], preferred_element_type=F32) * rinv
            place(b, c, o)
        return carry

    lax.fori_loop(0, bb // SAMPLE_UNROLL, body, 0)

    o_ref[...] = x + jnp.dot(oa_s[...].astype(BF16), wo_ref[...], preferred_element_type=F32) + bo_ref[...]


def _attn_sample(x, cache_k, cache_v, g, w_qkv, b_qkv, w_o, b_o, sinks, layer, *, row0, dt, bb):
    b = cache_k.shape[0]
    rows = _row_blocks(row0, b, dt, bb, dt)
    rows1 = lambda i: rows(i, 0)
    cos, sin = _rope_tables(PAST_LEN + jnp.arange(dt))
    cos, sin = jnp.tile(cos, (bb, 1)), jnp.tile(sin, (bb, 1))
    sinkcol = jnp.repeat(sinks, dt).reshape(N_HEADS * dt, 1)
    n = bb * dt
    return pl.pallas_call(
        functools.partial(_attn_sample_kernel, dt=dt),
        out_shape=(jax.ShapeDtypeStruct(x.shape, F32),
                   jax.ShapeDtypeStruct((b, WINDOW, KV_DIM), F32),
                   jax.ShapeDtypeStruct((b, WINDOW, KV_DIM), F32)),
        grid=(b // bb,),
        in_specs=[
            pl.BlockSpec((n, D_MODEL), rows1),
            pl.BlockSpec((bb, WINDOW, KV_DIM), lambda i: (i, 0, 0)),
            pl.BlockSpec((bb, WINDOW, KV_DIM), lambda i: (i, 0, 0)),
            _resident((n, LANES)),
            _resident((n, LANES)),
            _resident((1, D_MODEL)),
            _resident_layer((D_MODEL, QKV_DIM), layer),
            _resident((1, QKV_DIM)),
            _resident_layer((Q_DIM, D_MODEL), layer),
            _resident((1, D_MODEL)),
            _resident((N_HEADS * dt, 1)),
        ],
        out_specs=(pl.BlockSpec((n, D_MODEL), rows1),
                   pl.BlockSpec((bb, WINDOW, KV_DIM), lambda i: (i, 0, 0)),
                   pl.BlockSpec((bb, WINDOW, KV_DIM), lambda i: (i, 0, 0))),
        scratch_shapes=[pltpu.VMEM((n, Q_DIM), F32),
                        pltpu.VMEM((n, KV_DIM), F32),
                        pltpu.VMEM((n, KV_DIM), F32),
                        pltpu.VMEM((SAMPLE_UNROLL, SAMPLE_KEYS, KV_DIM), BF16),
                        pltpu.VMEM((SAMPLE_UNROLL, SAMPLE_KEYS, KV_DIM), BF16),
                        pltpu.VMEM((n, Q_DIM), F32)],
        input_output_aliases={0: 0},
        compiler_params=pltpu.CompilerParams(
            dimension_semantics=("arbitrary",), vmem_limit_bytes=VMEM_LIMIT),
        name="attn_sample",
    )(x, cache_k, cache_v, cos, sin, g.reshape(1, D_MODEL), w_qkv, b_qkv.reshape(1, QKV_DIM), w_o,
      b_o.reshape(1, D_MODEL), sinkcol)


def kernel(x_prompt, x_sample, state_pool, cache_k, cache_v, norm_ffn1, ffn1_w_in, ffn1_w_out, norm_mix,
           norm_ffn2, ffn2_w_in, ffn2_w_out, pool_w, pool_scale, attn_w_qkv, attn_b_qkv, attn_w_o, attn_b_o,
           attn_sinks, norm_final):
    batch, seq, _ = x_prompt.shape
    dec_batch, dec_seq, _ = x_sample.shape
    depth = norm_ffn1.shape[0]
    n_mixers = 2
    w_in, w_out = ffn1_w_in, ffn1_w_out
    pw = pool_w.astype(BF16)
    wqkv, wo = attn_w_qkv.astype(BF16), attn_w_o.astype(BF16)
    n_attn = cache_k.shape[0]
    ck_t = jnp.transpose(cache_k, (0, 1, 3, 4, 2)).reshape(n_attn, dec_batch, KV_DIM, WINDOW)
    cv_t = jnp.transpose(cache_v, (0, 1, 3, 4, 2)).reshape(n_attn, dec_batch, KV_DIM, WINDOW)
    ck = cv = None
    state_t = jnp.swapaxes(state_pool, 1, 2)

    n_p, n_s = batch * seq, dec_batch * dec_seq
    x = (x_prompt.reshape(n_p, D_MODEL), x_sample.reshape(n_s, D_MODEL))
    pool_p, pool_s, kp_l, vp_l, ks_l, vs_l = [], [], [], [], [], []
    for i in range(depth):
        j = i // n_mixers
        is_attn = i % n_mixers == 1
        raw = 0 if i == 0 else None
        if is_attn:
            x, w_in, w_out, cv = _ffn(x, norm_ffn1[i], w_in, w_out, n_p, next_weights=(ffn2_w_in, ffn2_w_out, i),
                                      cache_t=(cv_t, j), raw_layer=raw, tm=FFN_TILE)
        else:
            x, w_in, w_out, state = _ffn(x, norm_ffn1[i], w_in, w_out, n_p, next_weights=(ffn2_w_in, ffn2_w_out, i),
                                         state_t=(state_t, j), raw_layer=raw, tm=FFN_TILE)
        if not is_attn:
            x, sp = _pool(x, None, norm_mix[i], pw, pool_scale[j], j, row0=0, n_seq=batch, t=seq,
                          start_pos=0, bb=1, tt=1024)
            x, ss = _pool(x, state, norm_mix[i], pw, pool_scale[j], j, row0=n_p, n_seq=dec_batch, t=dec_seq,
                          start_pos=PAST_LEN, bb=32, tt=dec_seq)
            pool_p.append(sp)
            pool_s.append(ss)
        else:
            x, kp, vp = _attn_prompt(x, norm_mix[i], wqkv, attn_b_qkv[j], wo, attn_b_o[j], attn_sinks[j], j,
                                     n_seq=batch, t=seq, nblk=ATTN_BLOCKS_PER_STEP)
            if ck is None:
                ck = jnp.swapaxes(ck_t[j], 1, 2)
            x, kn, vn = _attn_sample(x, ck, cv, norm_mix[i], wqkv, attn_b_qkv[j], wo, attn_b_o[j],
                                     attn_sinks[j], j, row0=n_p, dt=dec_seq, bb=32)
            ck = None
            kp_l.append(kp.reshape(batch, WINDOW, N_KV_HEADS, HEAD_DIM))
            vp_l.append(vp.reshape(batch, WINDOW, N_KV_HEADS, HEAD_DIM))
            ks_l.append(kn.reshape(dec_batch, WINDOW, N_KV_HEADS, HEAD_DIM))
            vs_l.append(vn.reshape(dec_batch, WINDOW, N_KV_HEADS, HEAD_DIM))
        if i == depth - 1:
            xp, xs = _ffn(x, norm_ffn2[i], w_in, w_out, n_p, final_gain=norm_final, split_out=True, tm=FFN_TILE)
        elif (i + 1) % n_mixers == 1:
            x, w_in, w_out, ck = _ffn(x, norm_ffn2[i], w_in, w_out, n_p,
                                      next_weights=(ffn1_w_in, ffn1_w_out, i + 1),
                                      cache_t=(ck_t, (i + 1) // n_mixers), tm=FFN_TILE)
        else:
            x, w_in, w_out = _ffn(x, norm_ffn2[i], w_in, w_out, n_p,
                                  next_weights=(ffn1_w_in, ffn1_w_out, i + 1), tm=FFN_TILE)
    return (xp.reshape(batch, seq, D_MODEL), xs.reshape(dec_batch, dec_seq, D_MODEL),
            jnp.stack(pool_p), jnp.stack(pool_s), jnp.stack(kp_l), jnp.stack(vp_l),
            jnp.stack(ks_l), jnp.stack(vs_l))
```

```python
import functools

import jax
import jax.numpy as jnp
from jax import lax
from jax.experimental import pallas as pl
from jax.experimental.pallas import tpu as pltpu

D_MODEL = 1024
D_FF = 2816
POOL_WINDOWS = (2, 4, 8, 16)
POOL_GROUP_DIM = D_MODEL // len(POOL_WINDOWS)
POOL_PREFIX = max(POOL_WINDOWS) - 1
N_HEADS = 16
N_KV_HEADS = 4
HEAD_DIM = 64
GROUP = N_HEADS // N_KV_HEADS
WINDOW = 128
ROPE_THETA = 10000.0
Q_DIM = N_HEADS * HEAD_DIM
KV_DIM = N_KV_HEADS * HEAD_DIM
QKV_DIM = Q_DIM + 2 * KV_DIM
RMS_EPS = 1e-6
NEG_INF = -1e30
PAST_LEN = 8192

LANES = 128
SUBLANES = 8
HEADS_PER_VREG = LANES // HEAD_DIM
Q_COLS = Q_DIM // LANES
KV_COLS = KV_DIM // LANES
POOL_HALO = 16
FF_CHUNK = 256
FFN_TILE = 512
FFN_SUB_ROWS = 256
RELAYOUT_SEQS = 4
RESTATE_SEQS = 8
WEIGHT_CAST_STEPS = 16
ATTN_BLOCKS_PER_STEP = 8
ATTN_PROJ_BLOCKS = 2
PROJ_PIECES = QKV_DIM // (2 * LANES)
PROJ_AT_GROUP = 5
QK_LOOKAHEAD = 3
LOG2_E = 1.4426950408889634
QK_SCALE_LOG2 = HEAD_DIM ** -0.5 * LOG2_E
VMEM_LIMIT = 52 * 1024 * 1024

F32 = jnp.float32
BF16 = jnp.bfloat16


def _rms(x, g):
    ms = jnp.mean(x * x, axis=-1, keepdims=True)
    return x * lax.rsqrt(ms + RMS_EPS) * g


def _resident(shape):
    nd = len(shape)
    return pl.BlockSpec(shape, lambda *_: (0,) * nd, pipeline_mode=pl.Buffered(1))


def _resident_layer(shape, layer):
    nd = len(shape)
    return pl.BlockSpec((None,) + shape, lambda *_: (layer,) + (0,) * nd, pipeline_mode=pl.Buffered(1))


def _zero_bits_of(v):
    bits = lax.bitcast_convert_type(v, jnp.uint32)
    return (bits >> 16) >> 16


def _after(v, zero_bits):
    reps = (v.shape[0] // zero_bits.shape[0], v.shape[1] // zero_bits.shape[1])
    bits = lax.bitcast_convert_type(v, jnp.uint32) | jnp.tile(zero_bits, reps)
    return lax.bitcast_convert_type(bits, F32)


def _ffn_kernel(*refs, n_sample_tiles, split_in, split_out, final_norm, convert_next, relayout, restate,
                raw_layer):
    refs = list(refs)
    if split_in:
        xp_ref, xs_ref = refs.pop(0), refs.pop(0)
    else:
        x_ref = refs.pop(0)
    g_ref, win_ref, wout_ref = refs[:3]
    del refs[:3]
    gf_ref = refs.pop(0) if final_norm else None
    if convert_next:
        nwin_ref, nwout_ref = refs.pop(0), refs.pop(0)
    if relayout:
        ri_ref = refs.pop(0)
    if restate:
        si_ref = refs.pop(0)
    if split_out:
        op_ref, os_ref = refs.pop(0), refs.pop(0)
    else:
        op_ref = refs.pop(0)
    if convert_next:
        cwin_ref, cwout_ref = refs.pop(0), refs.pop(0)
    if relayout:
        ro_ref = refs.pop(0)
    if restate:
        so_ref = refs.pop(0)
    h_ref = refs.pop(0)
    if raw_layer is not None:
        win_hbm, wout_hbm = win_ref, wout_ref
        win_ref, wout_ref, stage_in, stage_out, sem = refs
        slab_cols = stage_in.shape[2]
        slab_rows = stage_out.shape[1]
        n_slabs = 2 * D_FF // slab_cols

        def in_copy(c):
            return pltpu.make_async_copy(win_hbm.at[raw_layer, :, pl.ds(c * slab_cols, slab_cols)],
                                         stage_in.at[c % 2], sem.at[0, c % 2])

        def out_copy(c):
            return pltpu.make_async_copy(wout_hbm.at[raw_layer, pl.ds(c * slab_rows, slab_rows), :],
                                         stage_out.at[c % 2], sem.at[1, c % 2])

        @pl.when(pl.program_id(0) == 0)
        def _():
            in_copy(0).start()
            out_copy(0).start()
            for c in range(n_slabs):
                if c + 1 < n_slabs:
                    in_copy(c + 1).start()
                    out_copy(c + 1).start()
                in_copy(c).wait()
                win_ref[:, c * slab_cols:(c + 1) * slab_cols] = stage_in[c % 2].astype(BF16)
                out_copy(c).wait()
                wout_ref[c * slab_rows:(c + 1) * slab_rows, :] = stage_out[c % 2].astype(BF16)

    is_sample = pl.program_id(0) < n_sample_tiles
    x = jnp.where(is_sample, xs_ref[...], xp_ref[...]) if split_in else x_ref[...]
    xn = _rms(x, g_ref[...]).astype(BF16)
    tm = x.shape[0]
    n_chunks = D_FF // FF_CHUNK
    for c in range(n_chunks):
        lo = c * FF_CHUNK
        for r0 in range(0, tm, FFN_SUB_ROWS):
            xs_ = xn[r0:r0 + FFN_SUB_ROWS]
            gate = jnp.dot(xs_, win_ref[:, lo:lo + FF_CHUNK], preferred_element_type=F32)
            up = jnp.dot(xs_, win_ref[:, D_FF + lo:D_FF + lo + FF_CHUNK], preferred_element_type=F32)
            h_ref[r0:r0 + FFN_SUB_ROWS, lo:lo + FF_CHUNK] = (gate * jax.nn.sigmoid(gate) * up).astype(BF16)
        anchor = _zero_bits_of(gate[0:SUBLANES, 0:LANES])
        if convert_next:
            cw = 2 * D_FF // n_chunks
            cwin_ref[:, c * cw:(c + 1) * cw] = _after(nwin_ref[:, c * cw:(c + 1) * cw], anchor).astype(BF16)
            cr = cwout_ref.shape[0] // n_chunks
            cwout_ref[c * cr:(c + 1) * cr, :] = _after(nwout_ref[c * cr:(c + 1) * cr, :], anchor).astype(BF16)
        if relayout and c < ri_ref.shape[0]:
            ro_ref[c] = _after(ri_ref[c], anchor).T
        if restate:
            for p in range(c, POOL_PREFIX, n_chunks):
                so_ref[:, p, :] = _after(si_ref[p], anchor)
    ys = []
    for r0 in range(0, tm, FFN_SUB_ROWS):
        rows = slice(r0, r0 + FFN_SUB_ROWS)
        yr = x[rows] + 0.5 * jnp.dot(h_ref[rows, :], wout_ref[...], preferred_element_type=F32)
        ys.append(_rms(yr, gf_ref[...]) if final_norm else yr)
    y = jnp.concatenate(ys, axis=0)

    op_ref[...] = y
    if split_out:
        @pl.when(is_sample)
        def _():
            os_ref[...] = op_ref[...]


def _ffn(x, g, w_in, w_out, n_prompt_rows, final_gain=None, next_weights=None, cache_t=None, state_t=None, *,
         raw_layer=None, split_out=False, tm):
    split_in = isinstance(x, tuple)
    n_rows = sum(a.shape[0] for a in x) if split_in else x.shape[0]
    n_p, n_s = n_prompt_rows // tm, (n_rows - n_prompt_rows) // tm
    final_norm = final_gain is not None
    convert_next = next_weights is not None
    relayout = cache_t is not None
    restate = state_t is not None

    def sample_tile(i):
        return (jnp.minimum(i, n_s - 1), 0)

    def prompt_tile(i):
        return (jnp.maximum(i - n_s, 0), 0)

    def sample_first_tile(i):
        return (jnp.where(i < n_s, n_p + i, i - n_s), 0)

    def same_tile(i):
        return (i, 0)

    reordered = split_in or split_out
    combined_tile = sample_first_tile if reordered else same_tile
    if split_in:
        in_specs = [pl.BlockSpec((tm, D_MODEL), prompt_tile), pl.BlockSpec((tm, D_MODEL), sample_tile)]
        args = list(x)
    else:
        in_specs = [pl.BlockSpec((tm, D_MODEL), combined_tile)]
        args = [x]
    scratch_shapes = [pltpu.VMEM((tm, D_FF), BF16)]
    if raw_layer is None:
        in_specs += [_resident((1, D_MODEL)), _resident((D_MODEL, 2 * D_FF)), _resident((D_FF, D_MODEL))]
    else:
        in_specs += [_resident((1, D_MODEL)), pl.BlockSpec(memory_space=pl.ANY), pl.BlockSpec(memory_space=pl.ANY)]
        n_slabs = D_FF // FF_CHUNK
        scratch_shapes += [pltpu.VMEM((D_MODEL, 2 * D_FF), BF16), pltpu.VMEM((D_FF, D_MODEL), BF16),
                           pltpu.VMEM((2, D_MODEL, 2 * D_FF // n_slabs), F32),
                           pltpu.VMEM((2, D_FF // n_slabs, D_MODEL), F32),
                           pltpu.SemaphoreType.DMA((2, 2))]
    args += [g.reshape(1, D_MODEL), w_in, w_out]
    if split_out:
        out_shape = [jax.ShapeDtypeStruct((n_p * tm, D_MODEL), F32), jax.ShapeDtypeStruct((n_s * tm, D_MODEL), F32)]
        out_specs = [pl.BlockSpec((tm, D_MODEL), prompt_tile), pl.BlockSpec((tm, D_MODEL), sample_tile)]
    else:
        out_shape = [jax.ShapeDtypeStruct((n_rows, D_MODEL), F32)]
        out_specs = [pl.BlockSpec((tm, D_MODEL), combined_tile)]
    if final_norm:
        in_specs.append(_resident((1, D_MODEL)))
        args.append(final_gain.reshape(1, D_MODEL))
    if convert_next:
        nw_in, nw_out, layer = next_weights
        rows_in, rows_out = D_MODEL // WEIGHT_CAST_STEPS, D_FF // WEIGHT_CAST_STEPS

        def slab(i):
            return (jnp.minimum(i, WEIGHT_CAST_STEPS - 1), 0)

        in_specs += [pl.BlockSpec((None, rows_in, 2 * D_FF), lambda i: (layer,) + slab(i)),
                     pl.BlockSpec((None, rows_out, D_MODEL), lambda i: (layer,) + slab(i))]
        args += [nw_in, nw_out]
        out_shape += [jax.ShapeDtypeStruct((D_MODEL, 2 * D_FF), BF16), jax.ShapeDtypeStruct((D_FF, D_MODEL), BF16)]
        out_specs += [pl.BlockSpec((rows_in, 2 * D_FF), slab), pl.BlockSpec((rows_out, D_MODEL), slab)]
    if relayout:
        cache, cache_layer = cache_t
        n_seq = cache.shape[1]
        last_group = n_seq // RELAYOUT_SEQS - 1

        def seq_group(i):
            return (jnp.minimum(i, last_group), 0, 0)

        in_specs.append(pl.BlockSpec((None, RELAYOUT_SEQS, KV_DIM, WINDOW), lambda i: (cache_layer,) + seq_group(i)))
        args.append(cache)
        out_shape.append(jax.ShapeDtypeStruct((n_seq, WINDOW, KV_DIM), F32))
        out_specs.append(pl.BlockSpec((RELAYOUT_SEQS, WINDOW, KV_DIM), seq_group))
    if restate:
        state, state_layer = state_t
        n_seq = state.shape[2]
        last_state_group = n_seq // RESTATE_SEQS - 1

        def state_group(i):
            return jnp.minimum(i, last_state_group)

        in_specs.append(pl.BlockSpec((None, POOL_PREFIX, RESTATE_SEQS, D_MODEL),
                                     lambda i: (state_layer, 0, state_group(i), 0)))
        args.append(state)
        out_shape.append(jax.ShapeDtypeStruct((n_seq, POOL_PREFIX, D_MODEL), F32))
        out_specs.append(pl.BlockSpec((RESTATE_SEQS, POOL_PREFIX, D_MODEL), lambda i: (state_group(i), 0, 0)))
    return pl.pallas_call(
        functools.partial(_ffn_kernel, n_sample_tiles=n_s, split_in=split_in, split_out=split_out,
                          final_norm=final_norm, convert_next=convert_next, relayout=relayout, restate=restate,
                          raw_layer=raw_layer),
        out_shape=tuple(out_shape),
        grid=(n_p + n_s,),
        in_specs=in_specs,
        out_specs=tuple(out_specs),
        scratch_shapes=scratch_shapes,
        compiler_params=pltpu.CompilerParams(
            dimension_semantics=("arbitrary",), vmem_limit_bytes=VMEM_LIMIT),
        name="ffn_final" if final_norm else "ffn",
    )(*args)


def _pool_kernel(x_ref, *rest, has_prefix, start_pos, bb, tt):
    rest = list(rest)
    pre_ref = rest.pop(0) if has_prefix else None
    g_ref, w_ref, sc_ref, o_ref, st_ref, ext_ref = rest
    ti = pl.program_id(1)
    te = tt + POOL_HALO

    @pl.when(ti == 0)
    def _():
        ext_ref[:, 0:POOL_HALO, :] = jnp.zeros((bb, POOL_HALO, D_MODEL), F32)
        if has_prefix:
            ext_ref[:, POOL_HALO - POOL_PREFIX:POOL_HALO, :] = pre_ref[...]

    if tt >= POOL_HALO:
        @pl.when(ti > 0)
        def _():
            ext_ref[:, 0:POOL_HALO, :] = ext_ref[:, tt:te, :]

    x = x_ref[...].reshape(bb, tt, D_MODEL)
    u = _rms(x, g_ref[...])
    ext_ref[:, POOL_HALO:te, :] = u

    n_seen = start_pos + ti * tt + lax.broadcasted_iota(jnp.int32, (1, tt, 1), 1) + 1
    for gi, wg in enumerate(POOL_WINDOWS):
        sl = slice(gi * POOL_GROUP_DIM, (gi + 1) * POOL_GROUP_DIM)
        s = ext_ref[:, :, sl].reshape(bb * te, POOL_GROUP_DIM)
        span = 1
        while span < wg:
            s = s + pltpu.roll(s, span, axis=0)
            span *= 2
        s = s.reshape(bb, te, POOL_GROUP_DIM)[:, POOL_HALO:, :]
        cnt = jnp.minimum(n_seen, wg).astype(F32)
        p = s / cnt - u[:, :, sl]
        p2 = p.reshape(bb * tt, POOL_GROUP_DIM).astype(BF16)
        y = jnp.dot(p2, w_ref[gi], preferred_element_type=F32).reshape(bb, tt, POOL_GROUP_DIM)
        o_ref[:, sl] = (x[:, :, sl] + y * sc_ref[:, :, sl]).reshape(bb * tt, POOL_GROUP_DIM)

    @pl.when(ti == pl.num_programs(1) - 1)
    def _():
        st_ref[...] = ext_ref[:, te - POOL_PREFIX:te, :]


def _row_blocks(row0, n_seq, t, bb, tt):
    first, per_seq = row0 // (bb * tt), t // tt
    return lambda i, j: (first + i * per_seq + j, 0)


def _pool(x, prefix, g, w, scale, layer, *, row0, n_seq, t, start_pos, bb, tt):
    has_prefix = prefix is not None
    rows = _row_blocks(row0, n_seq, t, bb, tt)
    in_specs = [pl.BlockSpec((bb * tt, D_MODEL), rows)]
    args = [x]
    if has_prefix:
        in_specs.append(pl.BlockSpec((bb, POOL_PREFIX, D_MODEL), lambda i, j: (i, 0, 0)))
        args.append(prefix)
    in_specs += [
        _resident((1, 1, D_MODEL)),
        _resident_layer((len(POOL_WINDOWS), POOL_GROUP_DIM, POOL_GROUP_DIM), layer),
        _resident((1, 1, D_MODEL)),
    ]
    args += [g.reshape(1, 1, D_MODEL), w, scale.reshape(1, 1, D_MODEL)]
    return pl.pallas_call(
        functools.partial(_pool_kernel, has_prefix=has_prefix, start_pos=start_pos, bb=bb, tt=tt),
        out_shape=(jax.ShapeDtypeStruct(x.shape, F32),
                   jax.ShapeDtypeStruct((n_seq, POOL_PREFIX, D_MODEL), F32)),
        grid=(n_seq // bb, t // tt),
        in_specs=in_specs,
        out_specs=(pl.BlockSpec((bb * tt, D_MODEL), rows),
                   pl.BlockSpec((bb, POOL_PREFIX, D_MODEL), lambda i, j: (i, 0, 0))),
        scratch_shapes=[pltpu.VMEM((bb, tt + POOL_HALO, D_MODEL), F32)],
        input_output_aliases={0: 0},
        compiler_params=pltpu.CompilerParams(
            dimension_semantics=("arbitrary", "arbitrary"), vmem_limit_bytes=VMEM_LIMIT),
        name="pool_sample" if has_prefix else "pool_prompt",
    )(*args)


def _rope_tables(pos):
    half = HEAD_DIM // 2
    lane = jnp.arange(LANES)
    inv = ROPE_THETA ** (-(lane % half).astype(F32) / half)
    ang = pos.astype(F32)[:, None] * inv[None, :]
    sign = jnp.where(lane % HEAD_DIM < half, -1.0, 1.0).astype(F32)
    return jnp.cos(ang), jnp.sin(ang) * sign[None, :]


def _rope_col(xc, cos, sin, first_half):
    half = HEAD_DIM // 2
    partner = jnp.where(first_half, pltpu.roll(xc, LANES - half, axis=1), pltpu.roll(xc, half, axis=1))
    return xc * cos + partner * sin


def _head_half_variants(col, lo_half):
    sw = pltpu.roll(col, HEAD_DIM, axis=1)
    zero = jnp.zeros_like(col)
    return {
        (0, 0): jnp.where(lo_half, col, zero),
        (0, 1): jnp.where(lo_half, zero, sw),
        (1, 0): jnp.where(lo_half, sw, zero),
        (1, 1): jnp.where(lo_half, zero, col),
    }


def _softmax_with_sink(s, sink):
    m = jnp.maximum(jnp.max(s, axis=-1, keepdims=True), sink)
    e = jnp.exp2(s - m)
    denom = jnp.sum(e, axis=-1, keepdims=True) + jnp.exp2(sink - m)
    return e.astype(BF16), 1.0 / denom


def _attn_prompt_kernel(x_ref, cos_ref, sin_ref, g_ref, wqkv_ref, bqkv_ref, wo_ref, bo_ref, sink_ref,
                        o_ref, kout_ref, vout_ref, kt_ref, vlo_ref, vhi_ref, *, nblk):
    j = pl.program_id(1)
    w = WINDOW
    r = nblk * w

    @pl.when(j == 0)
    def _():
        kt_ref[...] = jnp.zeros(kt_ref.shape, BF16)
        vlo_ref[:, 0:w, 0:LANES] = jnp.zeros((N_KV_HEADS, w, LANES), BF16)
        vhi_ref[:, 0:w, 0:LANES] = jnp.zeros((N_KV_HEADS, w, LANES), BF16)
        keys = vlo_ref.shape[1]
        ones_lo = (lax.broadcasted_iota(jnp.int32, (N_KV_HEADS, keys, LANES), 2) < HEAD_DIM).astype(BF16)
        vlo_ref[:, :, LANES:] = ones_lo
        vhi_ref[:, :, LANES:] = 1 - ones_lo

    @pl.when(j > 0)
    def _():
        kt_ref[:, :, 0:w] = kt_ref[:, :, r:r + w]
        vlo_ref[:, 0:w, 0:LANES] = vlo_ref[:, r:r + w, 0:LANES]
        vhi_ref[:, 0:w, 0:LANES] = vhi_ref[:, r:r + w, 0:LANES]

    rc = ATTN_PROJ_BLOCKS * w
    n_chunks = nblk // ATTN_PROJ_BLOCKS
    lane = lax.broadcasted_iota(jnp.int32, (rc, LANES), 1)
    first_half = (lane % HEAD_DIM) < (HEAD_DIM // 2)
    lo_half = lane < HEAD_DIM
    lo_half_2w = lax.broadcasted_iota(jnp.int32, (2 * w, LANES), 1) < HEAD_DIM

    u_of, qcols_of = {}, {}

    def project_piece(ch, t):
        rows = slice(ch * rc, (ch + 1) * rc)
        if t == 0:
            u_of[ch] = _rms(x_ref[rows, :], g_ref[...]).astype(BF16)
            qcols_of[ch] = []
        cols = slice(t * 2 * LANES, (t + 1) * 2 * LANES)
        tile = jnp.dot(u_of[ch], wqkv_ref[:, cols], preferred_element_type=F32) + bqkv_ref[:, cols]
        halves = [tile[:, :LANES], tile[:, LANES:]]
        if t < Q_COLS // 2:
            cos, sin = cos_ref[rows, :], sin_ref[rows, :]
            for hc in halves:
                qcols_of[ch].append((_rope_col(hc, cos, sin, first_half) * QK_SCALE_LOG2).astype(BF16))
            return
        is_k = t == Q_COLS // 2
        if is_k:
            cos, sin = cos_ref[rows, :], sin_ref[rows, :]
            halves = [_rope_col(hc, cos, sin, first_half) for hc in halves]
        krows = slice(w + ch * rc, w + (ch + 1) * rc)
        for c, col in enumerate(halves):
            if ch == n_chunks - 1:
                out_ref = kout_ref if is_k else vout_ref
                out_ref[0, :, c * LANES:(c + 1) * LANES] = col[rc - w:]
            if is_k:
                col_t = col.T.astype(BF16)
                for hb in range(HEADS_PER_VREG):
                    kvh = c * HEADS_PER_VREG + hb
                    head_t = col_t[hb * HEAD_DIM:(hb + 1) * HEAD_DIM, :]
                    for a in range(HEADS_PER_VREG):
                        kt_ref[kvh * HEADS_PER_VREG + a, a * HEAD_DIM:(a + 1) * HEAD_DIM, krows] = head_t
            else:
                var = _head_half_variants(col, lo_half)
                for hb in range(HEADS_PER_VREG):
                    kvh = c * HEADS_PER_VREG + hb
                    vlo_ref[kvh, krows, 0:LANES] = var[(hb, 0)].astype(BF16)
                    vhi_ref[kvh, krows, 0:LANES] = var[(hb, 1)].astype(BF16)

    for t in range(PROJ_PIECES):
        project_piece(0, t)

    row = lax.broadcasted_iota(jnp.int32, (2 * w, 2 * w), 0) % w
    col = lax.broadcasted_iota(jnp.int32, (2 * w, 2 * w), 1)
    band = (col > row) & (col <= row + w)
    first_key = jnp.where(j > 0, 0, w)
    band_first = band & (col >= first_key)
    top_rows = lax.broadcasted_iota(jnp.int32, (2 * w, 1), 0) < w

    groups = [(ch, kvh, bl, a) for ch in range(n_chunks) for kvh in range(N_KV_HEADS)
              for bl in range(ATTN_PROJ_BLOCKS) for a in range(HEADS_PER_VREG)]

    def scores(ch, kvh, bl, a):
        qcols = qcols_of[ch]
        qrows = slice(bl * w, (bl + 1) * w)
        blk = ch * ATTN_PROJ_BLOCKS + bl
        qop = jnp.concatenate([qcols[2 * kvh][qrows], qcols[2 * kvh + 1][qrows]], axis=0)
        return jnp.dot(qop, kt_ref[kvh * HEADS_PER_VREG + a, :, blk * w:(blk + 2) * w],
                       preferred_element_type=F32)

    groups_per_chunk = len(groups) // n_chunks
    pending = [scores(*g) for g in groups[:QK_LOOKAHEAD]]
    es, sink_terms, o_parts, acc = [], [], [], None
    for gi, (ch, kvh, bl, a) in enumerate(groups):
        gl = gi % groups_per_chunk
        if 0 <= gl - PROJ_AT_GROUP < PROJ_PIECES and ch + 1 < n_chunks:
            project_piece(ch + 1, gl - PROJ_AT_GROUP)
        if gi + QK_LOOKAHEAD < len(groups):
            pending.append(scores(*groups[gi + QK_LOOKAHEAD]))
        blk = ch * ATTN_PROJ_BLOCKS + bl
        s = jnp.where(band_first if blk == 0 else band, pending[gi], NEG_INF)
        pending[gi] = None
        h0 = kvh * GROUP + a
        sink = jnp.where(top_rows, sink_ref[h0] * LOG2_E, sink_ref[h0 + HEADS_PER_VREG] * LOG2_E)
        m = jnp.maximum(jnp.max(s, axis=-1, keepdims=True), sink)
        es.append(jnp.exp2(s - m).astype(BF16))
        sink_terms.append(jnp.exp2(sink - m))
        if a < HEADS_PER_VREG - 1:
            continue
        krows = slice(blk * w, (blk + 2) * w)
        vstack = jnp.concatenate([vlo_ref[kvh, krows, :], vhi_ref[kvh, krows, :]], axis=0)
        res = jnp.dot(jnp.concatenate(es, axis=1), vstack, preferred_element_type=F32)
        den = res[:, LANES:] + jnp.where(lo_half_2w, sink_terms[0], sink_terms[1])
        o = res[:, :LANES] * (1.0 / den)
        o_parts.append(jnp.concatenate([o[:w], o[w:]], axis=1).astype(BF16))
        es, sink_terms = [], []
        if bl < ATTN_PROJ_BLOCKS - 1:
            continue
        part = jnp.dot(jnp.concatenate(o_parts, axis=0),
                       wo_ref[kvh * GROUP * HEAD_DIM:(kvh + 1) * GROUP * HEAD_DIM, :],
                       preferred_element_type=F32)
        o_parts = []
        acc = part if acc is None else acc + part
        if kvh == N_KV_HEADS - 1:
            rows = slice(ch * rc, (ch + 1) * rc)
            o_ref[rows, :] = x_ref[rows, :] + acc + bo_ref[...]
            acc = None


def _attn_prompt(x, g, w_qkv, b_qkv, w_o, b_o, sinks, layer, *, n_seq, t, nblk):
    b = n_seq
    r = nblk * WINDOW
    cos, sin = _rope_tables(jnp.arange(t))
    keys = (nblk + 1) * WINDOW
    rows = _row_blocks(0, n_seq, t, 1, r)
    return pl.pallas_call(
        functools.partial(_attn_prompt_kernel, nblk=nblk),
        out_shape=(jax.ShapeDtypeStruct(x.shape, F32),
                   jax.ShapeDtypeStruct((b, WINDOW, KV_DIM), F32),
                   jax.ShapeDtypeStruct((b, WINDOW, KV_DIM), F32)),
        grid=(b, t // r),
        in_specs=[
            pl.BlockSpec((r, D_MODEL), rows),
            pl.BlockSpec((r, LANES), lambda i, j: (j, 0)),
            pl.BlockSpec((r, LANES), lambda i, j: (j, 0)),
            _resident((1, D_MODEL)),
            _resident_layer((D_MODEL, QKV_DIM), layer),
            _resident((1, QKV_DIM)),
            _resident_layer((Q_DIM, D_MODEL), layer),
            _resident((1, D_MODEL)),
            pl.BlockSpec(memory_space=pltpu.SMEM),
        ],
        out_specs=(pl.BlockSpec((r, D_MODEL), rows),
                   pl.BlockSpec((1, WINDOW, KV_DIM), lambda i, j: (i, 0, 0)),
                   pl.BlockSpec((1, WINDOW, KV_DIM), lambda i, j: (i, 0, 0))),
        scratch_shapes=[pltpu.VMEM((2 * N_KV_HEADS, LANES, keys), BF16),
                        pltpu.VMEM((N_KV_HEADS, keys, 2 * LANES), BF16),
                        pltpu.VMEM((N_KV_HEADS, keys, 2 * LANES), BF16)],
        input_output_aliases={0: 0},
        compiler_params=pltpu.CompilerParams(
            dimension_semantics=("arbitrary", "arbitrary"), vmem_limit_bytes=VMEM_LIMIT),
        name="attn_prompt",
    )(x, cos, sin, g.reshape(1, D_MODEL), w_qkv, b_qkv.reshape(1, QKV_DIM), w_o,
      b_o.reshape(1, D_MODEL), sinks)


SAMPLE_KEYS = 2 * WINDOW
SAMPLE_UNROLL = 4


def _attn_sample_kernel(x_ref, ck_ref, cv_ref, cos_ref, sin_ref, g_ref, wqkv_ref, bqkv_ref, wo_ref, bo_ref,
                        sinkcol_ref, o_ref, kout_ref, vout_ref, q_s, kn_s, vn_s, kk_s, vv_s, oa_s, *, dt):
    bb = ck_ref.shape[0]
    w = WINDOW
    x = x_ref[...]
    u = _rms(x, g_ref[...]).astype(BF16)
    qkv = jnp.dot(u, wqkv_ref[...], preferred_element_type=F32) + bqkv_ref[...]
    cos, sin = cos_ref[...], sin_ref[...]
    lane_t = lax.broadcasted_iota(jnp.int32, (bb * dt, LANES), 1)
    first_half = (lane_t % HEAD_DIM) < (HEAD_DIM // 2)
    for c in range(Q_COLS):
        qc = _rope_col(qkv[:, c * LANES:(c + 1) * LANES], cos, sin, first_half)
        q_s[:, c * LANES:(c + 1) * LANES] = qc * QK_SCALE_LOG2
    for c in range(KV_COLS):
        kn_s[:, c * LANES:(c + 1) * LANES] = _rope_col(
            qkv[:, Q_DIM + c * LANES:Q_DIM + (c + 1) * LANES], cos, sin, first_half)
    vn_s[...] = qkv[:, Q_DIM + KV_DIM:]

    kk_s[:, w + dt:, :] = jnp.zeros((SAMPLE_UNROLL, SAMPLE_KEYS - w - dt, KV_DIM), BF16)
    vv_s[:, w + dt:, :] = jnp.zeros((SAMPLE_UNROLL, SAMPLE_KEYS - w - dt, KV_DIM), BF16)

    heads_per_col = N_HEADS // KV_COLS
    qcols_per_col = heads_per_col // HEADS_PER_VREG
    rows = heads_per_col * dt
    lane8 = lax.broadcasted_iota(jnp.int32, (dt, LANES), 1)
    lo8 = lane8 < HEAD_DIM
    t_row = lax.broadcasted_iota(jnp.int32, (rows, SAMPLE_KEYS), 0) % dt
    key = lax.broadcasted_iota(jnp.int32, (rows, SAMPLE_KEYS), 1)
    valid = (key > t_row) & (key <= t_row + w)

    def stage(b, slot):
        r0 = pl.multiple_of(b * dt, dt)
        knew = kn_s[pl.ds(r0, dt), :]
        vnew = vn_s[pl.ds(r0, dt), :]
        kout_ref[b, 0:w - dt, :] = ck_ref[b, dt:w, :]
        kout_ref[b, w - dt:w, :] = knew
        vout_ref[b, 0:w - dt, :] = cv_ref[b, dt:w, :]
        vout_ref[b, w - dt:w, :] = vnew
        kk_s[slot, 0:w, :] = ck_ref[b].astype(BF16)
        kk_s[slot, w:w + dt, :] = knew.astype(BF16)
        vv_s[slot, 0:w, :] = cv_ref[b].astype(BF16)
        vv_s[slot, w:w + dt, :] = vnew.astype(BF16)

    def scores(b, slot, c):
        r0 = pl.multiple_of(b * dt, dt)
        pieces = []
        for h8 in range(heads_per_col):
            qi = c * qcols_per_col + h8 // HEADS_PER_VREG
            qcol = q_s[pl.ds(r0, dt), qi * LANES:(qi + 1) * LANES]
            a, hb = h8 % HEADS_PER_VREG, h8 // GROUP
            src = qcol if a == hb else pltpu.roll(qcol, HEAD_DIM, axis=1)
            keep = lo8 if hb == 0 else jnp.logical_not(lo8)
            pieces.append(jnp.where(keep, src, 0.0))
        qop = jnp.concatenate(pieces, axis=0).astype(BF16)
        return lax.dot_general(qop, kk_s[slot, :, c * LANES:(c + 1) * LANES],
                               (((1,), (1,)), ((), ())), preferred_element_type=F32)

    def place(b, c, o):
        r0 = pl.multiple_of(b * dt, dt)
        for mm in range(qcols_per_col):
            halves = []
            for a in range(HEADS_PER_VREG):
                h8 = mm * HEADS_PER_VREG + a
                hb = h8 // GROUP
                piece = o[h8 * dt:(h8 + 1) * dt, :]
                halves.append(piece if a == hb else pltpu.roll(piece, HEAD_DIM, axis=1))
            oi = c * qcols_per_col + mm
            oa_s[pl.ds(r0, dt), oi * LANES:(oi + 1) * LANES] = jnp.where(lo8, halves[0], halves[1])

    def body(gidx, carry):
        work = [(gidx * SAMPLE_UNROLL + slot, slot, c) for slot in range(SAMPLE_UNROLL) for c in range(KV_COLS)]
        for slot in range(SAMPLE_UNROLL):
            stage(gidx * SAMPLE_UNROLL + slot, slot)
        ss = [scores(b, slot, c) for b, slot, c in work]
        sm = [_softmax_with_sink(jnp.where(valid, s, NEG_INF), sinkcol_ref[c * rows:(c + 1) * rows, :] * LOG2_E)
              for s, (b, slot, c) in zip(ss, work)]
        for (e, rinv), (b, slot, c) in zip(sm, work):
            o = jnp.dot(e, vv_s[slot, :, c * LANES:(c + 1) * LANES], preferred_element_type=F32) * rinv
            place(b, c, o)
        return carry

    lax.fori_loop(0, bb // SAMPLE_UNROLL, body, 0)

    o_ref[...] = x + jnp.dot(oa_s[...].astype(BF16), wo_ref[...], preferred_element_type=F32) + bo_ref[...]


def _attn_sample(x, cache_k, cache_v, g, w_qkv, b_qkv, w_o, b_o, sinks, layer, *, row0, dt, bb):
    b = cache_k.shape[0]
    rows = _row_blocks(row0, b, dt, bb, dt)
    rows1 = lambda i: rows(i, 0)
    cos, sin = _rope_tables(PAST_LEN + jnp.arange(dt))
    cos, sin = jnp.tile(cos, (bb, 1)), jnp.tile(sin, (bb, 1))
    sinkcol = jnp.repeat(sinks, dt).reshape(N_HEADS * dt, 1)
    n = bb * dt
    return pl.pallas_call(
        functools.partial(_attn_sample_kernel, dt=dt),
        out_shape=(jax.ShapeDtypeStruct(x.shape, F32),
                   jax.ShapeDtypeStruct((b, WINDOW, KV_DIM), F32),
                   jax.ShapeDtypeStruct((b, WINDOW, KV_DIM), F32)),
        grid=(b // bb,),
        in_specs=[
            pl.BlockSpec((n, D_MODEL), rows1),
            pl.BlockSpec((bb, WINDOW, KV_DIM), lambda i: (i, 0, 0)),
            pl.BlockSpec((bb, WINDOW, KV_DIM), lambda i: (i, 0, 0)),
            _resident((n, LANES)),
            _resident((n, LANES)),
            _resident((1, D_MODEL)),
            _resident_layer((D_MODEL, QKV_DIM), layer),
            _resident((1, QKV_DIM)),
            _resident_layer((Q_DIM, D_MODEL), layer),
            _resident((1, D_MODEL)),
            _resident((N_HEADS * dt, 1)),
        ],
        out_specs=(pl.BlockSpec((n, D_MODEL), rows1),
                   pl.BlockSpec((bb, WINDOW, KV_DIM), lambda i: (i, 0, 0)),
                   pl.BlockSpec((bb, WINDOW, KV_DIM), lambda i: (i, 0, 0))),
        scratch_shapes=[pltpu.VMEM((n, Q_DIM), F32),
                        pltpu.VMEM((n, KV_DIM), F32),
                        pltpu.VMEM((n, KV_DIM), F32),
                        pltpu.VMEM((SAMPLE_UNROLL, SAMPLE_KEYS, KV_DIM), BF16),
                        pltpu.VMEM((SAMPLE_UNROLL, SAMPLE_KEYS, KV_DIM), BF16),
                        pltpu.VMEM((n, Q_DIM), F32)],
        input_output_aliases={0: 0},
        compiler_params=pltpu.CompilerParams(
            dimension_semantics=("arbitrary",), vmem_limit_bytes=VMEM_LIMIT),
        name="attn_sample",
    )(x, cache_k, cache_v, cos, sin, g.reshape(1, D_MODEL), w_qkv, b_qkv.reshape(1, QKV_DIM), w_o,
      b_o.reshape(1, D_MODEL), sinkcol)


def kernel(x_prompt, x_sample, state_pool, cache_k, cache_v, norm_ffn1, ffn1_w_in, ffn1_w_out, norm_mix,
           norm_ffn2, ffn2_w_in, ffn2_w_out, pool_w, pool_scale, attn_w_qkv, attn_b_qkv, attn_w_o, attn_b_o,
           attn_sinks, norm_final):
    batch, seq, _ = x_prompt.shape
    dec_batch, dec_seq, _ = x_sample.shape
    depth = norm_ffn1.shape[0]
    n_mixers = 2
    w_in, w_out = ffn1_w_in, ffn1_w_out
    pw = pool_w.astype(BF16)
    wqkv, wo = attn_w_qkv.astype(BF16), attn_w_o.astype(BF16)
    n_attn = cache_k.shape[0]
    ck_t = jnp.transpose(cache_k, (0, 1, 3, 4, 2)).reshape(n_attn, dec_batch, KV_DIM, WINDOW)
    cv_t = jnp.transpose(cache_v, (0, 1, 3, 4, 2)).reshape(n_attn, dec_batch, KV_DIM, WINDOW)
    ck = cv = None
    state_t = jnp.swapaxes(state_pool, 1, 2)

    n_p, n_s = batch * seq, dec_batch * dec_seq
    x = (x_prompt.reshape(n_p, D_MODEL), x_sample.reshape(n_s, D_MODEL))
    pool_p, pool_s, kp_l, vp_l, ks_l, vs_l = [], [], [], [], [], []
    for i in range(depth):
        j = i // n_mixers
        is_attn = i % n_mixers == 1
        raw = 0 if i == 0 else None
        if is_attn:
            x, w_in, w_out, cv = _ffn(x, norm_ffn1[i], w_in, w_out, n_p, next_weights=(ffn2_w_in, ffn2_w_out, i),
                                      cache_t=(cv_t, j), raw_layer=raw, tm=FFN_TILE)
        else:
            x, w_in, w_out, state = _ffn(x, norm_ffn1[i], w_in, w_out, n_p, next_weights=(ffn2_w_in, ffn2_w_out, i),
                                         state_t=(state_t, j), raw_layer=raw, tm=FFN_TILE)
        if not is_attn:
            x, sp = _pool(x, None, norm_mix[i], pw, pool_scale[j], j, row0=0, n_seq=batch, t=seq,
                          start_pos=0, bb=1, tt=2048)
            x, ss = _pool(x, state, norm_mix[i], pw, pool_scale[j], j, row0=n_p, n_seq=dec_batch, t=dec_seq,
                          start_pos=PAST_LEN, bb=32, tt=dec_seq)
            pool_p.append(sp)
            pool_s.append(ss)
        else:
            x, kp, vp = _attn_prompt(x, norm_mix[i], wqkv, attn_b_qkv[j], wo, attn_b_o[j], attn_sinks[j], j,
                                     n_seq=batch, t=seq, nblk=ATTN_BLOCKS_PER_STEP)
            if ck is None:
                ck = jnp.swapaxes(ck_t[j], 1, 2)
            x, kn, vn = _attn_sample(x, ck, cv, norm_mix[i], wqkv, attn_b_qkv[j], wo, attn_b_o[j],
                                     attn_sinks[j], j, row0=n_p, dt=dec_seq, bb=32)
            ck = None
            kp_l.append(kp.reshape(batch, WINDOW, N_KV_HEADS, HEAD_DIM))
            vp_l.append(vp.reshape(batch, WINDOW, N_KV_HEADS, HEAD_DIM))
            ks_l.append(kn.reshape(dec_batch, WINDOW, N_KV_HEADS, HEAD_DIM))
            vs_l.append(vn.reshape(dec_batch, WINDOW, N_KV_HEADS, HEAD_DIM))
        if i == depth - 1:
            xp, xs = _ffn(x, norm_ffn2[i], w_in, w_out, n_p, final_gain=norm_final, split_out=True, tm=FFN_TILE)
        elif (i + 1) % n_mixers == 1:
            x, w_in, w_out, ck = _ffn(x, norm_ffn2[i], w_in, w_out, n_p,
                                      next_weights=(ffn1_w_in, ffn1_w_out, i + 1),
                                      cache_t=(ck_t, (i + 1) // n_mixers), tm=FFN_TILE)
        else:
            x, w_in, w_out = _ffn(x, norm_ffn2[i], w_in, w_out, n_p,
                                  next_weights=(ffn1_w_in, ffn1_w_out, i + 1), tm=FFN_TILE)
    return (xp.reshape(batch, seq, D_MODEL), xs.reshape(dec_batch, dec_seq, D_MODEL),
            jnp.stack(pool_p), jnp.stack(pool_s), jnp.stack(kp_l), jnp.stack(vp_l),
            jnp.stack(ks_l), jnp.stack(vs_l))
```

```python
import functools

import jax
import jax.numpy as jnp
from jax import lax
from jax.experimental import pallas as pl
from jax.experimental.pallas import tpu as pltpu

D_MODEL = 1024
D_FF = 2816
POOL_WINDOWS = (2, 4, 8, 16)
POOL_GROUP_DIM = D_MODEL // len(POOL_WINDOWS)
POOL_PREFIX = max(POOL_WINDOWS) - 1
N_HEADS = 16
N_KV_HEADS = 4
HEAD_DIM = 64
GROUP = N_HEADS // N_KV_HEADS
WINDOW = 128
ROPE_THETA = 10000.0
Q_DIM = N_HEADS * HEAD_DIM
KV_DIM = N_KV_HEADS * HEAD_DIM
QKV_DIM = Q_DIM + 2 * KV_DIM
RMS_EPS = 1e-6
NEG_INF = -1e30
PAST_LEN = 8192

LANES = 128
SUBLANES = 8
HEADS_PER_VREG = LANES // HEAD_DIM
Q_COLS = Q_DIM // LANES
KV_COLS = KV_DIM // LANES
POOL_HALO = 16
FF_CHUNK = 256
FFN_TILE = 512
FFN_SUB_ROWS = 256
RELAYOUT_SEQS = 4
RESTATE_SEQS = 8
WEIGHT_CAST_STEPS = 16
ATTN_BLOCKS_PER_STEP = 8
ATTN_PROJ_BLOCKS = 2
PROJ_PIECES = QKV_DIM // (2 * LANES)
PROJ_AT_GROUP = 3
QK_LOOKAHEAD = 3
LOG2_E = 1.4426950408889634
QK_SCALE_LOG2 = HEAD_DIM ** -0.5 * LOG2_E
VMEM_LIMIT = 52 * 1024 * 1024

F32 = jnp.float32
BF16 = jnp.bfloat16


def _rms(x, g):
    ms = jnp.mean(x * x, axis=-1, keepdims=True)
    return x * lax.rsqrt(ms + RMS_EPS) * g


def _resident(shape):
    nd = len(shape)
    return pl.BlockSpec(shape, lambda *_: (0,) * nd, pipeline_mode=pl.Buffered(1))


def _resident_layer(shape, layer):
    nd = len(shape)
    return pl.BlockSpec((None,) + shape, lambda *_: (layer,) + (0,) * nd, pipeline_mode=pl.Buffered(1))


def _zero_bits_of(v):
    bits = lax.bitcast_convert_type(v, jnp.uint32)
    return (bits >> 16) >> 16


def _after(v, zero_bits):
    reps = (v.shape[0] // zero_bits.shape[0], v.shape[1] // zero_bits.shape[1])
    bits = lax.bitcast_convert_type(v, jnp.uint32) | jnp.tile(zero_bits, reps)
    return lax.bitcast_convert_type(bits, F32)


def _ffn_kernel(*refs, n_sample_tiles, split_in, split_out, final_norm, convert_next, relayout, restate,
                raw_layer):
    refs = list(refs)
    if split_in:
        xp_ref, xs_ref = refs.pop(0), refs.pop(0)
    else:
        x_ref = refs.pop(0)
    g_ref, win_ref, wout_ref = refs[:3]
    del refs[:3]
    gf_ref = refs.pop(0) if final_norm else None
    if convert_next:
        nwin_ref, nwout_ref = refs.pop(0), refs.pop(0)
    if relayout:
        ri_ref = refs.pop(0)
    if restate:
        si_ref = refs.pop(0)
    if split_out:
        op_ref, os_ref = refs.pop(0), refs.pop(0)
    else:
        op_ref = refs.pop(0)
    if convert_next:
        cwin_ref, cwout_ref = refs.pop(0), refs.pop(0)
    if relayout:
        ro_ref = refs.pop(0)
    if restate:
        so_ref = refs.pop(0)
    h_ref = refs.pop(0)
    if raw_layer is not None:
        win_hbm, wout_hbm = win_ref, wout_ref
        win_ref, wout_ref, stage_in, stage_out, sem = refs
        slab_cols = stage_in.shape[2]
        slab_rows = stage_out.shape[1]
        n_slabs = 2 * D_FF // slab_cols

        def in_copy(c):
            return pltpu.make_async_copy(win_hbm.at[raw_layer, :, pl.ds(c * slab_cols, slab_cols)],
                                         stage_in.at[c % 2], sem.at[0, c % 2])

        def out_copy(c):
            return pltpu.make_async_copy(wout_hbm.at[raw_layer, pl.ds(c * slab_rows, slab_rows), :],
                                         stage_out.at[c % 2], sem.at[1, c % 2])

        @pl.when(pl.program_id(0) == 0)
        def _():
            in_copy(0).start()
            out_copy(0).start()
            for c in range(n_slabs):
                if c + 1 < n_slabs:
                    in_copy(c + 1).start()
                    out_copy(c + 1).start()
                in_copy(c).wait()
                win_ref[:, c * slab_cols:(c + 1) * slab_cols] = stage_in[c % 2].astype(BF16)
                out_copy(c).wait()
                wout_ref[c * slab_rows:(c + 1) * slab_rows, :] = stage_out[c % 2].astype(BF16)

    is_sample = pl.program_id(0) < n_sample_tiles
    x = jnp.where(is_sample, xs_ref[...], xp_ref[...]) if split_in else x_ref[...]
    xn = _rms(x, g_ref[...]).astype(BF16)
    tm = x.shape[0]
    n_chunks = D_FF // FF_CHUNK
    for c in range(n_chunks):
        lo = c * FF_CHUNK
        for r0 in range(0, tm, FFN_SUB_ROWS):
            xs_ = xn[r0:r0 + FFN_SUB_ROWS]
            gate = jnp.dot(xs_, win_ref[:, lo:lo + FF_CHUNK], preferred_element_type=F32)
            up = jnp.dot(xs_, win_ref[:, D_FF + lo:D_FF + lo + FF_CHUNK], preferred_element_type=F32)
            h_ref[r0:r0 + FFN_SUB_ROWS, lo:lo + FF_CHUNK] = (gate * jax.nn.sigmoid(gate) * up).astype(BF16)
        anchor = _zero_bits_of(gate[0:SUBLANES, 0:LANES])
        if convert_next:
            cw = 2 * D_FF // n_chunks
            cwin_ref[:, c * cw:(c + 1) * cw] = _after(nwin_ref[:, c * cw:(c + 1) * cw], anchor).astype(BF16)
            cr = cwout_ref.shape[0] // n_chunks
            cwout_ref[c * cr:(c + 1) * cr, :] = _after(nwout_ref[c * cr:(c + 1) * cr, :], anchor).astype(BF16)
        if relayout and c < ri_ref.shape[0]:
            ro_ref[c] = _after(ri_ref[c], anchor).T
        if restate:
            for p in range(c, POOL_PREFIX, n_chunks):
                so_ref[:, p, :] = _after(si_ref[p], anchor)
    ys = []
    for r0 in range(0, tm, FFN_SUB_ROWS):
        rows = slice(r0, r0 + FFN_SUB_ROWS)
        yr = x[rows] + 0.5 * jnp.dot(h_ref[rows, :], wout_ref[...], preferred_element_type=F32)
        ys.append(_rms(yr, gf_ref[...]) if final_norm else yr)
    y = jnp.concatenate(ys, axis=0)

    op_ref[...] = y
    if split_out:
        @pl.when(is_sample)
        def _():
            os_ref[...] = op_ref[...]


def _ffn(x, g, w_in, w_out, n_prompt_rows, final_gain=None, next_weights=None, cache_t=None, state_t=None, *,
         raw_layer=None, split_out=False, tm):
    split_in = isinstance(x, tuple)
    n_rows = sum(a.shape[0] for a in x) if split_in else x.shape[0]
    n_p, n_s = n_prompt_rows // tm, (n_rows - n_prompt_rows) // tm
    final_norm = final_gain is not None
    convert_next = next_weights is not None
    relayout = cache_t is not None
    restate = state_t is not None

    def sample_tile(i):
        return (jnp.minimum(i, n_s - 1), 0)

    def prompt_tile(i):
        return (jnp.maximum(i - n_s, 0), 0)

    def sample_first_tile(i):
        return (jnp.where(i < n_s, n_p + i, i - n_s), 0)

    def same_tile(i):
        return (i, 0)

    reordered = split_in or split_out
    combined_tile = sample_first_tile if reordered else same_tile
    if split_in:
        in_specs = [pl.BlockSpec((tm, D_MODEL), prompt_tile), pl.BlockSpec((tm, D_MODEL), sample_tile)]
        args = list(x)
    else:
        in_specs = [pl.BlockSpec((tm, D_MODEL), combined_tile)]
        args = [x]
    scratch_shapes = [pltpu.VMEM((tm, D_FF), BF16)]
    if raw_layer is None:
        in_specs += [_resident((1, D_MODEL)), _resident((D_MODEL, 2 * D_FF)), _resident((D_FF, D_MODEL))]
    else:
        in_specs += [_resident((1, D_MODEL)), pl.BlockSpec(memory_space=pl.ANY), pl.BlockSpec(memory_space=pl.ANY)]
        n_slabs = D_FF // FF_CHUNK
        scratch_shapes += [pltpu.VMEM((D_MODEL, 2 * D_FF), BF16), pltpu.VMEM((D_FF, D_MODEL), BF16),
                           pltpu.VMEM((2, D_MODEL, 2 * D_FF // n_slabs), F32),
                           pltpu.VMEM((2, D_FF // n_slabs, D_MODEL), F32),
                           pltpu.SemaphoreType.DMA((2, 2))]
    args += [g.reshape(1, D_MODEL), w_in, w_out]
    if split_out:
        out_shape = [jax.ShapeDtypeStruct((n_p * tm, D_MODEL), F32), jax.ShapeDtypeStruct((n_s * tm, D_MODEL), F32)]
        out_specs = [pl.BlockSpec((tm, D_MODEL), prompt_tile), pl.BlockSpec((tm, D_MODEL), sample_tile)]
    else:
        out_shape = [jax.ShapeDtypeStruct((n_rows, D_MODEL), F32)]
        out_specs = [pl.BlockSpec((tm, D_MODEL), combined_tile)]
    if final_norm:
        in_specs.append(_resident((1, D_MODEL)))
        args.append(final_gain.reshape(1, D_MODEL))
    if convert_next:
        nw_in, nw_out, layer = next_weights
        rows_in, rows_out = D_MODEL // WEIGHT_CAST_STEPS, D_FF // WEIGHT_CAST_STEPS

        def slab(i):
            return (jnp.minimum(i, WEIGHT_CAST_STEPS - 1), 0)

        in_specs += [pl.BlockSpec((None, rows_in, 2 * D_FF), lambda i: (layer,) + slab(i)),
                     pl.BlockSpec((None, rows_out, D_MODEL), lambda i: (layer,) + slab(i))]
        args += [nw_in, nw_out]
        out_shape += [jax.ShapeDtypeStruct((D_MODEL, 2 * D_FF), BF16), jax.ShapeDtypeStruct((D_FF, D_MODEL), BF16)]
        out_specs += [pl.BlockSpec((rows_in, 2 * D_FF), slab), pl.BlockSpec((rows_out, D_MODEL), slab)]
    if relayout:
        cache, cache_layer = cache_t
        n_seq = cache.shape[1]
        last_group = n_seq // RELAYOUT_SEQS - 1

        def seq_group(i):
            return (jnp.minimum(i, last_group), 0, 0)

        in_specs.append(pl.BlockSpec((None, RELAYOUT_SEQS, KV_DIM, WINDOW), lambda i: (cache_layer,) + seq_group(i)))
        args.append(cache)
        out_shape.append(jax.ShapeDtypeStruct((n_seq, WINDOW, KV_DIM), F32))
        out_specs.append(pl.BlockSpec((RELAYOUT_SEQS, WINDOW, KV_DIM), seq_group))
    if restate:
        state, state_layer = state_t
        n_seq = state.shape[2]
        last_state_group = n_seq // RESTATE_SEQS - 1

        def state_group(i):
            return jnp.minimum(i, last_state_group)

        in_specs.append(pl.BlockSpec((None, POOL_PREFIX, RESTATE_SEQS, D_MODEL),
                                     lambda i: (state_layer, 0, state_group(i), 0)))
        args.append(state)
        out_shape.append(jax.ShapeDtypeStruct((n_seq, POOL_PREFIX, D_MODEL), F32))
        out_specs.append(pl.BlockSpec((RESTATE_SEQS, POOL_PREFIX, D_MODEL), lambda i: (state_group(i), 0, 0)))
    return pl.pallas_call(
        functools.partial(_ffn_kernel, n_sample_tiles=n_s, split_in=split_in, split_out=split_out,
                          final_norm=final_norm, convert_next=convert_next, relayout=relayout, restate=restate,
                          raw_layer=raw_layer),
        out_shape=tuple(out_shape),
        grid=(n_p + n_s,),
        in_specs=in_specs,
        out_specs=tuple(out_specs),
        scratch_shapes=scratch_shapes,
        compiler_params=pltpu.CompilerParams(
            dimension_semantics=("arbitrary",), vmem_limit_bytes=VMEM_LIMIT),
        name="ffn_final" if final_norm else "ffn",
    )(*args)


def _pool_kernel(x_ref, *rest, has_prefix, start_pos, bb, tt):
    rest = list(rest)
    pre_ref = rest.pop(0) if has_prefix else None
    g_ref, w_ref, sc_ref, o_ref, st_ref, ext_ref = rest
    ti = pl.program_id(1)
    te = tt + POOL_HALO

    @pl.when(ti == 0)
    def _():
        ext_ref[:, 0:POOL_HALO, :] = jnp.zeros((bb, POOL_HALO, D_MODEL), F32)
        if has_prefix:
            ext_ref[:, POOL_HALO - POOL_PREFIX:POOL_HALO, :] = pre_ref[...]

    if tt >= POOL_HALO:
        @pl.when(ti > 0)
        def _():
            ext_ref[:, 0:POOL_HALO, :] = ext_ref[:, tt:te, :]

    x = x_ref[...].reshape(bb, tt, D_MODEL)
    u = _rms(x, g_ref[...])
    ext_ref[:, POOL_HALO:te, :] = u

    n_seen = start_pos + ti * tt + lax.broadcasted_iota(jnp.int32, (1, tt, 1), 1) + 1
    for gi, wg in enumerate(POOL_WINDOWS):
        sl = slice(gi * POOL_GROUP_DIM, (gi + 1) * POOL_GROUP_DIM)
        s = ext_ref[:, :, sl].reshape(bb * te, POOL_GROUP_DIM)
        span = 1
        while span < wg:
            s = s + pltpu.roll(s, span, axis=0)
            span *= 2
        s = s.reshape(bb, te, POOL_GROUP_DIM)[:, POOL_HALO:, :]
        cnt = jnp.minimum(n_seen, wg).astype(F32)
        p = s / cnt - u[:, :, sl]
        p2 = p.reshape(bb * tt, POOL_GROUP_DIM).astype(BF16)
        y = jnp.dot(p2, w_ref[gi], preferred_element_type=F32).reshape(bb, tt, POOL_GROUP_DIM)
        o_ref[:, sl] = (x[:, :, sl] + y * sc_ref[:, :, sl]).reshape(bb * tt, POOL_GROUP_DIM)

    @pl.when(ti == pl.num_programs(1) - 1)
    def _():
        st_ref[...] = ext_ref[:, te - POOL_PREFIX:te, :]


def _row_blocks(row0, n_seq, t, bb, tt):
    first, per_seq = row0 // (bb * tt), t // tt
    return lambda i, j: (first + i * per_seq + j, 0)


def _pool(x, prefix, g, w, scale, layer, *, row0, n_seq, t, start_pos, bb, tt):
    has_prefix = prefix is not None
    rows = _row_blocks(row0, n_seq, t, bb, tt)
    in_specs = [pl.BlockSpec((bb * tt, D_MODEL), rows)]
    args = [x]
    if has_prefix:
        in_specs.append(pl.BlockSpec((bb, POOL_PREFIX, D_MODEL), lambda i, j: (i, 0, 0)))
        args.append(prefix)
    in_specs += [
        _resident((1, 1, D_MODEL)),
        _resident_layer((len(POOL_WINDOWS), POOL_GROUP_DIM, POOL_GROUP_DIM), layer),
        _resident((1, 1, D_MODEL)),
    ]
    args += [g.reshape(1, 1, D_MODEL), w, scale.reshape(1, 1, D_MODEL)]
    return pl.pallas_call(
        functools.partial(_pool_kernel, has_prefix=has_prefix, start_pos=start_pos, bb=bb, tt=tt),
        out_shape=(jax.ShapeDtypeStruct(x.shape, F32),
                   jax.ShapeDtypeStruct((n_seq, POOL_PREFIX, D_MODEL), F32)),
        grid=(n_seq // bb, t // tt),
        in_specs=in_specs,
        out_specs=(pl.BlockSpec((bb * tt, D_MODEL), rows),
                   pl.BlockSpec((bb, POOL_PREFIX, D_MODEL), lambda i, j: (i, 0, 0))),
        scratch_shapes=[pltpu.VMEM((bb, tt + POOL_HALO, D_MODEL), F32)],
        input_output_aliases={0: 0},
        compiler_params=pltpu.CompilerParams(
            dimension_semantics=("arbitrary", "arbitrary"), vmem_limit_bytes=VMEM_LIMIT),
        name="pool_sample" if has_prefix else "pool_prompt",
    )(*args)


def _rope_tables(pos):
    half = HEAD_DIM // 2
    lane = jnp.arange(LANES)
    inv = ROPE_THETA ** (-(lane % half).astype(F32) / half)
    ang = pos.astype(F32)[:, None] * inv[None, :]
    sign = jnp.where(lane % HEAD_DIM < half, -1.0, 1.0).astype(F32)
    return jnp.cos(ang), jnp.sin(ang) * sign[None, :]


def _rope_col(xc, cos, sin, first_half):
    half = HEAD_DIM // 2
    partner = jnp.where(first_half, pltpu.roll(xc, LANES - half, axis=1), pltpu.roll(xc, half, axis=1))
    return xc * cos + partner * sin


def _head_half_variants(col, lo_half):
    sw = pltpu.roll(col, HEAD_DIM, axis=1)
    zero = jnp.zeros_like(col)
    return {
        (0, 0): jnp.where(lo_half, col, zero),
        (0, 1): jnp.where(lo_half, zero, sw),
        (1, 0): jnp.where(lo_half, sw, zero),
        (1, 1): jnp.where(lo_half, zero, col),
    }


def _softmax_with_sink(s, sink):
    m = jnp.maximum(jnp.max(s, axis=-1, keepdims=True), sink)
    e = jnp.exp2(s - m)
    denom = jnp.sum(e, axis=-1, keepdims=True) + jnp.exp2(sink - m)
    return e.astype(BF16), 1.0 / denom


def _attn_prompt_kernel(x_ref, cos_ref, sin_ref, g_ref, wqkv_ref, bqkv_ref, wo_ref, bo_ref, sink_ref,
                        o_ref, kout_ref, vout_ref, kt_ref, vlo_ref, vhi_ref, *, nblk):
    j = pl.program_id(1)
    w = WINDOW
    r = nblk * w

    @pl.when(j == 0)
    def _():
        kt_ref[...] = jnp.zeros(kt_ref.shape, BF16)
        vlo_ref[:, 0:w, 0:LANES] = jnp.zeros((N_KV_HEADS, w, LANES), BF16)
        vhi_ref[:, 0:w, 0:LANES] = jnp.zeros((N_KV_HEADS, w, LANES), BF16)
        keys = vlo_ref.shape[1]
        ones_lo = (lax.broadcasted_iota(jnp.int32, (N_KV_HEADS, keys, LANES), 2) < HEAD_DIM).astype(BF16)
        vlo_ref[:, :, LANES:] = ones_lo
        vhi_ref[:, :, LANES:] = 1 - ones_lo

    @pl.when(j > 0)
    def _():
        kt_ref[:, :, 0:w] = kt_ref[:, :, r:r + w]
        vlo_ref[:, 0:w, 0:LANES] = vlo_ref[:, r:r + w, 0:LANES]
        vhi_ref[:, 0:w, 0:LANES] = vhi_ref[:, r:r + w, 0:LANES]

    rc = ATTN_PROJ_BLOCKS * w
    n_chunks = nblk // ATTN_PROJ_BLOCKS
    lane = lax.broadcasted_iota(jnp.int32, (rc, LANES), 1)
    first_half = (lane % HEAD_DIM) < (HEAD_DIM // 2)
    lo_half = lane < HEAD_DIM
    lo_half_2w = lax.broadcasted_iota(jnp.int32, (2 * w, LANES), 1) < HEAD_DIM

    u_of, qcols_of = {}, {}

    def project_piece(ch, t):
        rows = slice(ch * rc, (ch + 1) * rc)
        if t == 0:
            u_of[ch] = _rms(x_ref[rows, :], g_ref[...]).astype(BF16)
            qcols_of[ch] = []
        cols = slice(t * 2 * LANES, (t + 1) * 2 * LANES)
        tile = jnp.dot(u_of[ch], wqkv_ref[:, cols], preferred_element_type=F32) + bqkv_ref[:, cols]
        halves = [tile[:, :LANES], tile[:, LANES:]]
        if t < Q_COLS // 2:
            cos, sin = cos_ref[rows, :], sin_ref[rows, :]
            for hc in halves:
                qcols_of[ch].append((_rope_col(hc, cos, sin, first_half) * QK_SCALE_LOG2).astype(BF16))
            return
        is_k = t == Q_COLS // 2
        if is_k:
            cos, sin = cos_ref[rows, :], sin_ref[rows, :]
            halves = [_rope_col(hc, cos, sin, first_half) for hc in halves]
        krows = slice(w + ch * rc, w + (ch + 1) * rc)
        for c, col in enumerate(halves):
            if ch == n_chunks - 1:
                out_ref = kout_ref if is_k else vout_ref
                out_ref[0, :, c * LANES:(c + 1) * LANES] = col[rc - w:]
            if is_k:
                col_t = col.T.astype(BF16)
                for hb in range(HEADS_PER_VREG):
                    kvh = c * HEADS_PER_VREG + hb
                    head_t = col_t[hb * HEAD_DIM:(hb + 1) * HEAD_DIM, :]
                    for a in range(HEADS_PER_VREG):
                        kt_ref[kvh * HEADS_PER_VREG + a, a * HEAD_DIM:(a + 1) * HEAD_DIM, krows] = head_t
            else:
                var = _head_half_variants(col, lo_half)
                for hb in range(HEADS_PER_VREG):
                    kvh = c * HEADS_PER_VREG + hb
                    vlo_ref[kvh, krows, 0:LANES] = var[(hb, 0)].astype(BF16)
                    vhi_ref[kvh, krows, 0:LANES] = var[(hb, 1)].astype(BF16)

    for t in range(PROJ_PIECES):
        project_piece(0, t)

    row = lax.broadcasted_iota(jnp.int32, (2 * w, 2 * w), 0) % w
    col = lax.broadcasted_iota(jnp.int32, (2 * w, 2 * w), 1)
    band = (col > row) & (col <= row + w)
    first_key = jnp.where(j > 0, 0, w)
    band_first = band & (col >= first_key)
    top_rows = lax.broadcasted_iota(jnp.int32, (2 * w, 1), 0) < w

    groups = [(ch, kvh, bl, a) for ch in range(n_chunks) for kvh in range(N_KV_HEADS)
              for bl in range(ATTN_PROJ_BLOCKS) for a in range(HEADS_PER_VREG)]

    def scores(ch, kvh, bl, a):
        qcols = qcols_of[ch]
        qrows = slice(bl * w, (bl + 1) * w)
        blk = ch * ATTN_PROJ_BLOCKS + bl
        qop = jnp.concatenate([qcols[2 * kvh][qrows], qcols[2 * kvh + 1][qrows]], axis=0)
        return jnp.dot(qop, kt_ref[kvh * HEADS_PER_VREG + a, :, blk * w:(blk + 2) * w],
                       preferred_element_type=F32)

    groups_per_chunk = len(groups) // n_chunks
    pending = [scores(*g) for g in groups[:QK_LOOKAHEAD]]
    es, sink_terms, o_parts, acc = [], [], [], None
    for gi, (ch, kvh, bl, a) in enumerate(groups):
        gl = gi % groups_per_chunk
        if 0 <= gl - PROJ_AT_GROUP < PROJ_PIECES and ch + 1 < n_chunks:
            project_piece(ch + 1, gl - PROJ_AT_GROUP)
        if gi + QK_LOOKAHEAD < len(groups):
            pending.append(scores(*groups[gi + QK_LOOKAHEAD]))
        blk = ch * ATTN_PROJ_BLOCKS + bl
        s = jnp.where(band_first if blk == 0 else band, pending[gi], NEG_INF)
        pending[gi] = None
        h0 = kvh * GROUP + a
        sink = jnp.where(top_rows, sink_ref[h0] * LOG2_E, sink_ref[h0 + HEADS_PER_VREG] * LOG2_E)
        m = jnp.maximum(jnp.max(s, axis=-1, keepdims=True), sink)
        es.append(jnp.exp2(s - m).astype(BF16))
        sink_terms.append(jnp.exp2(sink - m))
        if a < HEADS_PER_VREG - 1:
            continue
        krows = slice(blk * w, (blk + 2) * w)
        vstack = jnp.concatenate([vlo_ref[kvh, krows, :], vhi_ref[kvh, krows, :]], axis=0)
        res = jnp.dot(jnp.concatenate(es, axis=1), vstack, preferred_element_type=F32)
        den = res[:, LANES:] + jnp.where(lo_half_2w, sink_terms[0], sink_terms[1])
        o = res[:, :LANES] * (1.0 / den)
        o_parts.append(jnp.concatenate([o[:w], o[w:]], axis=1).astype(BF16))
        es, sink_terms = [], []
        if bl < ATTN_PROJ_BLOCKS - 1:
            continue
        part = jnp.dot(jnp.concatenate(o_parts, axis=0),
                       wo_ref[kvh * GROUP * HEAD_DIM:(kvh + 1) * GROUP * HEAD_DIM, :],
                       preferred_element_type=F32)
        o_parts = []
        acc = part if acc is None else acc + part
        if kvh == N_KV_HEADS - 1:
            rows = slice(ch * rc, (ch + 1) * rc)
            o_ref[rows, :] = x_ref[rows, :] + acc + bo_ref[...]
            acc = None


def _attn_prompt(x, g, w_qkv, b_qkv, w_o, b_o, sinks, layer, *, n_seq, t, nblk):
    b = n_seq
    r = nblk * WINDOW
    cos, sin = _rope_tables(jnp.arange(t))
    keys = (nblk + 1) * WINDOW
    rows = _row_blocks(0, n_seq, t, 1, r)
    return pl.pallas_call(
        functools.partial(_attn_prompt_kernel, nblk=nblk),
        out_shape=(jax.ShapeDtypeStruct(x.shape, F32),
                   jax.ShapeDtypeStruct((b, WINDOW, KV_DIM), F32),
                   jax.ShapeDtypeStruct((b, WINDOW, KV_DIM), F32)),
        grid=(b, t // r),
        in_specs=[
            pl.BlockSpec((r, D_MODEL), rows),
            pl.BlockSpec((r, LANES), lambda i, j: (j, 0)),
            pl.BlockSpec((r, LANES), lambda i, j: (j, 0)),
            _resident((1, D_MODEL)),
            _resident_layer((D_MODEL, QKV_DIM), layer),
            _resident((1, QKV_DIM)),
            _resident_layer((Q_DIM, D_MODEL), layer),
            _resident((1, D_MODEL)),
            pl.BlockSpec(memory_space=pltpu.SMEM),
        ],
        out_specs=(pl.BlockSpec((r, D_MODEL), rows),
                   pl.BlockSpec((1, WINDOW, KV_DIM), lambda i, j: (i, 0, 0)),
                   pl.BlockSpec((1, WINDOW, KV_DIM), lambda i, j: (i, 0, 0))),
        scratch_shapes=[pltpu.VMEM((2 * N_KV_HEADS, LANES, keys), BF16),
                        pltpu.VMEM((N_KV_HEADS, keys, 2 * LANES), BF16),
                        pltpu.VMEM((N_KV_HEADS, keys, 2 * LANES), BF16)],
        input_output_aliases={0: 0},
        compiler_params=pltpu.CompilerParams(
            dimension_semantics=("arbitrary", "arbitrary"), vmem_limit_bytes=VMEM_LIMIT),
        name="attn_prompt",
    )(x, cos, sin, g.reshape(1, D_MODEL), w_qkv, b_qkv.reshape(1, QKV_DIM), w_o,
      b_o.reshape(1, D_MODEL), sinks)


SAMPLE_KEYS = 2 * WINDOW
SAMPLE_UNROLL = 4


def _attn_sample_kernel(x_ref, ck_ref, cv_ref, cos_ref, sin_ref, g_ref, wqkv_ref, bqkv_ref, wo_ref, bo_ref,
                        sinkcol_ref, o_ref, kout_ref, vout_ref, q_s, kn_s, vn_s, kk_s, vv_s, oa_s, *, dt):
    bb = ck_ref.shape[0]
    w = WINDOW
    x = x_ref[...]
    u = _rms(x, g_ref[...]).astype(BF16)
    qkv = jnp.dot(u, wqkv_ref[...], preferred_element_type=F32) + bqkv_ref[...]
    cos, sin = cos_ref[...], sin_ref[...]
    lane_t = lax.broadcasted_iota(jnp.int32, (bb * dt, LANES), 1)
    first_half = (lane_t % HEAD_DIM) < (HEAD_DIM // 2)
    for c in range(Q_COLS):
        qc = _rope_col(qkv[:, c * LANES:(c + 1) * LANES], cos, sin, first_half)
        q_s[:, c * LANES:(c + 1) * LANES] = qc * QK_SCALE_LOG2
    for c in range(KV_COLS):
        kn_s[:, c * LANES:(c + 1) * LANES] = _rope_col(
            qkv[:, Q_DIM + c * LANES:Q_DIM + (c + 1) * LANES], cos, sin, first_half)
    vn_s[...] = qkv[:, Q_DIM + KV_DIM:]

    kk_s[:, w + dt:, :] = jnp.zeros((SAMPLE_UNROLL, SAMPLE_KEYS - w - dt, KV_DIM), BF16)
    vv_s[:, w + dt:, :] = jnp.zeros((SAMPLE_UNROLL, SAMPLE_KEYS - w - dt, KV_DIM), BF16)

    heads_per_col = N_HEADS // KV_COLS
    qcols_per_col = heads_per_col // HEADS_PER_VREG
    rows = heads_per_col * dt
    lane8 = lax.broadcasted_iota(jnp.int32, (dt, LANES), 1)
    lo8 = lane8 < HEAD_DIM
    t_row = lax.broadcasted_iota(jnp.int32, (rows, SAMPLE_KEYS), 0) % dt
    key = lax.broadcasted_iota(jnp.int32, (rows, SAMPLE_KEYS), 1)
    valid = (key > t_row) & (key <= t_row + w)

    def stage(b, slot):
        r0 = pl.multiple_of(b * dt, dt)
        knew = kn_s[pl.ds(r0, dt), :]
        vnew = vn_s[pl.ds(r0, dt), :]
        kout_ref[b, 0:w - dt, :] = ck_ref[b, dt:w, :]
        kout_ref[b, w - dt:w, :] = knew
        vout_ref[b, 0:w - dt, :] = cv_ref[b, dt:w, :]
        vout_ref[b, w - dt:w, :] = vnew
        kk_s[slot, 0:w, :] = ck_ref[b].astype(BF16)
        kk_s[slot, w:w + dt, :] = knew.astype(BF16)
        vv_s[slot, 0:w, :] = cv_ref[b].astype(BF16)
        vv_s[slot, w:w + dt, :] = vnew.astype(BF16)

    def scores(b, slot, c):
        r0 = pl.multiple_of(b * dt, dt)
        pieces = []
        for h8 in range(heads_per_col):
            qi = c * qcols_per_col + h8 // HEADS_PER_VREG
            qcol = q_s[pl.ds(r0, dt), qi * LANES:(qi + 1) * LANES]
            a, hb = h8 % HEADS_PER_VREG, h8 // GROUP
            src = qcol if a == hb else pltpu.roll(qcol, HEAD_DIM, axis=1)
            keep = lo8 if hb == 0 else jnp.logical_not(lo8)
            pieces.append(jnp.where(keep, src, 0.0))
        qop = jnp.concatenate(pieces, axis=0).astype(BF16)
        return lax.dot_general(qop, kk_s[slot, :, c * LANES:(c + 1) * LANES],
                               (((1,), (1,)), ((), ())), preferred_element_type=F32)

    def place(b, c, o):
        r0 = pl.multiple_of(b * dt, dt)
        for mm in range(qcols_per_col):
            halves = []
            for a in range(HEADS_PER_VREG):
                h8 = mm * HEADS_PER_VREG + a
                hb = h8 // GROUP
                piece = o[h8 * dt:(h8 + 1) * dt, :]
                halves.append(piece if a == hb else pltpu.roll(piece, HEAD_DIM, axis=1))
            oi = c * qcols_per_col + mm
            oa_s[pl.ds(r0, dt), oi * LANES:(oi + 1) * LANES] = jnp.where(lo8, halves[0], halves[1])

    def body(gidx, carry):
        work = [(gidx * SAMPLE_UNROLL + slot, slot, c) for slot in range(SAMPLE_UNROLL) for c in range(KV_COLS)]
        for slot in range(SAMPLE_UNROLL):
            stage(gidx * SAMPLE_UNROLL + slot, slot)
        ss = [scores(b, slot, c) for b, slot, c in work]
        sm = [_softmax_with_sink(jnp.where(valid, s, NEG_INF), sinkcol_ref[c * rows:(c + 1) * rows, :] * LOG2_E)
              for s, (b, slot, c) in zip(ss, work)]
        for (e, rinv), (b, slot, c) in zip(sm, work):
            o = jnp.dot(e, vv_s[slot, :, c * LANES:(c + 1) * LANES], preferred_element_type=F32) * rinv
            place(b, c, o)
        return carry

    lax.fori_loop(0, bb // SAMPLE_UNROLL, body, 0)

    o_ref[...] = x + jnp.dot(oa_s[...].astype(BF16), wo_ref[...], preferred_element_type=F32) + bo_ref[...]


def _attn_sample(x, cache_k, cache_v, g, w_qkv, b_qkv, w_o, b_o, sinks, layer, *, row0, dt, bb):
    b = cache_k.shape[0]
    rows = _row_blocks(row0, b, dt, bb, dt)
    rows1 = lambda i: rows(i, 0)
    cos, sin = _rope_tables(PAST_LEN + jnp.arange(dt))
    cos, sin = jnp.tile(cos, (bb, 1)), jnp.tile(sin, (bb, 1))
    sinkcol = jnp.repeat(sinks, dt).reshape(N_HEADS * dt, 1)
    n = bb * dt
    return pl.pallas_call(
        functools.partial(_attn_sample_kernel, dt=dt),
        out_shape=(jax.ShapeDtypeStruct(x.shape, F32),
                   jax.ShapeDtypeStruct((b, WINDOW, KV_DIM), F32),
                   jax.ShapeDtypeStruct((b, WINDOW, KV_DIM), F32)),
        grid=(b // bb,),
        in_specs=[
            pl.BlockSpec((n, D_MODEL), rows1),
            pl.BlockSpec((bb, WINDOW, KV_DIM), lambda i: (i, 0, 0)),
            pl.BlockSpec((bb, WINDOW, KV_DIM), lambda i: (i, 0, 0)),
            _resident((n, LANES)),
            _resident((n, LANES)),
            _resident((1, D_MODEL)),
            _resident_layer((D_MODEL, QKV_DIM), layer),
            _resident((1, QKV_DIM)),
            _resident_layer((Q_DIM, D_MODEL), layer),
            _resident((1, D_MODEL)),
            _resident((N_HEADS * dt, 1)),
        ],
        out_specs=(pl.BlockSpec((n, D_MODEL), rows1),
                   pl.BlockSpec((bb, WINDOW, KV_DIM), lambda i: (i, 0, 0)),
                   pl.BlockSpec((bb, WINDOW, KV_DIM), lambda i: (i, 0, 0))),
        scratch_shapes=[pltpu.VMEM((n, Q_DIM), F32),
                        pltpu.VMEM((n, KV_DIM), F32),
                        pltpu.VMEM((n, KV_DIM), F32),
                        pltpu.VMEM((SAMPLE_UNROLL, SAMPLE_KEYS, KV_DIM), BF16),
                        pltpu.VMEM((SAMPLE_UNROLL, SAMPLE_KEYS, KV_DIM), BF16),
                        pltpu.VMEM((n, Q_DIM), F32)],
        input_output_aliases={0: 0},
        compiler_params=pltpu.CompilerParams(
            dimension_semantics=("arbitrary",), vmem_limit_bytes=VMEM_LIMIT),
        name="attn_sample",
    )(x, cache_k, cache_v, cos, sin, g.reshape(1, D_MODEL), w_qkv, b_qkv.reshape(1, QKV_DIM), w_o,
      b_o.reshape(1, D_MODEL), sinkcol)


def kernel(x_prompt, x_sample, state_pool, cache_k, cache_v, norm_ffn1, ffn1_w_in, ffn1_w_out, norm_mix,
           norm_ffn2, ffn2_w_in, ffn2_w_out, pool_w, pool_scale, attn_w_qkv, attn_b_qkv, attn_w_o, attn_b_o,
           attn_sinks, norm_final):
    batch, seq, _ = x_prompt.shape
    dec_batch, dec_seq, _ = x_sample.shape
    depth = norm_ffn1.shape[0]
    n_mixers = 2
    w_in, w_out = ffn1_w_in, ffn1_w_out
    pw = pool_w.astype(BF16)
    wqkv, wo = attn_w_qkv.astype(BF16), attn_w_o.astype(BF16)
    n_attn = cache_k.shape[0]
    ck_t = jnp.transpose(cache_k, (0, 1, 3, 4, 2)).reshape(n_attn, dec_batch, KV_DIM, WINDOW)
    cv_t = jnp.transpose(cache_v, (0, 1, 3, 4, 2)).reshape(n_attn, dec_batch, KV_DIM, WINDOW)
    ck = cv = None
    state_t = jnp.swapaxes(state_pool, 1, 2)

    n_p, n_s = batch * seq, dec_batch * dec_seq
    x = (x_prompt.reshape(n_p, D_MODEL), x_sample.reshape(n_s, D_MODEL))
    pool_p, pool_s, kp_l, vp_l, ks_l, vs_l = [], [], [], [], [], []
    for i in range(depth):
        j = i // n_mixers
        is_attn = i % n_mixers == 1
        raw = 0 if i == 0 else None
        if is_attn:
            x, w_in, w_out, cv = _ffn(x, norm_ffn1[i], w_in, w_out, n_p, next_weights=(ffn2_w_in, ffn2_w_out, i),
                                      cache_t=(cv_t, j), raw_layer=raw, tm=FFN_TILE)
        else:
            x, w_in, w_out, state = _ffn(x, norm_ffn1[i], w_in, w_out, n_p, next_weights=(ffn2_w_in, ffn2_w_out, i),
                                         state_t=(state_t, j), raw_layer=raw, tm=FFN_TILE)
        if not is_attn:
            x, sp = _pool(x, None, norm_mix[i], pw, pool_scale[j], j, row0=0, n_seq=batch, t=seq,
                          start_pos=0, bb=1, tt=2048)
            x, ss = _pool(x, state, norm_mix[i], pw, pool_scale[j], j, row0=n_p, n_seq=dec_batch, t=dec_seq,
                          start_pos=PAST_LEN, bb=32, tt=dec_seq)
            pool_p.append(sp)
            pool_s.append(ss)
        else:
            x, kp, vp = _attn_prompt(x, norm_mix[i], wqkv, attn_b_qkv[j], wo, attn_b_o[j], attn_sinks[j], j,
                                     n_seq=batch, t=seq, nblk=ATTN_BLOCKS_PER_STEP)
            if ck is None:
                ck = jnp.swapaxes(ck_t[j], 1, 2)
            x, kn, vn = _attn_sample(x, ck, cv, norm_mix[i], wqkv, attn_b_qkv[j], wo, attn_b_o[j],
                                     attn_sinks[j], j, row0=n_p, dt=dec_seq, bb=32)
            ck = None
            kp_l.append(kp.reshape(batch, WINDOW, N_KV_HEADS, HEAD_DIM))
            vp_l.append(vp.reshape(batch, WINDOW, N_KV_HEADS, HEAD_DIM))
            ks_l.append(kn.reshape(dec_batch, WINDOW, N_KV_HEADS, HEAD_DIM))
            vs_l.append(vn.reshape(dec_batch, WINDOW, N_KV_HEADS, HEAD_DIM))
        if i == depth - 1:
            xp, xs = _ffn(x, norm_ffn2[i], w_in, w_out, n_p, final_gain=norm_final, split_out=True, tm=FFN_TILE)
        elif (i + 1) % n_mixers == 1:
            x, w_in, w_out, ck = _ffn(x, norm_ffn2[i], w_in, w_out, n_p,
                                      next_weights=(ffn1_w_in, ffn1_w_out, i + 1),
                                      cache_t=(ck_t, (i + 1) // n_mixers), tm=FFN_TILE)
        else:
            x, w_in, w_out = _ffn(x, norm_ffn2[i], w_in, w_out, n_p,
                                  next_weights=(ffn1_w_in, ffn1_w_out, i + 1), tm=FFN_TILE)
    return (xp.reshape(batch, seq, D_MODEL), xs.reshape(dec_batch, dec_seq, D_MODEL),
            jnp.stack(pool_p), jnp.stack(pool_s), jnp.stack(kp_l), jnp.stack(vp_l),
            jnp.stack(ks_l), jnp.stack(vs_l))
```
